```python
import jax, jax.numpy as jnp
from jax import lax
import numpy as np

D_MODEL = 1024
BATCH = 8
SEQ = 8192
DEPTH = 2

RET_HEAD_DIM = 128
RET_WIDTH = D_MODEL // 2
RET_HEADS = RET_WIDTH // RET_HEAD_DIM
HGRN_HEAD_DIM = 128
HGRN_WIDTH = D_MODEL - RET_WIDTH
HGRN_HEADS = HGRN_WIDTH // HGRN_HEAD_DIM
MIX_WIDTH = RET_WIDTH + HGRN_WIDTH
IN_WIDTHS = [RET_WIDTH] * 4 + [HGRN_WIDTH] * 5
IN_COLS = sum(IN_WIDTHS)
RET_CHUNK = 128
HGRN_CHUNK = 64
ROPE_BASE = 10000.0
N_EXPERTS = 16
CAPACITY_FACTOR = 2
EXPERT_FF = 1024
NORM_EPS = 1e-6

kernel_name = "hymba_style_retention_hgrn2_expert_choice_encoder"


def rms_normalize(x):
    xf = x.astype(jnp.float32)
    return (xf * lax.rsqrt(jnp.mean(xf * xf, axis=-1, keepdims=True) + NORM_EPS)).astype(x.dtype)


def rms_norm(x, g):
    return rms_normalize(x) * g.astype(x.dtype)


def to_heads(t, n_heads):
    b, s, _ = t.shape
    return t.reshape(b, s, n_heads, -1).transpose(0, 2, 1, 3)


def from_heads(t):
    b, h, s, d = t.shape
    return t.transpose(0, 2, 1, 3).reshape(b, s, h * d)


def to_chunks(t, chunk):
    b, h, s, d = t.shape
    return t.reshape(b, h, s // chunk, chunk, d).transpose(2, 0, 1, 3, 4)


def from_chunks(t):
    n, b, h, c, d = t.shape
    return t.transpose(1, 2, 0, 3, 4).reshape(b, h, n * c, d)


def rotary(x, pos):
    half = x.shape[-1] // 2
    inv_freq = ROPE_BASE ** (-jnp.arange(half, dtype=jnp.float32) / half)
    ang = pos.astype(jnp.float32)[:, None] * inv_freq[None, :]
    cos = jnp.cos(ang).astype(x.dtype)
    sin = jnp.sin(ang).astype(x.dtype)
    x1, x2 = x[..., :half], x[..., half:]
    return jnp.concatenate([x1 * cos - x2 * sin, x1 * sin + x2 * cos], axis=-1)


def retention_scan(q, k, v, log_gamma, include_diag):
    b, h, s, dk = q.shape
    dv = v.shape[-1]
    dt = q.dtype
    L = RET_CHUNK
    t = jnp.arange(L, dtype=jnp.float32)
    diff = t[:, None] - t[None, :]
    mask = (diff >= 0) if include_diag else (diff > 0)
    lg = log_gamma[:, None, None]
    intra_decay = jnp.where(mask, jnp.exp(lg * jnp.maximum(diff, 0.0)), 0.0).astype(dt)
    q_decay = jnp.exp(log_gamma[:, None] * (t + 1.0))[:, :, None].astype(dt)
    k_decay = jnp.exp(log_gamma[:, None] * (L - 1.0 - t))[:, :, None].astype(dt)
    chunk_decay = jnp.exp(log_gamma * L)[:, None, None].astype(dt)

    def step(state, inp):
        qc, kc, vc = inp
        scores = jnp.einsum('bhid,bhjd->bhij', qc, kc) * intra_decay
        out = (jnp.einsum('bhij,bhjv->bhiv', scores, vc)
               + jnp.einsum('bhid,bhdv->bhiv', qc * q_decay, state))
        state = chunk_decay * state + jnp.einsum('bhjd,bhjv->bhdv', kc * k_decay, vc)
        return state, out

    state0 = jnp.zeros((b, h, dk, dv), dt)
    _, out = lax.scan(step, state0, (to_chunks(q, L), to_chunks(k, L), to_chunks(v, L)))
    return from_chunks(out)


def hgrn2_scan(q, k, v, log_f):
    b, h, s, dk = q.shape
    dv = v.shape[-1]
    dt = q.dtype
    L = HGRN_CHUNK
    causal = jnp.tril(jnp.ones((L, L), dtype=bool))[:, :, None]

    def step(state, inp):
        qc, kc, vc, gc = inp
        cum = jnp.cumsum(gc, axis=2)
        rel = jnp.exp(jnp.where(causal, cum[:, :, :, None, :] - cum[:, :, None, :, :], -jnp.inf)).astype(dt)
        scores = jnp.einsum('bhtd,bhsd,bhtsd->bhts', qc, kc, rel)
        out = (jnp.einsum('bhts,bhsv->bhtv', scores, vc)
               + jnp.einsum('bhtd,bhdv->bhtv', qc * jnp.exp(cum).astype(dt), state))
        cum_end = cum[:, :, -1, :]
        k_to_end = kc * jnp.exp(cum_end[:, :, None, :] - cum).astype(dt)
        state = (jnp.exp(cum_end).astype(dt)[..., None] * state
                 + jnp.einsum('bhsd,bhsv->bhdv', k_to_end, vc))
        return state, out

    state0 = jnp.zeros((b, h, dk, dv), dt)
    _, out = lax.scan(step, state0, (to_chunks(q, L), to_chunks(k, L), to_chunks(v, L), to_chunks(log_f, L)))
    return from_chunks(out)


def flip_seq(t):
    return jnp.flip(t, axis=2)


def hybrid_mixer(xn, w_in, ret_g, hgrn_g, w_out, lb):
    b, s, _ = xn.shape
    dt = xn.dtype
    proj = xn @ w_in
    splits = np.cumsum(IN_WIDTHS)[:-1].tolist()
    rq, rk, rv, rg, hq, hf_fwd, hf_bwd, hi, hg = jnp.split(proj, splits, axis=-1)

    pos = jnp.arange(s)
    q = rotary(to_heads(rq, RET_HEADS), pos)
    k = rotary(to_heads(rk, RET_HEADS), pos) * (RET_HEAD_DIM ** -0.5)
    v = to_heads(rv, RET_HEADS)
    log_gamma = jnp.log1p(-(2.0 ** (-5.0 - jnp.arange(RET_HEADS, dtype=jnp.float32))))
    r_out = (retention_scan(q, k, v, log_gamma, True)
             + flip_seq(retention_scan(flip_seq(q), flip_seq(k), flip_seq(v), log_gamma, False)))
    r_out = from_heads(rms_normalize(r_out)) * ret_g.astype(dt) * jax.nn.silu(rg)

    q2 = jax.nn.silu(to_heads(hq, HGRN_HEADS)) * (HGRN_HEAD_DIM ** -0.5)
    v2 = to_heads(hi, HGRN_HEADS)

    def forget(z):
        z = z.astype(jnp.float32)
        if lb is None:
            return jax.nn.log_sigmoid(z), jax.nn.sigmoid(-z).astype(dt)
        lbh = lb.astype(jnp.float32).reshape(HGRN_HEADS, 1, HGRN_HEAD_DIM)
        f = lbh + (1.0 - lbh) * jax.nn.sigmoid(z)
        return jnp.log(f), ((1.0 - lbh) * jax.nn.sigmoid(-z)).astype(dt)

    logf_f, k_f = forget(to_heads(hf_fwd, HGRN_HEADS))
    logf_b, k_b = forget(to_heads(hf_bwd, HGRN_HEADS))
    h_out = (hgrn2_scan(q2, k_f, v2, logf_f)
             + flip_seq(hgrn2_scan(flip_seq(q2), flip_seq(k_b), flip_seq(v2), flip_seq(logf_b))))
    h_out = rms_norm(from_heads(h_out), hgrn_g) * jax.nn.silu(hg)

    return jnp.concatenate([r_out, h_out], axis=-1) @ w_out


def expert_choice_ffn(xn, w_router, w_gate, w_up, w_down):
    b, s, d = xn.shape
    cap = CAPACITY_FACTOR * s // N_EXPERTS
    affinity = jax.nn.softmax((xn @ w_router).astype(jnp.float32), axis=-1)
    gate, idx = lax.top_k(jnp.swapaxes(affinity, 1, 2), cap)
    xe = jax.vmap(lambda xb, ib: xb[ib])(xn, idx)
    hid = (jax.nn.silu(jnp.einsum('becd,edf->becf', xe, w_gate))
           * jnp.einsum('becd,edf->becf', xe, w_up))
    ye = jnp.einsum('becf,efd->becd', hid, w_down) * gate[..., None].astype(xn.dtype)
    return jax.vmap(lambda yb, ib: jnp.zeros((s, d), yb.dtype).at[ib.reshape(-1)].add(yb.reshape(-1, d)))(ye, idx)


def setup_inputs(seed: int = 0) -> dict:
    key = jax.random.key(seed)
    ks = jax.random.split(key, 14)
    f32 = jnp.float32
    x = jax.random.normal(ks[0], (BATCH, SEQ, D_MODEL), f32)
    norm1_g = 1.0 + 0.02 * jax.random.normal(ks[1], (DEPTH, D_MODEL), f32)
    w_in = jax.random.normal(ks[2], (DEPTH, D_MODEL, IN_COLS), f32) * D_MODEL ** -0.5
    ret_norm_g = 1.0 + 0.02 * jax.random.normal(ks[3], (DEPTH, RET_WIDTH), f32)
    hgrn_norm_g = 1.0 + 0.02 * jax.random.normal(ks[4], (DEPTH, HGRN_WIDTH), f32)
    w_out = jax.random.normal(ks[5], (DEPTH, MIX_WIDTH, D_MODEL), f32) * MIX_WIDTH ** -0.5
    lower_bounds = 0.1 * jax.random.normal(ks[6], (DEPTH, HGRN_WIDTH), f32)
    norm2_g = 1.0 + 0.02 * jax.random.normal(ks[7], (DEPTH, D_MODEL), f32)
    w_router = jax.random.normal(ks[8], (DEPTH, D_MODEL, N_EXPERTS), f32) * D_MODEL ** -0.5
    w_gate = jax.random.normal(ks[9], (DEPTH, N_EXPERTS, D_MODEL, EXPERT_FF), f32) * D_MODEL ** -0.5
    w_up = jax.random.normal(ks[10], (DEPTH, N_EXPERTS, D_MODEL, EXPERT_FF), f32) * D_MODEL ** -0.5
    w_down = jax.random.normal(ks[11], (DEPTH, N_EXPERTS, EXPERT_FF, D_MODEL), f32) * EXPERT_FF ** -0.5
    final_norm_g = 1.0 + 0.02 * jax.random.normal(ks[12], (D_MODEL,), f32)
    return {"x": x, "norm1_g": norm1_g, "w_in": w_in, "ret_norm_g": ret_norm_g,
            "hgrn_norm_g": hgrn_norm_g, "w_out": w_out, "lower_bounds": lower_bounds,
            "norm2_g": norm2_g, "w_router": w_router, "w_gate": w_gate, "w_up": w_up,
            "w_down": w_down, "final_norm_g": final_norm_g}


def reference(x, norm1_g, w_in, ret_norm_g, hgrn_norm_g, w_out, lower_bounds,
              norm2_g, w_router, w_gate, w_up, w_down, final_norm_g):
    lbs = jax.nn.softmax(lower_bounds.astype(jnp.float32), axis=0)
    lbs = jnp.cumsum(lbs, axis=0) - lbs[0]
    h = x
    for layer in range(DEPTH):
        lb = lbs[layer] if layer > 0 else None
        h = h + hybrid_mixer(rms_norm(h, norm1_g[layer]), w_in[layer], ret_norm_g[layer],
                             hgrn_norm_g[layer], w_out[layer], lb)
        h = h + expert_choice_ffn(rms_norm(h, norm2_g[layer]), w_router[layer],
                                  w_gate[layer], w_up[layer], w_down[layer])
    return rms_norm(h, final_norm_g)
```

```python
import functools

import numpy as np
import jax
import jax.numpy as jnp
from jax import lax
from jax.experimental import pallas as pl
from jax.experimental.pallas import tpu as pltpu

F32 = jnp.float32
BF16 = jnp.bfloat16

HEAD_DIM = 128
N_HEADS = 4
GROUP = N_HEADS * HEAD_DIM
N_GROUPS = 9
ROPE_BASE = 10000.0
NORM_EPS = 1e-6
CAPACITY_FACTOR = 2

RET_CHUNK = 256
HGRN_CHUNK = 128
VMEM_LIMIT = 58 * 1024 * 1024


def _cparams(n_axes):
    return pltpu.CompilerParams(
        dimension_semantics=("arbitrary",) * n_axes, vmem_limit_bytes=VMEM_LIMIT)


def _dot(a, b):
    return jnp.dot(a, b, preferred_element_type=F32)


def _dot_nt(a, b):
    return lax.dot_general(a, b, (((1,), (1,)), ((), ())), preferred_element_type=F32)


def _dot_tn(a, b):
    return lax.dot_general(a, b, (((0,), (0,)), ((), ())), preferred_element_type=F32)


def _silu(x):
    return x * (1.0 / (1.0 + jnp.exp(-x)))


def _rms(x):
    return x * lax.rsqrt(jnp.mean(x * x, axis=-1, keepdims=True) + NORM_EPS)


def _inproj_kernel(h_ref, g_ref, w_ref, cos_ref, sin_ref, out_ref):
    xn = (_rms(h_ref[...]) * g_ref[...]).astype(BF16)
    cos = cos_ref[...]
    sin = sin_ref[...]
    scale = HEAD_DIM ** -0.5
    for j in range(N_GROUPS):
        acc = _dot(xn, w_ref[:, j * GROUP:(j + 1) * GROUP])
        if j in (0, 1):
            for hh in range(N_HEADS):
                sl = acc[:, hh * HEAD_DIM:(hh + 1) * HEAD_DIM]
                rot = sl * cos + pltpu.roll(sl, HEAD_DIM // 2, 1) * sin
                if j == 1:
                    rot = rot * scale
                out_ref[:, j * GROUP + hh * HEAD_DIM:j * GROUP + (hh + 1) * HEAD_DIM] = rot.astype(BF16)
            continue
        if j in (3, 8):
            acc = _silu(acc)
        elif j == 4:
            acc = _silu(acc) * scale
        out_ref[:, j * GROUP:(j + 1) * GROUP] = acc.astype(BF16)


def _inproj(h, g, w_bf, cos, sin, tm):
    B, S, D = h.shape
    ncol = w_bf.shape[1]
    return pl.pallas_call(
        _inproj_kernel,
        grid=(B, S // tm),
        in_specs=[
            pl.BlockSpec((None, tm, D), lambda b, i: (b, i, 0)),
            pl.BlockSpec((1, D), lambda b, i: (0, 0)),
            pl.BlockSpec((D, ncol), lambda b, i: (0, 0), pipeline_mode=pl.Buffered(1)),
            pl.BlockSpec((tm, HEAD_DIM), lambda b, i: (i, 0)),
            pl.BlockSpec((tm, HEAD_DIM), lambda b, i: (i, 0)),
        ],
        out_specs=pl.BlockSpec((None, tm, ncol), lambda b, i: (b, i, 0)),
        out_shape=jax.ShapeDtypeStruct((B, S, ncol), BF16),
        compiler_params=_cparams(2),
        name="inproj",
    )(h, g.reshape(1, D), w_bf, cos, sin)


def _ret_kernel(q_ref, k_ref, v_ref, g_ref, dmat_ref, qdec_ref, kdec_ref, cdec_ref, rg_ref,
                out_ref, acc_ref, *, n_chunks):
    L = RET_CHUNK
    cdec = cdec_ref[0:1, :]
    zero_state = jnp.zeros((HEAD_DIM, HEAD_DIM), F32)

    def rows(i):
        return pl.ds(pl.multiple_of(i * L, L), L)

    def fwd(i, state):
        sl = rows(i)
        q, k, v = q_ref[sl, :], k_ref[sl, :], v_ref[sl, :]
        s = (_dot_nt(q, k) * dmat_ref[...]).astype(BF16)
        qd = (q.astype(F32) * qdec_ref[...]).astype(BF16)
        acc_ref[sl, :] = _dot(s, v) + _dot(qd, state.astype(BF16))
        kd = (k.astype(F32) * kdec_ref[...]).astype(BF16)
        return cdec * state + _dot_tn(kd, v)

    lax.fori_loop(0, n_chunks, fwd, zero_state)

    def bwd(ii, state):
        sl = rows(n_chunks - 1 - ii)
        q, k, v = q_ref[sl, :], k_ref[sl, :], v_ref[sl, :]
        qd = (q.astype(F32) * kdec_ref[...]).astype(BF16)
        o = acc_ref[sl, :] + _dot(qd, state.astype(BF16))
        y = _rms(o) * rg_ref[...] * g_ref[sl, :].astype(F32)
        out_ref[sl, :] = y.astype(BF16)
        kd = (k.astype(F32) * qdec_ref[...]).astype(BF16)
        return cdec * state + _dot_tn(kd, v)

    lax.fori_loop(0, n_chunks, bwd, zero_state)


def _ret_tables():
    L = RET_CHUNK
    t = np.arange(L, dtype=np.float64)
    lg = np.log1p(-(2.0 ** (-5.0 - np.arange(N_HEADS, dtype=np.float64))))
    dmat = np.exp(lg[:, None, None] * np.abs(t[:, None] - t[None, :])[None])
    qdec = np.exp(lg[:, None] * (t + 1.0))[:, :, None] * np.ones((1, 1, HEAD_DIM))
    kdec = np.exp(lg[:, None] * (L - 1.0 - t))[:, :, None] * np.ones((1, 1, HEAD_DIM))
    cdec = np.exp(lg * L)[:, None, None] * np.ones((1, 8, HEAD_DIM))
    return tuple(jnp.asarray(a, F32) for a in (dmat, qdec, kdec, cdec))


def _retention(proj, ret_g):
    B, S, _ = proj.shape
    L = RET_CHUNK
    dmat, qdec, kdec, cdec = _ret_tables()

    def col(c0):
        return pl.BlockSpec((None, S, HEAD_DIM), lambda b, h: (b, 0, c0 + h))

    def tab(r):
        return pl.BlockSpec((None, r, HEAD_DIM), lambda b, h: (h, 0, 0))

    return pl.pallas_call(
        functools.partial(_ret_kernel, n_chunks=S // L),
        grid=(B, N_HEADS),
        in_specs=[col(0), col(N_HEADS), col(2 * N_HEADS), col(3 * N_HEADS),
                  pl.BlockSpec((None, L, L), lambda b, h: (h, 0, 0)),
                  tab(L), tab(L), tab(8),
                  pl.BlockSpec((1, HEAD_DIM), lambda b, h: (0, h))],
        out_specs=pl.BlockSpec((None, S, HEAD_DIM), lambda b, h: (b, 0, h)),
        out_shape=jax.ShapeDtypeStruct((B, S, GROUP), BF16),
        scratch_shapes=[pltpu.VMEM((S, HEAD_DIM), F32)],
        compiler_params=_cparams(2),
        name="retention",
    )(proj, proj, proj, proj, dmat, qdec, kdec, cdec, ret_g.reshape(1, GROUP))


def _anchor_bcast(c, m, r):
    L = c.shape[0]
    blk = 2 * m
    if blk >= 8:
        c3 = c.reshape(L // blk, blk, HEAD_DIM)
        return jnp.broadcast_to(c3[:, r:r + 1, :], c3.shape).reshape(L, HEAD_DIM)
    c3 = c.reshape(L // 8, 8, HEAD_DIM)
    sub = lax.broadcasted_iota(jnp.int32, c3.shape, 1)
    out = jnp.broadcast_to(c3[:, r:r + 1, :], c3.shape)
    for j in range(1, 8 // blk):
        row = j * blk + r
        out = jnp.where(sub >= j * blk, jnp.broadcast_to(c3[:, row:row + 1, :], c3.shape), out)
    return out.reshape(L, HEAD_DIM)


def _split3(x):
    hi = x.astype(BF16)
    r1 = x - hi.astype(F32)
    mid = r1.astype(BF16)
    lo = (r1 - mid.astype(F32)).astype(BF16)
    return hi, mid, lo


def _hgrn_chunk(q, z, v, lb, state_t, tri, lvl, *, backward, first_layer):
    L = HGRN_CHUNK
    zf = z.astype(F32)
    e = jnp.exp(-jnp.abs(zf))
    inv = 1.0 / (1.0 + e)
    pos = zf >= 0.0
    sig = jnp.where(pos, inv, e * inv)
    sigm = jnp.where(pos, e * inv, inv)
    if first_layer:
        logf = jnp.minimum(zf, 0.0) - jnp.log(1.0 + e)
        kk = sigm
    else:
        logf = jnp.log(lb + (1.0 - lb) * sig)
        kk = (1.0 - lb) * sigm
    hi, mid, lo = _split3(logf)
    c = _dot(tri, hi) + _dot(tri, mid) + _dot(tri, lo)

    qf = q.astype(F32)
    scores = jnp.where(lvl == -1, _dot_nt(q, kk.astype(BF16)), 0.0)
    m, level = 1, 0
    while m < L:
        anchor = _anchor_bcast(c, m, m if backward else m - 1)
        p = jnp.exp(-jnp.abs(c - anchor))
        s = _dot_nt((qf * p).astype(BF16), (kk * p).astype(BF16))
        scores = jnp.where(lvl == level, s, scores)
        m, level = 2 * m, level + 1

    edge = 0 if backward else L - 1
    c_end = c[edge:edge + 1, :]
    qe = (qf * jnp.exp(c)).astype(BF16)
    out = _dot(scores.astype(BF16), v) + _dot_nt(qe, state_t.astype(BF16))
    ke = (kk * jnp.exp(c_end - c)).astype(BF16)
    state_t = jnp.exp(c_end) * state_t + _dot_tn(v, ke)
    return out, state_t


def _hgrn_kernel(q_ref, zf_ref, zb_ref, v_ref, lb_ref, trif_ref, trib_ref, lvlf_ref, lvlb_ref,
                 out_ref, *, n_chunks, first_layer):
    L = HGRN_CHUNK
    lb = lb_ref[...]
    zero_state = jnp.zeros((HEAD_DIM, HEAD_DIM), F32)

    def rows(i):
        return pl.ds(pl.multiple_of(i * L, L), L)

    def fwd(i, state):
        sl = rows(i)
        o, state = _hgrn_chunk(q_ref[sl, :], zf_ref[sl, :], v_ref[sl, :], lb, state,
                               trif_ref[...], lvlf_ref[...], backward=False, first_layer=first_layer)
        out_ref[sl, :] = o
        return state

    lax.fori_loop(0, n_chunks, fwd, zero_state)

    def bwd(ii, state):
        sl = rows(n_chunks - 1 - ii)
        o, state = _hgrn_chunk(q_ref[sl, :], zb_ref[sl, :], v_ref[sl, :], lb, state,
                               trib_ref[...], lvlb_ref[...], backward=True, first_layer=first_layer)
        out_ref[sl, :] += o
        return state

    lax.fori_loop(0, n_chunks, bwd, zero_state)


def _hgrn_tables():
    L = HGRN_CHUNK
    t = np.arange(L)
    tri_f = (t[None, :] <= t[:, None]).astype(np.float32)
    tri_b = (t[None, :] >= t[:, None]).astype(np.float32)
    x = t[:, None] ^ t[None, :]
    lev = np.floor(np.log2(np.maximum(x, 1))).astype(np.int32)
    lvl_f = np.where(t[:, None] > t[None, :], lev, np.where(x == 0, -1, -2)).astype(np.int32)
    lvl_b = np.where(t[:, None] < t[None, :], lev, np.where(x == 0, -1, -2)).astype(np.int32)
    return (jnp.asarray(tri_f, BF16), jnp.asarray(tri_b, BF16),
            jnp.asarray(lvl_f, jnp.int32), jnp.asarray(lvl_b, jnp.int32))


def _hgrn(proj, lb, first_layer):
    B, S, _ = proj.shape
    L = HGRN_CHUNK
    tri_f, tri_b, lvl_f, lvl_b = _hgrn_tables()

    def col(g):
        return pl.BlockSpec((None, S, HEAD_DIM), lambda b, h: (b, 0, g * N_HEADS + h))

    def full():
        return pl.BlockSpec((L, L), lambda b, h: (0, 0))

    return pl.pallas_call(
        functools.partial(_hgrn_kernel, n_chunks=S // L, first_layer=first_layer),
        grid=(B, N_HEADS),
        in_specs=[col(4), col(5), col(6), col(7),
                  pl.BlockSpec((1, HEAD_DIM), lambda b, h: (0, h)),
                  full(), full(), full(), full()],
        out_specs=pl.BlockSpec((None, S, HEAD_DIM), lambda b, h: (b, 0, h)),
        out_shape=jax.ShapeDtypeStruct((B, S, GROUP), F32),
        compiler_params=_cparams(2),
        name="hgrn",
    )(proj, proj, proj, proj, lb.reshape(1, GROUP), tri_f, tri_b, lvl_f, lvl_b)


def _outproj_kernel(r_ref, hraw_ref, hg_ref, hgn_ref, w_ref, h_ref, g2_ref, wr_ref,
                    h1_ref, xn_ref, aff_ref):
    hn = _rms(hraw_ref[...]) * hgn_ref[...] * hg_ref[...].astype(F32)
    mix = _dot(r_ref[...], w_ref[0:GROUP, :]) + _dot(hn.astype(BF16), w_ref[GROUP:2 * GROUP, :])
    h1 = h_ref[...] + mix
    h1_ref[...] = h1
    xn = (_rms(h1) * g2_ref[...]).astype(BF16)
    xn_ref[...] = xn
    logits = _dot_nt(wr_ref[...], xn)
    mx = jnp.max(logits, axis=0, keepdims=True)
    ex = jnp.exp(logits - mx)
    aff_ref[...] = ex / jnp.sum(ex, axis=0, keepdims=True)


def _outproj(r_out, h_raw, proj, hgrn_g, w_out_bf, h, g2, wr_t_bf, tm):
    B, S, D = h.shape
    E = wr_t_bf.shape[0]
    return pl.pallas_call(
        _outproj_kernel,
        grid=(B, S // tm),
        in_specs=[
            pl.BlockSpec((None, tm, GROUP), lambda b, i: (b, i, 0)),
            pl.BlockSpec((None, tm, GROUP), lambda b, i: (b, i, 0)),
            pl.BlockSpec((None, tm, GROUP), lambda b, i: (b, i, N_GROUPS - 1)),
            pl.BlockSpec((1, GROUP), lambda b, i: (0, 0)),
            pl.BlockSpec((2 * GROUP, D), lambda b, i: (0, 0)),
            pl.BlockSpec((None, tm, D), lambda b, i: (b, i, 0)),
            pl.BlockSpec((1, D), lambda b, i: (0, 0)),
            pl.BlockSpec((E, D), lambda b, i: (0, 0)),
        ],
        out_specs=[
            pl.BlockSpec((None, tm, D), lambda b, i: (b, i, 0)),
            pl.BlockSpec((None, tm, D), lambda b, i: (b, i, 0)),
            pl.BlockSpec((None, E, tm), lambda b, i: (b, 0, i)),
        ],
        out_shape=[jax.ShapeDtypeStruct((B, S, D), F32),
                   jax.ShapeDtypeStruct((B, S, D), BF16),
                   jax.ShapeDtypeStruct((B, E, S), F32)],
        compiler_params=_cparams(2),
        name="outproj",
    )(r_out, h_raw, proj, hgrn_g.reshape(1, GROUP), w_out_bf, h, g2.reshape(1, D), wr_t_bf)


def _select_kernel(aff_ref, pos_ref, gate_ref, *, cap):
    a = aff_ref[...]
    E, S = a.shape
    u = pltpu.bitcast(a, jnp.int32)
    capf = jnp.float32(cap)

    def count(mask):
        return jnp.sum(jnp.where(mask, 1.0, 0.0), axis=-1, keepdims=True)

    def value_bit(i, thr):
        cand = thr | (jnp.int32(1) << (30 - i))
        return jnp.where(count(u >= cand) >= capf, cand, thr)

    thr = lax.fori_loop(0, 31, value_bit, jnp.zeros((E, 1), jnp.int32))
    gt = u > thr
    eq = u == thr
    need = capf - count(gt)
    idx = lax.broadcasted_iota(jnp.int32, (E, S), 1)
    nbits = int(S).bit_length()

    def index_bit(i, cut):
        cand = cut | (jnp.int32(1) << (nbits - 1 - i))
        return jnp.where(count(eq & (idx < cand)) <= need, cand, cut)

    cut = lax.fori_loop(0, nbits, index_bit, jnp.zeros((E, 1), jnp.int32))
    sel = gt | (eq & (idx < cut))
    gate_ref[...] = jnp.where(sel, a, 0.0)

    li = lax.broadcasted_iota(jnp.int32, (128, 128), 0)
    lj = lax.broadcasted_iota(jnp.int32, (128, 128), 1)
    upper = jnp.where(li <= lj, 1.0, 0.0).astype(BF16)
    carry = jnp.zeros((E, 1), F32)
    for j in range(S // 128):
        sb = jnp.where(sel[:, j * 128:(j + 1) * 128], 1.0, 0.0)
        incl = _dot(sb.astype(BF16), upper)
        pos_ref[:, j * 128:(j + 1) * 128] = (incl - sb + carry).astype(jnp.int32)
        carry = carry + incl[:, 127:128]


def _select(aff, cap):
    B, E, S = aff.shape
    spec = pl.BlockSpec((None, E, S), lambda b: (b, 0, 0))
    return pl.pallas_call(
        functools.partial(_select_kernel, cap=cap),
        grid=(B,),
        in_specs=[spec],
        out_specs=[spec, spec],
        out_shape=[jax.ShapeDtypeStruct((B, E, S), jnp.int32),
                   jax.ShapeDtypeStruct((B, E, S), F32)],
        compiler_params=_cparams(1),
        name="select",
    )(aff)


def _moe_kernel(bs_ref, x_ref, pos_ref, gate_ref, wg_ref, wu_ref, wd_ref, ye_ref, xe_ref,
                *, n_tb, tk, cg, cap, n_exp):
    b = pl.program_id(0)
    e = pl.program_id(1)
    base = (b * n_exp + e) * (n_tb + 1)
    slot = lax.broadcasted_iota(jnp.int32, (cg, tk), 0)
    for g in range(cap // cg):
        p0 = g * cg
        xe_ref[...] = jnp.zeros_like(xe_ref)

        def gather(tb, carry):
            lo = bs_ref[base + tb]
            hi = bs_ref[base + tb + 1]

            @pl.when(jnp.logical_and(lo < p0 + cg, hi > p0))
            def _():
                key = jnp.where(gate_ref[pl.ds(tb, 1), :] > 0.0, pos_ref[pl.ds(tb, 1), :] - p0, -1)
                onehot = jnp.where(key == slot, 1.0, 0.0).astype(BF16)
                rows = pl.ds(pl.multiple_of(tb * tk, tk), tk)
                xe_ref[...] += _dot(onehot, x_ref[rows, :])
            return carry

        lax.fori_loop(0, n_tb, gather, 0)
        xe = xe_ref[...].astype(BF16)
        hid = (_silu(_dot(xe, wg_ref[...])) * _dot(xe, wu_ref[...])).astype(BF16)
        ye_ref[p0:p0 + cg, :] = _dot(hid, wd_ref[...]).astype(BF16)


def _moe_ffn(bs, xn, pos4, gate4, wg, wu, wd, cap, tk, cg):
    B, S, D = xn.shape
    E, _, FF = wg.shape
    n_tb = S // tk
    grid_spec = pltpu.PrefetchScalarGridSpec(
        num_scalar_prefetch=1,
        grid=(B, E),
        in_specs=[
            pl.BlockSpec((None, S, D), lambda b, e, bs: (b, 0, 0), pipeline_mode=pl.Buffered(1)),
            pl.BlockSpec((None, None, n_tb, tk), lambda b, e, bs: (b, e, 0, 0)),
            pl.BlockSpec((None, None, n_tb, tk), lambda b, e, bs: (b, e, 0, 0)),
            pl.BlockSpec((None, D, FF), lambda b, e, bs: (e, 0, 0)),
            pl.BlockSpec((None, D, FF), lambda b, e, bs: (e, 0, 0)),
            pl.BlockSpec((None, FF, D), lambda b, e, bs: (e, 0, 0)),
        ],
        out_specs=pl.BlockSpec((None, None, cap, D), lambda b, e, bs: (b, e, 0, 0)),
        scratch_shapes=[pltpu.VMEM((cg, D), F32)],
    )
    return pl.pallas_call(
        functools.partial(_moe_kernel, n_tb=n_tb, tk=tk, cg=cg, cap=cap, n_exp=E),
        grid_spec=grid_spec,
        out_shape=jax.ShapeDtypeStruct((B, E, cap, D), BF16),
        compiler_params=_cparams(2),
        name="moe_ffn",
    )(bs, xn, pos4, gate4, wg, wu, wd)


def _combine_kernel(bs_ref, h1_ref, ye_ref, pos_ref, gate_ref, fg_ref, out_ref,
                    *, n_tb, tk, wn, cap, n_exp, final_norm):
    b = pl.program_id(0)
    tb = pl.program_id(1)
    out_ref[...] = h1_ref[...]
    lane_slot = lax.broadcasted_iota(jnp.int32, (wn, tk), 0)
    for e in range(n_exp):
        base = (b * n_exp + e) * (n_tb + 1)
        ws = bs_ref[base + tb]
        we = bs_ref[base + tb + 1]
        ws_al = (ws // 16) * 16
        n_win = jnp.where(we > ws, (we - ws_al + wn - 1) // wn, 0)
        prow = pos_ref[e:e + 1, :]
        grow = gate_ref[e:e + 1, :]

        def window(w, carry):
            first = ws_al + w * wn
            start = pl.multiple_of(jnp.minimum(first, cap - wn), 16)
            slot = start + lane_slot
            hit = jnp.logical_and(prow == slot, slot >= first)
            weight = jnp.where(hit, grow, 0.0).astype(BF16)
            out_ref[...] += _dot_tn(weight, ye_ref[e, pl.ds(start, wn), :])
            return carry

        lax.fori_loop(0, n_win, window, 0)
    if final_norm:
        out_ref[...] = _rms(out_ref[...]) * fg_ref[...]


def _combine(bs, h1, ye, pos_t, gate_t, final_g, cap, tk, wn, final_norm):
    B, S, D = h1.shape
    E = ye.shape[1]
    n_tb = S // tk
    grid_spec = pltpu.PrefetchScalarGridSpec(
        num_scalar_prefetch=1,
        grid=(B, n_tb),
        in_specs=[
            pl.BlockSpec((None, tk, D), lambda b, t, bs: (b, t, 0)),
            pl.BlockSpec((None, E, cap, D), lambda b, t, bs: (b, 0, 0, 0), pipeline_mode=pl.Buffered(1)),
            pl.BlockSpec((None, None, E, tk), lambda b, t, bs: (b, t, 0, 0)),
            pl.BlockSpec((None, None, E, tk), lambda b, t, bs: (b, t, 0, 0)),
            pl.BlockSpec((1, D), lambda b, t, bs: (0, 0)),
        ],
        out_specs=pl.BlockSpec((None, tk, D), lambda b, t, bs: (b, t, 0)),
    )
    return pl.pallas_call(
        functools.partial(_combine_kernel, n_tb=n_tb, tk=tk, wn=wn, cap=cap, n_exp=E,
                          final_norm=final_norm),
        grid_spec=grid_spec,
        out_shape=jax.ShapeDtypeStruct((B, S, D), F32),
        compiler_params=_cparams(2),
        name="combine",
    )(bs, h1, ye, pos_t, gate_t, final_g.reshape(1, D))


def _rope_tables(S):
    half = HEAD_DIM // 2
    inv_freq = ROPE_BASE ** (-jnp.arange(half, dtype=F32) / half)
    ang = jnp.arange(S).astype(F32)[:, None] * inv_freq[None, :]
    cos, sin = jnp.cos(ang), jnp.sin(ang)
    return jnp.concatenate([cos, cos], axis=-1), jnp.concatenate([-sin, sin], axis=-1)


def _block_starts(pos, tk, cap):
    B, E, _ = pos.shape
    bs = jnp.concatenate([pos[:, :, ::tk], jnp.full((B, E, 1), cap, jnp.int32)], axis=-1)
    return bs.reshape(-1)


def kernel(x, norm1_g, w_in, ret_norm_g, hgrn_norm_g, w_out, lower_bounds, norm2_g, w_router,
           w_gate, w_up, w_down, final_norm_g):
    B, S, D = x.shape
    depth = w_in.shape[0]
    E = w_router.shape[-1]
    cap = CAPACITY_FACTOR * S // E
    tm = min(512, S)
    tk = min(512, S)
    cg = min(256, cap)
    wn = min(128, cap)
    n_tb = S // tk

    lbs = jax.nn.softmax(lower_bounds.astype(F32), axis=0)
    lbs = jnp.cumsum(lbs, axis=0) - lbs[0]
    cos, sin = _rope_tables(S)

    h = x
    for layer in range(depth):
        proj = _inproj(h, norm1_g[layer], w_in[layer].astype(BF16), cos, sin, tm)
        r_out = _retention(proj, ret_norm_g[layer])
        h_raw = _hgrn(proj, lbs[layer], first_layer=(layer == 0))
        h1, xn, aff = _outproj(r_out, h_raw, proj, hgrn_norm_g[layer], w_out[layer].astype(BF16),
                               h, norm2_g[layer], w_router[layer].T.astype(BF16), tm)
        pos, gate = _select(aff, cap)
        bs = _block_starts(pos, tk, cap)
        ye = _moe_ffn(bs, xn, pos.reshape(B, E, n_tb, tk), gate.reshape(B, E, n_tb, tk),
                      w_gate[layer].astype(BF16), w_up[layer].astype(BF16),
                      w_down[layer].astype(BF16), cap, tk, cg)
        pos_t = pos.reshape(B, E, n_tb, tk).transpose(0, 2, 1, 3)
        gate_t = gate.reshape(B, E, n_tb, tk).transpose(0, 2, 1, 3)
        h = _combine(bs, h1, ye, pos_t, gate_t, final_norm_g, cap, tk, wn,
                     final_norm=(layer == depth - 1))
    return h
```

```python
import functools

import numpy as np
import jax
import jax.numpy as jnp
from jax import lax
from jax.experimental import pallas as pl
from jax.experimental.pallas import tpu as pltpu

F32 = jnp.float32
BF16 = jnp.bfloat16

HEAD_DIM = 128
N_HEADS = 4
GROUP = N_HEADS * HEAD_DIM
N_GROUPS = 9
ROPE_BASE = 10000.0
NORM_EPS = 1e-6
CAPACITY_FACTOR = 2

RET_CHUNK = 256
HGRN_CHUNK = 128
VMEM_LIMIT = 58 * 1024 * 1024


def _cparams(n_axes):
    return pltpu.CompilerParams(
        dimension_semantics=("arbitrary",) * n_axes, vmem_limit_bytes=VMEM_LIMIT)


def _dot(a, b):
    return jnp.dot(a, b, preferred_element_type=F32)


def _dot_nt(a, b):
    return lax.dot_general(a, b, (((1,), (1,)), ((), ())), preferred_element_type=F32)


def _dot_tn(a, b):
    return lax.dot_general(a, b, (((0,), (0,)), ((), ())), preferred_element_type=F32)


def _silu(x):
    return x * (1.0 / (1.0 + jnp.exp(-x)))


def _rms(x):
    return x * lax.rsqrt(jnp.mean(x * x, axis=-1, keepdims=True) + NORM_EPS)


def _inproj_kernel(h_ref, g_ref, w_ref, cos_ref, sin_ref, out_ref):
    xn = (_rms(h_ref[...]) * g_ref[...]).astype(BF16)
    cos = cos_ref[...]
    sin = sin_ref[...]
    scale = HEAD_DIM ** -0.5
    for j in range(N_GROUPS):
        acc = _dot(xn, w_ref[:, j * GROUP:(j + 1) * GROUP])
        if j in (0, 1):
            for hh in range(N_HEADS):
                sl = acc[:, hh * HEAD_DIM:(hh + 1) * HEAD_DIM]
                rot = sl * cos + pltpu.roll(sl, HEAD_DIM // 2, 1) * sin
                if j == 1:
                    rot = rot * scale
                out_ref[:, j * GROUP + hh * HEAD_DIM:j * GROUP + (hh + 1) * HEAD_DIM] = rot.astype(BF16)
            continue
        if j in (3, 8):
            acc = _silu(acc)
        elif j == 4:
            acc = _silu(acc) * scale
        out_ref[:, j * GROUP:(j + 1) * GROUP] = acc.astype(BF16)


def _inproj(h, g, w_bf, cos, sin, tm):
    B, S, D = h.shape
    ncol = w_bf.shape[1]
    return pl.pallas_call(
        _inproj_kernel,
        grid=(B, S // tm),
        in_specs=[
            pl.BlockSpec((None, tm, D), lambda b, i: (b, i, 0)),
            pl.BlockSpec((1, D), lambda b, i: (0, 0)),
            pl.BlockSpec((D, ncol), lambda b, i: (0, 0), pipeline_mode=pl.Buffered(1)),
            pl.BlockSpec((tm, HEAD_DIM), lambda b, i: (i, 0)),
            pl.BlockSpec((tm, HEAD_DIM), lambda b, i: (i, 0)),
        ],
        out_specs=pl.BlockSpec((None, tm, ncol), lambda b, i: (b, i, 0)),
        out_shape=jax.ShapeDtypeStruct((B, S, ncol), BF16),
        compiler_params=_cparams(2),
        name="inproj",
    )(h, g.reshape(1, D), w_bf, cos, sin)


def _ret_kernel(q_ref, k_ref, v_ref, g_ref, dmat_ref, qdec_ref, kdec_ref, cdec_ref, rg_ref,
                out_ref, acc_ref, *, n_chunks):
    L = RET_CHUNK
    cdec = cdec_ref[0:1, :]
    zero_state = jnp.zeros((HEAD_DIM, HEAD_DIM), F32)

    def rows(i):
        return pl.ds(pl.multiple_of(i * L, L), L)

    def finish(sl, o):
        y = _rms(o) * rg_ref[...] * g_ref[sl, :].astype(F32)
        out_ref[sl, :] = y.astype(BF16)

    def step(i, states, second):
        state_f, state_b = states
        sf, sb = rows(i), rows(n_chunks - 1 - i)
        q, k, v = q_ref[sf, :], k_ref[sf, :], v_ref[sf, :]
        s = (_dot_nt(q, k) * dmat_ref[...]).astype(BF16)
        qd = (q.astype(F32) * qdec_ref[...]).astype(BF16)
        of = _dot(s, v) + _dot(qd, state_f.astype(BF16))
        kd = (k.astype(F32) * kdec_ref[...]).astype(BF16)
        state_f = cdec * state_f + _dot_tn(kd, v)

        q, k, v = q_ref[sb, :], k_ref[sb, :], v_ref[sb, :]
        qd = (q.astype(F32) * kdec_ref[...]).astype(BF16)
        ob = _dot(qd, state_b.astype(BF16))
        kd = (k.astype(F32) * qdec_ref[...]).astype(BF16)
        state_b = cdec * state_b + _dot_tn(kd, v)
        if second:
            finish(sf, acc_ref[sf, :] + of)
            finish(sb, acc_ref[sb, :] + ob)
        else:
            acc_ref[sf, :] = of
            acc_ref[sb, :] = ob
        return state_f, state_b

    half = n_chunks // 2
    states = lax.fori_loop(0, half, functools.partial(step, second=False), (zero_state, zero_state))
    lax.fori_loop(half, n_chunks, functools.partial(step, second=True), states)


def _ret_tables():
    L = RET_CHUNK
    t = np.arange(L, dtype=np.float64)
    lg = np.log1p(-(2.0 ** (-5.0 - np.arange(N_HEADS, dtype=np.float64))))
    dmat = np.exp(lg[:, None, None] * np.abs(t[:, None] - t[None, :])[None])
    qdec = np.exp(lg[:, None] * (t + 1.0))[:, :, None] * np.ones((1, 1, HEAD_DIM))
    kdec = np.exp(lg[:, None] * (L - 1.0 - t))[:, :, None] * np.ones((1, 1, HEAD_DIM))
    cdec = np.exp(lg * L)[:, None, None] * np.ones((1, 8, HEAD_DIM))
    return tuple(jnp.asarray(a, F32) for a in (dmat, qdec, kdec, cdec))


def _retention(proj, ret_g):
    B, S, _ = proj.shape
    L = RET_CHUNK
    assert S % (2 * L) == 0, "retention pairs chunk i with chunk n-1-i"
    dmat, qdec, kdec, cdec = _ret_tables()

    def col(c0):
        return pl.BlockSpec((None, S, HEAD_DIM), lambda b, h: (b, 0, c0 + h))

    def tab(r):
        return pl.BlockSpec((None, r, HEAD_DIM), lambda b, h: (h, 0, 0))

    return pl.pallas_call(
        functools.partial(_ret_kernel, n_chunks=S // L),
        grid=(B, N_HEADS),
        in_specs=[col(0), col(N_HEADS), col(2 * N_HEADS), col(3 * N_HEADS),
                  pl.BlockSpec((None, L, L), lambda b, h: (h, 0, 0)),
                  tab(L), tab(L), tab(8),
                  pl.BlockSpec((1, HEAD_DIM), lambda b, h: (0, h))],
        out_specs=pl.BlockSpec((None, S, HEAD_DIM), lambda b, h: (b, 0, h)),
        out_shape=jax.ShapeDtypeStruct((B, S, GROUP), BF16),
        scratch_shapes=[pltpu.VMEM((S, HEAD_DIM), F32)],
        compiler_params=_cparams(2),
        name="retention",
    )(proj, proj, proj, proj, dmat, qdec, kdec, cdec, ret_g.reshape(1, GROUP))


def _anchor_bcast(c, m, r):
    L = c.shape[0]
    blk = 2 * m
    if blk >= 8:
        c3 = c.reshape(L // blk, blk, HEAD_DIM)
        return jnp.broadcast_to(c3[:, r:r + 1, :], c3.shape).reshape(L, HEAD_DIM)
    c3 = c.reshape(L // 8, 8, HEAD_DIM)
    sub = lax.broadcasted_iota(jnp.int32, c3.shape, 1)
    out = jnp.broadcast_to(c3[:, r:r + 1, :], c3.shape)
    for j in range(1, 8 // blk):
        row = j * blk + r
        out = jnp.where(sub >= j * blk, jnp.broadcast_to(c3[:, row:row + 1, :], c3.shape), out)
    return out.reshape(L, HEAD_DIM)


def _split3(x):
    hi = x.astype(BF16)
    r1 = x - hi.astype(F32)
    mid = r1.astype(BF16)
    lo = (r1 - mid.astype(F32)).astype(BF16)
    return hi, mid, lo


def _hgrn_chunk(q, z, v, lb, state_t, tri, lvl, *, backward, first_layer):
    L = HGRN_CHUNK
    zf = z.astype(F32)
    e = jnp.exp(-jnp.abs(zf))
    inv = 1.0 / (1.0 + e)
    pos = zf >= 0.0
    sig = jnp.where(pos, inv, e * inv)
    sigm = jnp.where(pos, e * inv, inv)
    if first_layer:
        logf = jnp.minimum(zf, 0.0) - jnp.log(1.0 + e)
        kk = sigm
    else:
        logf = jnp.log(lb + (1.0 - lb) * sig)
        kk = (1.0 - lb) * sigm
    c3 = _dot(tri, jnp.concatenate(_split3(logf), axis=1))
    c = (c3[:, :HEAD_DIM] + c3[:, HEAD_DIM:2 * HEAD_DIM]) + c3[:, 2 * HEAD_DIM:]

    qf = q.astype(F32)
    scores = jnp.where(lvl == -1, _dot_nt(q, kk.astype(BF16)), 0.0)
    m, level = 1, 0
    while m < L:
        anchor = _anchor_bcast(c, m, m if backward else m - 1)
        p = jnp.exp(-jnp.abs(c - anchor))
        s = _dot_nt((qf * p).astype(BF16), (kk * p).astype(BF16))
        scores = jnp.where(lvl == level, s, scores)
        m, level = 2 * m, level + 1

    edge = 0 if backward else L - 1
    c_end = c[edge:edge + 1, :]
    qe = (qf * jnp.exp(c)).astype(BF16)
    out = _dot(scores.astype(BF16), v) + _dot_nt(qe, state_t.astype(BF16))
    ke = (kk * jnp.exp(c_end - c)).astype(BF16)
    state_t = jnp.exp(c_end) * state_t + _dot_tn(v, ke)
    return out, state_t


def _hgrn_kernel(q_ref, zf_ref, zb_ref, v_ref, lb_ref, trif_ref, trib_ref, lvlf_ref, lvlb_ref,
                 out_ref, *, n_chunks, first_layer):
    L = HGRN_CHUNK
    lb = lb_ref[...]
    zero_state = jnp.zeros((HEAD_DIM, HEAD_DIM), F32)

    def rows(i):
        return pl.ds(pl.multiple_of(i * L, L), L)

    def step(i, states, accumulate):
        sf, sb = rows(i), rows(n_chunks - 1 - i)
        of, state_f = _hgrn_chunk(q_ref[sf, :], zf_ref[sf, :], v_ref[sf, :], lb, states[0],
                                  trif_ref[...], lvlf_ref[...], backward=False, first_layer=first_layer)
        ob, state_b = _hgrn_chunk(q_ref[sb, :], zb_ref[sb, :], v_ref[sb, :], lb, states[1],
                                  trib_ref[...], lvlb_ref[...], backward=True, first_layer=first_layer)
        if accumulate:
            out_ref[sf, :] += of
            out_ref[sb, :] += ob
        else:
            out_ref[sf, :] = of
            out_ref[sb, :] = ob
        return state_f, state_b

    half = n_chunks // 2
    states = lax.fori_loop(0, half, functools.partial(step, accumulate=False), (zero_state, zero_state))
    lax.fori_loop(half, n_chunks, functools.partial(step, accumulate=True), states)


def _hgrn_tables():
    L = HGRN_CHUNK
    t = np.arange(L)
    tri_f = (t[None, :] <= t[:, None]).astype(np.float32)
    tri_b = (t[None, :] >= t[:, None]).astype(np.float32)
    x = t[:, None] ^ t[None, :]
    lev = np.floor(np.log2(np.maximum(x, 1))).astype(np.int32)
    lvl_f = np.where(t[:, None] > t[None, :], lev, np.where(x == 0, -1, -2)).astype(np.int32)
    lvl_b = np.where(t[:, None] < t[None, :], lev, np.where(x == 0, -1, -2)).astype(np.int32)
    return (jnp.asarray(tri_f, BF16), jnp.asarray(tri_b, BF16),
            jnp.asarray(lvl_f, jnp.int32), jnp.asarray(lvl_b, jnp.int32))


def _hgrn(proj, lb, first_layer):
    B, S, _ = proj.shape
    L = HGRN_CHUNK
    assert S % (2 * L) == 0, "HGRN2 pairs chunk i with chunk n-1-i"
    tri_f, tri_b, lvl_f, lvl_b = _hgrn_tables()

    def col(g):
        return pl.BlockSpec((None, S, HEAD_DIM), lambda b, h: (b, 0, g * N_HEADS + h))

    def full():
        return pl.BlockSpec((L, L), lambda b, h: (0, 0))

    return pl.pallas_call(
        functools.partial(_hgrn_kernel, n_chunks=S // L, first_layer=first_layer),
        grid=(B, N_HEADS),
        in_specs=[col(4), col(5), col(6), col(7),
                  pl.BlockSpec((1, HEAD_DIM), lambda b, h: (0, h)),
                  full(), full(), full(), full()],
        out_specs=pl.BlockSpec((None, S, HEAD_DIM), lambda b, h: (b, 0, h)),
        out_shape=jax.ShapeDtypeStruct((B, S, GROUP), F32),
        compiler_params=_cparams(2),
        name="hgrn",
    )(proj, proj, proj, proj, lb.reshape(1, GROUP), tri_f, tri_b, lvl_f, lvl_b)


def _outproj_kernel(r_ref, hraw_ref, hg_ref, hgn_ref, w_ref, h_ref, g2_ref, wr_ref,
                    h1_ref, xn_ref, aff_ref):
    hn = _rms(hraw_ref[...]) * hgn_ref[...] * hg_ref[...].astype(F32)
    mix = _dot(r_ref[...], w_ref[0:GROUP, :]) + _dot(hn.astype(BF16), w_ref[GROUP:2 * GROUP, :])
    h1 = h_ref[...] + mix
    h1_ref[...] = h1
    xn = (_rms(h1) * g2_ref[...]).astype(BF16)
    xn_ref[...] = xn
    logits = _dot_nt(wr_ref[...], xn)
    mx = jnp.max(logits, axis=0, keepdims=True)
    ex = jnp.exp(logits - mx)
    aff_ref[...] = ex / jnp.sum(ex, axis=0, keepdims=True)


def _outproj(r_out, h_raw, proj, hgrn_g, w_out_bf, h, g2, wr_t_bf, tm):
    B, S, D = h.shape
    E = wr_t_bf.shape[0]
    return pl.pallas_call(
        _outproj_kernel,
        grid=(B, S // tm),
        in_specs=[
            pl.BlockSpec((None, tm, GROUP), lambda b, i: (b, i, 0)),
            pl.BlockSpec((None, tm, GROUP), lambda b, i: (b, i, 0)),
            pl.BlockSpec((None, tm, GROUP), lambda b, i: (b, i, N_GROUPS - 1)),
            pl.BlockSpec((1, GROUP), lambda b, i: (0, 0)),
            pl.BlockSpec((2 * GROUP, D), lambda b, i: (0, 0)),
            pl.BlockSpec((None, tm, D), lambda b, i: (b, i, 0)),
            pl.BlockSpec((1, D), lambda b, i: (0, 0)),
            pl.BlockSpec((E, D), lambda b, i: (0, 0)),
        ],
        out_specs=[
            pl.BlockSpec((None, tm, D), lambda b, i: (b, i, 0)),
            pl.BlockSpec((None, tm, D), lambda b, i: (b, i, 0)),
            pl.BlockSpec((None, E, tm), lambda b, i: (b, 0, i)),
        ],
        out_shape=[jax.ShapeDtypeStruct((B, S, D), F32),
                   jax.ShapeDtypeStruct((B, S, D), BF16),
                   jax.ShapeDtypeStruct((B, E, S), F32)],
        compiler_params=_cparams(2),
        name="outproj",
    )(r_out, h_raw, proj, hgrn_g.reshape(1, GROUP), w_out_bf, h, g2.reshape(1, D), wr_t_bf)


def _select_kernel(aff_ref, pos_ref, gate_ref, *, cap):
    a = aff_ref[...]
    E, S = a.shape
    u = pltpu.bitcast(a, jnp.int32)
    capf = jnp.float32(cap)

    def count(mask):
        return jnp.sum(jnp.where(mask, 1.0, 0.0), axis=-1, keepdims=True)

    def value_bit(i, thr):
        cand = thr | (jnp.int32(1) << (30 - i))
        return jnp.where(count(u >= cand) >= capf, cand, thr)

    thr = lax.fori_loop(0, 31, value_bit, jnp.zeros((E, 1), jnp.int32))
    gt = u > thr
    eq = u == thr
    need = capf - count(gt)
    idx = lax.broadcasted_iota(jnp.int32, (E, S), 1)
    nbits = int(S).bit_length()

    def index_bit(i, cut):
        cand = cut | (jnp.int32(1) << (nbits - 1 - i))
        return jnp.where(count(eq & (idx < cand)) <= need, cand, cut)

    cut = lax.fori_loop(0, nbits, index_bit, jnp.zeros((E, 1), jnp.int32))
    sel = gt | (eq & (idx < cut))
    gate_ref[...] = jnp.where(sel, a, 0.0)

    li = lax.broadcasted_iota(jnp.int32, (128, 128), 0)
    lj = lax.broadcasted_iota(jnp.int32, (128, 128), 1)
    upper = jnp.where(li <= lj, 1.0, 0.0).astype(BF16)
    carry = jnp.zeros((E, 1), F32)
    for j in range(S // 128):
        sb = jnp.where(sel[:, j * 128:(j + 1) * 128], 1.0, 0.0)
        incl = _dot(sb.astype(BF16), upper)
        pos_ref[:, j * 128:(j + 1) * 128] = (incl - sb + carry).astype(jnp.int32)
        carry = carry + incl[:, 127:128]


def _select(aff, cap):
    B, E, S = aff.shape
    spec = pl.BlockSpec((None, E, S), lambda b: (b, 0, 0))
    return pl.pallas_call(
        functools.partial(_select_kernel, cap=cap),
        grid=(B,),
        in_specs=[spec],
        out_specs=[spec, spec],
        out_shape=[jax.ShapeDtypeStruct((B, E, S), jnp.int32),
                   jax.ShapeDtypeStruct((B, E, S), F32)],
        compiler_params=_cparams(1),
        name="select",
    )(aff)


def _moe_kernel(bs_ref, x_ref, pos_ref, gate_ref, wg_ref, wu_ref, wd_ref, ye_ref, xe_ref,
                *, n_tb, tk, cg, cap, n_exp):
    b = pl.program_id(0)
    e = pl.program_id(1)
    base = (b * n_exp + e) * (n_tb + 1)
    slot = lax.broadcasted_iota(jnp.int32, (cg, tk), 0)
    for g in range(cap // cg):
        p0 = g * cg
        xe_ref[...] = jnp.zeros_like(xe_ref)

        def gather(tb, carry):
            lo = bs_ref[base + tb]
            hi = bs_ref[base + tb + 1]

            @pl.when(jnp.logical_and(lo < p0 + cg, hi > p0))
            def _():
                key = jnp.where(gate_ref[pl.ds(tb, 1), :] > 0.0, pos_ref[pl.ds(tb, 1), :] - p0, -1)
                onehot = jnp.where(key == slot, 1.0, 0.0).astype(BF16)
                rows = pl.ds(pl.multiple_of(tb * tk, tk), tk)
                xe_ref[...] += _dot(onehot, x_ref[rows, :])
            return carry

        lax.fori_loop(0, n_tb, gather, 0)
        xe = xe_ref[...].astype(BF16)
        hid = (_silu(_dot(xe, wg_ref[...])) * _dot(xe, wu_ref[...])).astype(BF16)
        ye_ref[p0:p0 + cg, :] = _dot(hid, wd_ref[...]).astype(BF16)


def _moe_ffn(bs, xn, pos4, gate4, wg, wu, wd, cap, tk, cg):
    B, S, D = xn.shape
    E, _, FF = wg.shape
    n_tb = S // tk
    grid_spec = pltpu.PrefetchScalarGridSpec(
        num_scalar_prefetch=1,
        grid=(B, E),
        in_specs=[
            pl.BlockSpec((None, S, D), lambda b, e, bs: (b, 0, 0), pipeline_mode=pl.Buffered(1)),
            pl.BlockSpec((None, None, n_tb, tk), lambda b, e, bs: (b, e, 0, 0)),
            pl.BlockSpec((None, None, n_tb, tk), lambda b, e, bs: (b, e, 0, 0)),
            pl.BlockSpec((None, D, FF), lambda b, e, bs: (e, 0, 0)),
            pl.BlockSpec((None, D, FF), lambda b, e, bs: (e, 0, 0)),
            pl.BlockSpec((None, FF, D), lambda b, e, bs: (e, 0, 0)),
        ],
        out_specs=pl.BlockSpec((None, None, cap, D), lambda b, e, bs: (b, e, 0, 0)),
        scratch_shapes=[pltpu.VMEM((cg, D), F32)],
    )
    return pl.pallas_call(
        functools.partial(_moe_kernel, n_tb=n_tb, tk=tk, cg=cg, cap=cap, n_exp=E),
        grid_spec=grid_spec,
        out_shape=jax.ShapeDtypeStruct((B, E, cap, D), BF16),
        compiler_params=_cparams(2),
        name="moe_ffn",
    )(bs, xn, pos4, gate4, wg, wu, wd)


def _combine_kernel(bs_ref, h1_ref, ye_ref, pos_ref, gate_ref, fg_ref, out_ref,
                    *, n_tb, tk, wn, cap, n_exp, final_norm):
    b = pl.program_id(0)
    tb = pl.program_id(1)
    out_ref[...] = h1_ref[...]
    lane_slot = lax.broadcasted_iota(jnp.int32, (wn, tk), 0)
    for e in range(n_exp):
        base = (b * n_exp + e) * (n_tb + 1)
        ws = bs_ref[base + tb]
        we = bs_ref[base + tb + 1]
        ws_al = (ws // 16) * 16
        n_win = jnp.where(we > ws, (we - ws_al + wn - 1) // wn, 0)
        prow = pos_ref[e:e + 1, :]
        grow = gate_ref[e:e + 1, :]

        def window(w, carry):
            first = ws_al + w * wn
            start = pl.multiple_of(jnp.minimum(first, cap - wn), 16)
            slot = start + lane_slot
            hit = jnp.logical_and(prow == slot, slot >= first)
            weight = jnp.where(hit, grow, 0.0).astype(BF16)
            out_ref[...] += _dot_tn(weight, ye_ref[e, pl.ds(start, wn), :])
            return carry

        lax.fori_loop(0, n_win, window, 0)
    if final_norm:
        out_ref[...] = _rms(out_ref[...]) * fg_ref[...]


def _combine(bs, h1, ye, pos_t, gate_t, final_g, cap, tk, wn, final_norm):
    B, S, D = h1.shape
    E = ye.shape[1]
    n_tb = S // tk
    grid_spec = pltpu.PrefetchScalarGridSpec(
        num_scalar_prefetch=1,
        grid=(B, n_tb),
        in_specs=[
            pl.BlockSpec((None, tk, D), lambda b, t, bs: (b, t, 0)),
            pl.BlockSpec((None, E, cap, D), lambda b, t, bs: (b, 0, 0, 0), pipeline_mode=pl.Buffered(1)),
            pl.BlockSpec((None, None, E, tk), lambda b, t, bs: (b, t, 0, 0)),
            pl.BlockSpec((None, None, E, tk), lambda b, t, bs: (b, t, 0, 0)),
            pl.BlockSpec((1, D), lambda b, t, bs: (0, 0)),
        ],
        out_specs=pl.BlockSpec((None, tk, D), lambda b, t, bs: (b, t, 0)),
    )
    return pl.pallas_call(
        functools.partial(_combine_kernel, n_tb=n_tb, tk=tk, wn=wn, cap=cap, n_exp=E,
                          final_norm=final_norm),
        grid_spec=grid_spec,
        out_shape=jax.ShapeDtypeStruct((B, S, D), F32),
        compiler_params=_cparams(2),
        name="combine",
    )(bs, h1, ye, pos_t, gate_t, final_g.reshape(1, D))


def _rope_tables(S):
    half = HEAD_DIM // 2
    inv_freq = ROPE_BASE ** (-jnp.arange(half, dtype=F32) / half)
    ang = jnp.arange(S).astype(F32)[:, None] * inv_freq[None, :]
    cos, sin = jnp.cos(ang), jnp.sin(ang)
    return jnp.concatenate([cos, cos], axis=-1), jnp.concatenate([-sin, sin], axis=-1)


def _block_starts(pos, tk, cap):
    B, E, _ = pos.shape
    bs = jnp.concatenate([pos[:, :, ::tk], jnp.full((B, E, 1), cap, jnp.int32)], axis=-1)
    return bs.reshape(-1)


def kernel(x, norm1_g, w_in, ret_norm_g, hgrn_norm_g, w_out, lower_bounds, norm2_g, w_router,
           w_gate, w_up, w_down, final_norm_g):
    B, S, D = x.shape
    depth = w_in.shape[0]
    E = w_router.shape[-1]
    cap = CAPACITY_FACTOR * S // E
    tm = min(512, S)
    tk = min(512, S)
    cg = min(256, cap)
    wn = min(128, cap)
    n_tb = S // tk

    lbs = jax.nn.softmax(lower_bounds.astype(F32), axis=0)
    lbs = jnp.cumsum(lbs, axis=0) - lbs[0]
    cos, sin = _rope_tables(S)

    h = x
    for layer in range(depth):
        proj = _inproj(h, norm1_g[layer], w_in[layer].astype(BF16), cos, sin, tm)
        r_out = _retention(proj, ret_norm_g[layer])
        h_raw = _hgrn(proj, lbs[layer], first_layer=(layer == 0))
        h1, xn, aff = _outproj(r_out, h_raw, proj, hgrn_norm_g[layer], w_out[layer].astype(BF16),
                               h, norm2_g[layer], w_router[layer].T.astype(BF16), tm)
        pos, gate = _select(aff, cap)
        bs = _block_starts(pos, tk, cap)
        ye = _moe_ffn(bs, xn, pos.reshape(B, E, n_tb, tk), gate.reshape(B, E, n_tb, tk),
                      w_gate[layer].astype(BF16), w_up[layer].astype(BF16),
                      w_down[layer].astype(BF16), cap, tk, cg)
        pos_t = pos.reshape(B, E, n_tb, tk).transpose(0, 2, 1, 3)
        gate_t = gate.reshape(B, E, n_tb, tk).transpose(0, 2, 1, 3)
        h = _combine(bs, h1, ye, pos_t, gate_t, final_norm_g, cap, tk, wn,
                     final_norm=(layer == depth - 1))
    return h
```

```python
import functools

import numpy as np
import jax
import jax.numpy as jnp
from jax import lax
from jax.experimental import pallas as pl
from jax.experimental.pallas import tpu as pltpu

F32 = jnp.float32
BF16 = jnp.bfloat16

HEAD_DIM = 128
N_HEADS = 4
GROUP = N_HEADS * HEAD_DIM
N_GROUPS = 9
ROPE_BASE = 10000.0
NORM_EPS = 1e-6
CAPACITY_FACTOR = 2

RET_CHUNK = 256
HGRN_CHUNK = 128
VMEM_LIMIT = 58 * 1024 * 1024


def _cparams(n_axes):
    return pltpu.CompilerParams(
        dimension_semantics=("arbitrary",) * n_axes, vmem_limit_bytes=VMEM_LIMIT)


def _dot(a, b):
    return jnp.dot(a, b, preferred_element_type=F32)


def _dot_nt(a, b):
    return lax.dot_general(a, b, (((1,), (1,)), ((), ())), preferred_element_type=F32)


def _dot_tn(a, b):
    return lax.dot_general(a, b, (((0,), (0,)), ((), ())), preferred_element_type=F32)


def _silu(x):
    return x * (1.0 / (1.0 + jnp.exp(-x)))


def _rms(x):
    return x * lax.rsqrt(jnp.mean(x * x, axis=-1, keepdims=True) + NORM_EPS)


def _inproj_kernel(h_ref, g_ref, w_ref, cos_ref, sin_ref, out_ref):
    xn = (_rms(h_ref[...]) * g_ref[...]).astype(BF16)
    cos = cos_ref[...]
    sin = sin_ref[...]
    scale = HEAD_DIM ** -0.5
    for j in range(N_GROUPS):
        acc = _dot(xn, w_ref[:, j * GROUP:(j + 1) * GROUP])
        if j in (0, 1):
            for hh in range(N_HEADS):
                sl = acc[:, hh * HEAD_DIM:(hh + 1) * HEAD_DIM]
                rot = sl * cos + pltpu.roll(sl, HEAD_DIM // 2, 1) * sin
                if j == 1:
                    rot = rot * scale
                out_ref[:, j * GROUP + hh * HEAD_DIM:j * GROUP + (hh + 1) * HEAD_DIM] = rot.astype(BF16)
            continue
        if j in (3, 8):
            acc = _silu(acc)
        elif j == 4:
            acc = _silu(acc) * scale
        out_ref[:, j * GROUP:(j + 1) * GROUP] = acc.astype(BF16)


def _inproj(h, g, w_bf, cos, sin, tm):
    B, S, D = h.shape
    ncol = w_bf.shape[1]
    return pl.pallas_call(
        _inproj_kernel,
        grid=(B, S // tm),
        in_specs=[
            pl.BlockSpec((None, tm, D), lambda b, i: (b, i, 0)),
            pl.BlockSpec((1, D), lambda b, i: (0, 0)),
            pl.BlockSpec((D, ncol), lambda b, i: (0, 0), pipeline_mode=pl.Buffered(1)),
            pl.BlockSpec((tm, HEAD_DIM), lambda b, i: (i, 0)),
            pl.BlockSpec((tm, HEAD_DIM), lambda b, i: (i, 0)),
        ],
        out_specs=pl.BlockSpec((None, tm, ncol), lambda b, i: (b, i, 0)),
        out_shape=jax.ShapeDtypeStruct((B, S, ncol), BF16),
        compiler_params=_cparams(2),
        name="inproj",
    )(h, g.reshape(1, D), w_bf, cos, sin)


def _ret_kernel(q_ref, k_ref, v_ref, g_ref, dmat_ref, qdec_ref, kdec_ref, cdec_ref, rg_ref,
                out_ref, acc_ref, *, n_chunks):
    L = RET_CHUNK
    cdec = cdec_ref[0:1, :]
    zero_state = jnp.zeros((HEAD_DIM, HEAD_DIM), F32)

    def rows(i):
        return pl.ds(pl.multiple_of(i * L, L), L)

    def finish(sl, o):
        y = _rms(o) * rg_ref[...] * g_ref[sl, :].astype(F32)
        out_ref[sl, :] = y.astype(BF16)

    def step(i, states, second):
        state_f, state_b = states
        sf, sb = rows(i), rows(n_chunks - 1 - i)
        q, k, v = q_ref[sf, :], k_ref[sf, :], v_ref[sf, :]
        s = (_dot_nt(q, k) * dmat_ref[...]).astype(BF16)
        qd = (q.astype(F32) * qdec_ref[...]).astype(BF16)
        of = _dot(s, v) + _dot(qd, state_f.astype(BF16))
        kd = (k.astype(F32) * kdec_ref[...]).astype(BF16)
        state_f = cdec * state_f + _dot_tn(kd, v)

        q, k, v = q_ref[sb, :], k_ref[sb, :], v_ref[sb, :]
        qd = (q.astype(F32) * kdec_ref[...]).astype(BF16)
        ob = _dot(qd, state_b.astype(BF16))
        kd = (k.astype(F32) * qdec_ref[...]).astype(BF16)
        state_b = cdec * state_b + _dot_tn(kd, v)
        if second:
            finish(sf, acc_ref[sf, :] + of)
            finish(sb, acc_ref[sb, :] + ob)
        else:
            acc_ref[sf, :] = of
            acc_ref[sb, :] = ob
        return state_f, state_b

    half = n_chunks // 2
    states = lax.fori_loop(0, half, functools.partial(step, second=False), (zero_state, zero_state))
    lax.fori_loop(half, n_chunks, functools.partial(step, second=True), states)


def _ret_tables():
    L = RET_CHUNK
    t = np.arange(L, dtype=np.float64)
    lg = np.log1p(-(2.0 ** (-5.0 - np.arange(N_HEADS, dtype=np.float64))))
    dmat = np.exp(lg[:, None, None] * np.abs(t[:, None] - t[None, :])[None])
    qdec = np.exp(lg[:, None] * (t + 1.0))[:, :, None] * np.ones((1, 1, HEAD_DIM))
    kdec = np.exp(lg[:, None] * (L - 1.0 - t))[:, :, None] * np.ones((1, 1, HEAD_DIM))
    cdec = np.exp(lg * L)[:, None, None] * np.ones((1, 8, HEAD_DIM))
    return tuple(jnp.asarray(a, F32) for a in (dmat, qdec, kdec, cdec))


def _retention(proj, ret_g):
    B, S, _ = proj.shape
    L = RET_CHUNK
    assert S % (2 * L) == 0, "retention pairs chunk i with chunk n-1-i"
    dmat, qdec, kdec, cdec = _ret_tables()

    def col(c0):
        return pl.BlockSpec((None, S, HEAD_DIM), lambda b, h: (b, 0, c0 + h))

    def tab(r):
        return pl.BlockSpec((None, r, HEAD_DIM), lambda b, h: (h, 0, 0))

    return pl.pallas_call(
        functools.partial(_ret_kernel, n_chunks=S // L),
        grid=(B, N_HEADS),
        in_specs=[col(0), col(N_HEADS), col(2 * N_HEADS), col(3 * N_HEADS),
                  pl.BlockSpec((None, L, L), lambda b, h: (h, 0, 0)),
                  tab(L), tab(L), tab(8),
                  pl.BlockSpec((1, HEAD_DIM), lambda b, h: (0, h))],
        out_specs=pl.BlockSpec((None, S, HEAD_DIM), lambda b, h: (b, 0, h)),
        out_shape=jax.ShapeDtypeStruct((B, S, GROUP), BF16),
        scratch_shapes=[pltpu.VMEM((S, HEAD_DIM), F32)],
        compiler_params=_cparams(2),
        name="retention",
    )(proj, proj, proj, proj, dmat, qdec, kdec, cdec, ret_g.reshape(1, GROUP))


def _anchor_bcast(c, m, r):
    L = c.shape[0]
    blk = 2 * m
    if blk >= 8:
        c3 = c.reshape(L // blk, blk, HEAD_DIM)
        return jnp.broadcast_to(c3[:, r:r + 1, :], c3.shape).reshape(L, HEAD_DIM)
    c3 = c.reshape(L // 8, 8, HEAD_DIM)
    sub = lax.broadcasted_iota(jnp.int32, c3.shape, 1)
    out = jnp.broadcast_to(c3[:, r:r + 1, :], c3.shape)
    for j in range(1, 8 // blk):
        row = j * blk + r
        out = jnp.where(sub >= j * blk, jnp.broadcast_to(c3[:, row:row + 1, :], c3.shape), out)
    return out.reshape(L, HEAD_DIM)


def _split3(x):
    hi = x.astype(BF16)
    r1 = x - hi.astype(F32)
    mid = r1.astype(BF16)
    lo = (r1 - mid.astype(F32)).astype(BF16)
    return hi, mid, lo


def _hgrn_chunk(q, z, v, lb, state_t, tri, lvl, *, backward, first_layer):
    L = HGRN_CHUNK
    zf = z.astype(F32)
    e = jnp.exp(-jnp.abs(zf))
    inv = 1.0 / (1.0 + e)
    pos = zf >= 0.0
    sig = jnp.where(pos, inv, e * inv)
    sigm = jnp.where(pos, e * inv, inv)
    if first_layer:
        logf = jnp.minimum(zf, 0.0) - jnp.log(1.0 + e)
        kk = sigm
    else:
        logf = jnp.log(lb + (1.0 - lb) * sig)
        kk = (1.0 - lb) * sigm
    c3 = _dot(tri, jnp.concatenate(_split3(logf), axis=1))
    c = (c3[:, :HEAD_DIM] + c3[:, HEAD_DIM:2 * HEAD_DIM]) + c3[:, 2 * HEAD_DIM:]

    qf = q.astype(F32)
    scores = jnp.where(lvl == -1, _dot_nt(q, kk.astype(BF16)), 0.0)
    m, level = 1, 0
    while m < L:
        anchor = _anchor_bcast(c, m, m if backward else m - 1)
        p = jnp.exp(-jnp.abs(c - anchor))
        s = _dot_nt((qf * p).astype(BF16), (kk * p).astype(BF16))
        scores = jnp.where(lvl == level, s, scores)
        m, level = 2 * m, level + 1

    edge = 0 if backward else L - 1
    c_end = c[edge:edge + 1, :]
    qe = (qf * jnp.exp(c)).astype(BF16)
    out = _dot(scores.astype(BF16), v) + _dot_nt(qe, state_t.astype(BF16))
    ke = (kk * jnp.exp(c_end - c)).astype(BF16)
    state_t = jnp.exp(c_end) * state_t + _dot_tn(v, ke)
    return out, state_t


def _hgrn_kernel(q_ref, zf_ref, zb_ref, v_ref, lb_ref, trif_ref, trib_ref, lvlf_ref, lvlb_ref,
                 out_ref, *, n_chunks, first_layer):
    L = HGRN_CHUNK
    lb = lb_ref[...]
    zero_state = jnp.zeros((HEAD_DIM, HEAD_DIM), F32)

    def rows(i):
        return pl.ds(pl.multiple_of(i * L, L), L)

    def step(i, states, accumulate):
        sf, sb = rows(i), rows(n_chunks - 1 - i)
        of, state_f = _hgrn_chunk(q_ref[sf, :], zf_ref[sf, :], v_ref[sf, :], lb, states[0],
                                  trif_ref[...], lvlf_ref[...], backward=False, first_layer=first_layer)
        ob, state_b = _hgrn_chunk(q_ref[sb, :], zb_ref[sb, :], v_ref[sb, :], lb, states[1],
                                  trib_ref[...], lvlb_ref[...], backward=True, first_layer=first_layer)
        if accumulate:
            out_ref[sf, :] += of
            out_ref[sb, :] += ob
        else:
            out_ref[sf, :] = of
            out_ref[sb, :] = ob
        return state_f, state_b

    half = n_chunks // 2
    states = lax.fori_loop(0, half, functools.partial(step, accumulate=False), (zero_state, zero_state))
    lax.fori_loop(half, n_chunks, functools.partial(step, accumulate=True), states)


def _hgrn_tables():
    L = HGRN_CHUNK
    t = np.arange(L)
    tri_f = (t[None, :] <= t[:, None]).astype(np.float32)
    tri_b = (t[None, :] >= t[:, None]).astype(np.float32)
    x = t[:, None] ^ t[None, :]
    lev = np.floor(np.log2(np.maximum(x, 1))).astype(np.int32)
    lvl_f = np.where(t[:, None] > t[None, :], lev, np.where(x == 0, -1, -2)).astype(np.int32)
    lvl_b = np.where(t[:, None] < t[None, :], lev, np.where(x == 0, -1, -2)).astype(np.int32)
    return (jnp.asarray(tri_f, BF16), jnp.asarray(tri_b, BF16),
            jnp.asarray(lvl_f, jnp.int32), jnp.asarray(lvl_b, jnp.int32))


def _hgrn(proj, lb, first_layer):
    B, S, _ = proj.shape
    L = HGRN_CHUNK
    assert S % (2 * L) == 0, "HGRN2 pairs chunk i with chunk n-1-i"
    tri_f, tri_b, lvl_f, lvl_b = _hgrn_tables()

    def col(g):
        return pl.BlockSpec((None, S, HEAD_DIM), lambda b, h: (b, 0, g * N_HEADS + h))

    def full():
        return pl.BlockSpec((L, L), lambda b, h: (0, 0))

    return pl.pallas_call(
        functools.partial(_hgrn_kernel, n_chunks=S // L, first_layer=first_layer),
        grid=(B, N_HEADS),
        in_specs=[col(4), col(5), col(6), col(7),
                  pl.BlockSpec((1, HEAD_DIM), lambda b, h: (0, h)),
                  full(), full(), full(), full()],
        out_specs=pl.BlockSpec((None, S, HEAD_DIM), lambda b, h: (b, 0, h)),
        out_shape=jax.ShapeDtypeStruct((B, S, GROUP), F32),
        compiler_params=_cparams(2),
        name="hgrn",
    )(proj, proj, proj, proj, lb.reshape(1, GROUP), tri_f, tri_b, lvl_f, lvl_b)


def _outproj_kernel(r_ref, hraw_ref, hg_ref, hgn_ref, w_ref, h_ref, g2_ref, wr_ref,
                    h1_ref, xn_ref, aff_ref):
    hn = _rms(hraw_ref[...]) * hgn_ref[...] * hg_ref[...].astype(F32)
    mix = _dot(r_ref[...], w_ref[0:GROUP, :]) + _dot(hn.astype(BF16), w_ref[GROUP:2 * GROUP, :])
    h1 = h_ref[...] + mix
    h1_ref[...] = h1
    xn = (_rms(h1) * g2_ref[...]).astype(BF16)
    xn_ref[...] = xn
    logits = _dot_nt(wr_ref[...], xn)
    mx = jnp.max(logits, axis=0, keepdims=True)
    ex = jnp.exp(logits - mx)
    aff_ref[...] = ex / jnp.sum(ex, axis=0, keepdims=True)


def _outproj(r_out, h_raw, proj, hgrn_g, w_out_bf, h, g2, wr_t_bf, tm):
    B, S, D = h.shape
    E = wr_t_bf.shape[0]
    return pl.pallas_call(
        _outproj_kernel,
        grid=(B, S // tm),
        in_specs=[
            pl.BlockSpec((None, tm, GROUP), lambda b, i: (b, i, 0)),
            pl.BlockSpec((None, tm, GROUP), lambda b, i: (b, i, 0)),
            pl.BlockSpec((None, tm, GROUP), lambda b, i: (b, i, N_GROUPS - 1)),
            pl.BlockSpec((1, GROUP), lambda b, i: (0, 0)),
            pl.BlockSpec((2 * GROUP, D), lambda b, i: (0, 0)),
            pl.BlockSpec((None, tm, D), lambda b, i: (b, i, 0)),
            pl.BlockSpec((1, D), lambda b, i: (0, 0)),
            pl.BlockSpec((E, D), lambda b, i: (0, 0)),
        ],
        out_specs=[
            pl.BlockSpec((None, tm, D), lambda b, i: (b, i, 0)),
            pl.BlockSpec((None, tm, D), lambda b, i: (b, i, 0)),
            pl.BlockSpec((None, E, tm), lambda b, i: (b, 0, i)),
        ],
        out_shape=[jax.ShapeDtypeStruct((B, S, D), F32),
                   jax.ShapeDtypeStruct((B, S, D), BF16),
                   jax.ShapeDtypeStruct((B, E, S), F32)],
        compiler_params=_cparams(2),
        name="outproj",
    )(r_out, h_raw, proj, hgrn_g.reshape(1, GROUP), w_out_bf, h, g2.reshape(1, D), wr_t_bf)


def _select_kernel(aff_ref, pos_ref, gate_ref, *, cap):
    a = aff_ref[...]
    E, S = a.shape
    u = pltpu.bitcast(a, jnp.int32)
    capf = jnp.float32(cap)

    def count(mask):
        return jnp.sum(jnp.where(mask, 1.0, 0.0), axis=-1, keepdims=True)

    def value_bit(i, thr):
        cand = thr | (jnp.int32(1) << (30 - i))
        return jnp.where(count(u >= cand) >= capf, cand, thr)

    thr = lax.fori_loop(0, 31, value_bit, jnp.zeros((E, 1), jnp.int32))
    gt = u > thr
    eq = u == thr
    need = capf - count(gt)
    idx = lax.broadcasted_iota(jnp.int32, (E, S), 1)
    nbits = int(S).bit_length()

    def index_bit(i, cut):
        cand = cut | (jnp.int32(1) << (nbits - 1 - i))
        return jnp.where(count(eq & (idx < cand)) <= need, cand, cut)

    cut = lax.fori_loop(0, nbits, index_bit, jnp.zeros((E, 1), jnp.int32))
    sel = gt | (eq & (idx < cut))
    gate_ref[...] = jnp.where(sel, a, 0.0)

    li = lax.broadcasted_iota(jnp.int32, (128, 128), 0)
    lj = lax.broadcasted_iota(jnp.int32, (128, 128), 1)
    upper = jnp.where(li <= lj, 1.0, 0.0).astype(BF16)
    carry = jnp.zeros((E, 1), F32)
    for j in range(S // 128):
        sb = jnp.where(sel[:, j * 128:(j + 1) * 128], 1.0, 0.0)
        incl = _dot(sb.astype(BF16), upper)
        pos_ref[:, j * 128:(j + 1) * 128] = (incl - sb + carry).astype(jnp.int32)
        carry = carry + incl[:, 127:128]


def _select(aff, cap):
    B, E, S = aff.shape
    spec = pl.BlockSpec((None, E, S), lambda b: (b, 0, 0))
    return pl.pallas_call(
        functools.partial(_select_kernel, cap=cap),
        grid=(B,),
        in_specs=[spec],
        out_specs=[spec, spec],
        out_shape=[jax.ShapeDtypeStruct((B, E, S), jnp.int32),
                   jax.ShapeDtypeStruct((B, E, S), F32)],
        compiler_params=_cparams(1),
        name="select",
    )(aff)


def _moe_kernel(bs_ref, x_ref, pos_ref, gate_ref, wg_ref, wu_ref, wd_ref, ye_ref, xe_ref,
                *, n_tb, tk, wn, fm, cap, n_exp):
    b = pl.program_id(0)
    e = pl.program_id(1)
    base = (b * n_exp + e) * (n_tb + 1)
    lane_slot = lax.broadcasted_iota(jnp.int32, (wn, tk), 0)
    xe_ref[...] = jnp.zeros_like(xe_ref)

    def gather(tb, carry):
        lo = bs_ref[base + tb]
        hi = bs_ref[base + tb + 1]
        lo_al = (lo // 8) * 8
        n_win = jnp.where(hi > lo, (hi - lo_al + wn - 1) // wn, 0)
        key = jnp.where(gate_ref[pl.ds(tb, 1), :] > 0.0, pos_ref[pl.ds(tb, 1), :], -1)
        rows = pl.ds(pl.multiple_of(tb * tk, tk), tk)

        def window(w, c):
            first = lo_al + w * wn
            start = pl.multiple_of(jnp.minimum(first, cap - wn), 8)
            slot = start + lane_slot
            hit = jnp.logical_and(key == slot, slot >= first)
            onehot = jnp.where(hit, 1.0, 0.0).astype(BF16)
            xe_ref[pl.ds(start, wn), :] += _dot(onehot, x_ref[rows, :])
            return c

        lax.fori_loop(0, n_win, window, 0)
        return carry

    lax.fori_loop(0, n_tb, gather, 0)
    for r in range(cap // fm):
        xe = xe_ref[r * fm:(r + 1) * fm, :].astype(BF16)
        hid = (_silu(_dot(xe, wg_ref[...])) * _dot(xe, wu_ref[...])).astype(BF16)
        ye_ref[r * fm:(r + 1) * fm, :] = _dot(hid, wd_ref[...]).astype(BF16)


def _moe_ffn(bs, xn, pos4, gate4, wg, wu, wd, cap, tk, wn, fm):
    B, S, D = xn.shape
    E, _, FF = wg.shape
    n_tb = S // tk
    grid_spec = pltpu.PrefetchScalarGridSpec(
        num_scalar_prefetch=1,
        grid=(B, E),
        in_specs=[
            pl.BlockSpec((None, S, D), lambda b, e, bs: (b, 0, 0), pipeline_mode=pl.Buffered(1)),
            pl.BlockSpec((None, None, n_tb, tk), lambda b, e, bs: (b, e, 0, 0)),
            pl.BlockSpec((None, None, n_tb, tk), lambda b, e, bs: (b, e, 0, 0)),
            pl.BlockSpec((None, D, FF), lambda b, e, bs: (e, 0, 0)),
            pl.BlockSpec((None, D, FF), lambda b, e, bs: (e, 0, 0)),
            pl.BlockSpec((None, FF, D), lambda b, e, bs: (e, 0, 0)),
        ],
        out_specs=pl.BlockSpec((None, None, cap, D), lambda b, e, bs: (b, e, 0, 0)),
        scratch_shapes=[pltpu.VMEM((cap, D), F32)],
    )
    return pl.pallas_call(
        functools.partial(_moe_kernel, n_tb=n_tb, tk=tk, wn=wn, fm=fm, cap=cap, n_exp=E),
        grid_spec=grid_spec,
        out_shape=jax.ShapeDtypeStruct((B, E, cap, D), BF16),
        compiler_params=_cparams(2),
        name="moe_ffn",
    )(bs, xn, pos4, gate4, wg, wu, wd)


def _combine_kernel(bs_ref, h1_ref, ye_ref, pos_ref, gate_ref, fg_ref, out_ref, w_ref, y_ref,
                    *, n_tb, tk, wn, cap, n_exp, final_norm):
    b = pl.program_id(0)
    tb = pl.program_id(1)
    lane_slot = lax.broadcasted_iota(jnp.int32, (wn, tk), 0)

    def window_start(first):
        return pl.multiple_of(jnp.minimum(first, cap - wn), 16)

    def weights(e, start, first):
        slot = start + lane_slot
        hit = jnp.logical_and(pos_ref[e:e + 1, :] == slot, slot >= first)
        return jnp.where(hit, gate_ref[e:e + 1, :], 0.0).astype(BF16)

    firsts = []
    for e in range(n_exp):
        base = (b * n_exp + e) * (n_tb + 1)
        first = (bs_ref[base + tb] // 16) * 16
        start = window_start(first)
        firsts.append((first, bs_ref[base + tb + 1]))
        w_ref[e * wn:(e + 1) * wn, :] = weights(e, start, first)
        y_ref[e * wn:(e + 1) * wn, :] = ye_ref[e, pl.ds(start, wn), :]
    out_ref[...] = h1_ref[...] + _dot_tn(w_ref[...], y_ref[...])

    for e in range(n_exp):
        first0, end = firsts[e]
        n_more = jnp.maximum(end - first0 - 1, 0) // wn

        def window(w, carry):
            first = first0 + (w + 1) * wn
            start = window_start(first)
            out_ref[...] += _dot_tn(weights(e, start, first), ye_ref[e, pl.ds(start, wn), :])
            return carry

        lax.fori_loop(0, n_more, window, 0)
    if final_norm:
        out_ref[...] = _rms(out_ref[...]) * fg_ref[...]


def _combine(bs, h1, ye, pos_t, gate_t, final_g, cap, tk, wn, final_norm):
    B, S, D = h1.shape
    E = ye.shape[1]
    n_tb = S // tk
    grid_spec = pltpu.PrefetchScalarGridSpec(
        num_scalar_prefetch=1,
        grid=(B, n_tb),
        in_specs=[
            pl.BlockSpec((None, tk, D), lambda b, t, bs: (b, t, 0)),
            pl.BlockSpec((None, E, cap, D), lambda b, t, bs: (b, 0, 0, 0), pipeline_mode=pl.Buffered(1)),
            pl.BlockSpec((None, None, E, tk), lambda b, t, bs: (b, t, 0, 0)),
            pl.BlockSpec((None, None, E, tk), lambda b, t, bs: (b, t, 0, 0)),
            pl.BlockSpec((1, D), lambda b, t, bs: (0, 0)),
        ],
        out_specs=pl.BlockSpec((None, tk, D), lambda b, t, bs: (b, t, 0)),
        scratch_shapes=[pltpu.VMEM((E * wn, tk), BF16), pltpu.VMEM((E * wn, D), BF16)],
    )
    return pl.pallas_call(
        functools.partial(_combine_kernel, n_tb=n_tb, tk=tk, wn=wn, cap=cap, n_exp=E,
                          final_norm=final_norm),
        grid_spec=grid_spec,
        out_shape=jax.ShapeDtypeStruct((B, S, D), F32),
        compiler_params=_cparams(2),
        name="combine",
    )(bs, h1, ye, pos_t, gate_t, final_g.reshape(1, D))


def _rope_tables(S):
    half = HEAD_DIM // 2
    inv_freq = ROPE_BASE ** (-jnp.arange(half, dtype=F32) / half)
    ang = jnp.arange(S).astype(F32)[:, None] * inv_freq[None, :]
    cos, sin = jnp.cos(ang), jnp.sin(ang)
    return jnp.concatenate([cos, cos], axis=-1), jnp.concatenate([-sin, sin], axis=-1)


def _block_starts(pos, tk, cap):
    B, E, _ = pos.shape
    bs = jnp.concatenate([pos[:, :, ::tk], jnp.full((B, E, 1), cap, jnp.int32)], axis=-1)
    return bs.reshape(-1)


def kernel(x, norm1_g, w_in, ret_norm_g, hgrn_norm_g, w_out, lower_bounds, norm2_g, w_router,
           w_gate, w_up, w_down, final_norm_g):
    B, S, D = x.shape
    depth = w_in.shape[0]
    E = w_router.shape[-1]
    cap = CAPACITY_FACTOR * S // E
    tm = min(512, S)
    tk = min(512, S)
    wn = min(128, cap)
    fm = min(512, cap)
    n_tb = S // tk

    lbs = jax.nn.softmax(lower_bounds.astype(F32), axis=0)
    lbs = jnp.cumsum(lbs, axis=0) - lbs[0]
    cos, sin = _rope_tables(S)

    h = x
    for layer in range(depth):
        proj = _inproj(h, norm1_g[layer], w_in[layer].astype(BF16), cos, sin, tm)
        r_out = _retention(proj, ret_norm_g[layer])
        h_raw = _hgrn(proj, lbs[layer], first_layer=(layer == 0))
        h1, xn, aff = _outproj(r_out, h_raw, proj, hgrn_norm_g[layer], w_out[layer].astype(BF16),
                               h, norm2_g[layer], w_router[layer].T.astype(BF16), tm)
        pos, gate = _select(aff, cap)
        bs = _block_starts(pos, tk, cap)
        ye = _moe_ffn(bs, xn, pos.reshape(B, E, n_tb, tk), gate.reshape(B, E, n_tb, tk),
                      w_gate[layer].astype(BF16), w_up[layer].astype(BF16),
                      w_down[layer].astype(BF16), cap, tk, wn, fm)
        pos_t = pos.reshape(B, E, n_tb, tk).transpose(0, 2, 1, 3)
        gate_t = gate.reshape(B, E, n_tb, tk).transpose(0, 2, 1, 3)
        h = _combine(bs, h1, ye, pos_t, gate_t, final_norm_g, cap, tk, wn,
                     final_norm=(layer == depth - 1))
    return h
```

```python
import functools

import numpy as np
import jax
import jax.numpy as jnp
from jax import lax
from jax.experimental import pallas as pl
from jax.experimental.pallas import tpu as pltpu

F32 = jnp.float32
BF16 = jnp.bfloat16

HEAD_DIM = 128
N_HEADS = 4
GROUP = N_HEADS * HEAD_DIM
N_GROUPS = 9
ROPE_BASE = 10000.0
NORM_EPS = 1e-6
CAPACITY_FACTOR = 2

RET_CHUNK = 256
HGRN_CHUNK = 128
HGRN_PAIR = 4
LOG2E = 1.4426950408889634
VMEM_LIMIT = 58 * 1024 * 1024


def _cparams(n_axes):
    return pltpu.CompilerParams(
        dimension_semantics=("arbitrary",) * n_axes, vmem_limit_bytes=VMEM_LIMIT)


def _dot(a, b):
    return jnp.dot(a, b, preferred_element_type=F32)


def _dot_nt(a, b):
    return lax.dot_general(a, b, (((1,), (1,)), ((), ())), preferred_element_type=F32)


def _dot_tn(a, b):
    return lax.dot_general(a, b, (((0,), (0,)), ((), ())), preferred_element_type=F32)


def _silu(x):
    return x * (1.0 / (1.0 + jnp.exp(-x)))


def _rms(x):
    return x * lax.rsqrt(jnp.mean(x * x, axis=-1, keepdims=True) + NORM_EPS)


def _inproj_kernel(h_ref, g_ref, w_ref, cos_ref, sin_ref, out_ref):
    xn = (_rms(h_ref[...]) * g_ref[...]).astype(BF16)
    cos = cos_ref[...]
    sin = sin_ref[...]
    scale = HEAD_DIM ** -0.5
    for j in range(N_GROUPS):
        acc = _dot(xn, w_ref[:, j * GROUP:(j + 1) * GROUP])
        if j in (0, 1):
            for hh in range(N_HEADS):
                sl = acc[:, hh * HEAD_DIM:(hh + 1) * HEAD_DIM]
                rot = sl * cos + pltpu.roll(sl, HEAD_DIM // 2, 1) * sin
                if j == 1:
                    rot = rot * scale
                out_ref[:, j * GROUP + hh * HEAD_DIM:j * GROUP + (hh + 1) * HEAD_DIM] = rot.astype(BF16)
            continue
        if j in (3, 8):
            acc = _silu(acc)
        elif j == 4:
            acc = _silu(acc) * scale
        out_ref[:, j * GROUP:(j + 1) * GROUP] = acc.astype(BF16)


def _inproj(h, g, w_bf, cos, sin, tm):
    B, S, D = h.shape
    ncol = w_bf.shape[1]
    return pl.pallas_call(
        _inproj_kernel,
        grid=(B, S // tm),
        in_specs=[
            pl.BlockSpec((None, tm, D), lambda b, i: (b, i, 0)),
            pl.BlockSpec((1, D), lambda b, i: (0, 0)),
            pl.BlockSpec((D, ncol), lambda b, i: (0, 0), pipeline_mode=pl.Buffered(1)),
            pl.BlockSpec((tm, HEAD_DIM), lambda b, i: (i, 0)),
            pl.BlockSpec((tm, HEAD_DIM), lambda b, i: (i, 0)),
        ],
        out_specs=pl.BlockSpec((None, tm, ncol), lambda b, i: (b, i, 0)),
        out_shape=jax.ShapeDtypeStruct((B, S, ncol), BF16),
        compiler_params=_cparams(2),
        name="inproj",
    )(h, g.reshape(1, D), w_bf, cos, sin)


def _ret_kernel(q_ref, k_ref, v_ref, g_ref, dmat_ref, qdec_ref, kdec_ref, cdec_ref, rg_ref,
                out_ref, acc_ref, *, n_chunks):
    L = RET_CHUNK
    cdec = cdec_ref[0:1, :]
    zero_state = jnp.zeros((HEAD_DIM, HEAD_DIM), F32)

    def rows(i):
        return pl.ds(pl.multiple_of(i * L, L), L)

    def finish(sl, o):
        y = _rms(o) * rg_ref[...] * g_ref[sl, :].astype(F32)
        out_ref[sl, :] = y.astype(BF16)

    def step(i, states, second):
        state_f, state_b = states
        sf, sb = rows(i), rows(n_chunks - 1 - i)
        q, k, v = q_ref[sf, :], k_ref[sf, :], v_ref[sf, :]
        s = (_dot_nt(q, k) * dmat_ref[...]).astype(BF16)
        qd = (q.astype(F32) * qdec_ref[...]).astype(BF16)
        of = _dot(s, v) + _dot(qd, state_f.astype(BF16))
        kd = (k.astype(F32) * kdec_ref[...]).astype(BF16)
        state_f = cdec * state_f + _dot_tn(kd, v)

        q, k, v = q_ref[sb, :], k_ref[sb, :], v_ref[sb, :]
        qd = (q.astype(F32) * kdec_ref[...]).astype(BF16)
        ob = _dot(qd, state_b.astype(BF16))
        kd = (k.astype(F32) * qdec_ref[...]).astype(BF16)
        state_b = cdec * state_b + _dot_tn(kd, v)
        if second:
            finish(sf, acc_ref[sf, :] + of)
            finish(sb, acc_ref[sb, :] + ob)
        else:
            acc_ref[sf, :] = of
            acc_ref[sb, :] = ob
        return state_f, state_b

    half = n_chunks // 2
    states = lax.fori_loop(0, half, functools.partial(step, second=False), (zero_state, zero_state))
    lax.fori_loop(half, n_chunks, functools.partial(step, second=True), states)


def _ret_tables():
    L = RET_CHUNK
    t = np.arange(L, dtype=np.float64)
    lg = np.log1p(-(2.0 ** (-5.0 - np.arange(N_HEADS, dtype=np.float64))))
    dmat = np.exp(lg[:, None, None] * np.abs(t[:, None] - t[None, :])[None])
    qdec = np.exp(lg[:, None] * (t + 1.0))[:, :, None] * np.ones((1, 1, HEAD_DIM))
    kdec = np.exp(lg[:, None] * (L - 1.0 - t))[:, :, None] * np.ones((1, 1, HEAD_DIM))
    cdec = np.exp(lg * L)[:, None, None] * np.ones((1, 8, HEAD_DIM))
    return tuple(jnp.asarray(a, F32) for a in (dmat, qdec, kdec, cdec))


def _retention(proj, ret_g):
    B, S, _ = proj.shape
    L = RET_CHUNK
    assert S % (2 * L) == 0, "retention pairs chunk i with chunk n-1-i"
    dmat, qdec, kdec, cdec = _ret_tables()

    def col(c0):
        return pl.BlockSpec((None, S, HEAD_DIM), lambda b, h: (b, 0, c0 + h))

    def tab(r):
        return pl.BlockSpec((None, r, HEAD_DIM), lambda b, h: (h, 0, 0))

    return pl.pallas_call(
        functools.partial(_ret_kernel, n_chunks=S // L),
        grid=(B, N_HEADS),
        in_specs=[col(0), col(N_HEADS), col(2 * N_HEADS), col(3 * N_HEADS),
                  pl.BlockSpec((None, L, L), lambda b, h: (h, 0, 0)),
                  tab(L), tab(L), tab(8),
                  pl.BlockSpec((1, HEAD_DIM), lambda b, h: (0, h))],
        out_specs=pl.BlockSpec((None, S, HEAD_DIM), lambda b, h: (b, 0, h)),
        out_shape=jax.ShapeDtypeStruct((B, S, GROUP), BF16),
        scratch_shapes=[pltpu.VMEM((S, HEAD_DIM), F32)],
        compiler_params=_cparams(2),
        name="retention",
    )(proj, proj, proj, proj, dmat, qdec, kdec, cdec, ret_g.reshape(1, GROUP))


def _hgrn_chunks(chains, lb, first_layer):
    L = HGRN_CHUNK
    n = len(chains)
    qs = [c[0] for c in chains]
    vs = [c[2] for c in chains]
    states = [c[3] for c in chains]
    lvls = [c[5] for c in chains]
    backs = [c[6] for c in chains]

    kks, decs = [], []
    for (_, z, _, _, mtab, _, _) in chains:
        zf = z.astype(F32)
        e = jnp.exp(-jnp.abs(zf))
        inv = 1.0 / (1.0 + e)
        pos = zf >= 0.0
        sigm = jnp.where(pos, e * inv, inv)
        if first_layer:
            logf2 = jnp.minimum(zf, 0.0) * LOG2E - jnp.log2(1.0 + e)
            kk = sigm
        else:
            sig = jnp.where(pos, inv, e * inv)
            logf2 = jnp.log2(lb + (1.0 - lb) * sig)
            kk = (1.0 - lb) * sigm
        hi = logf2.astype(BF16)
        lo = (logf2 - hi.astype(F32)).astype(BF16)
        kks.append(kk)
        decs.append(_dot(mtab, jnp.concatenate([hi, lo], axis=0)))

    qfs = [q.astype(F32) for q in qs]
    cs = [d[0:L, :] for d in decs]
    scores = [jnp.where(lvls[i] == -1, _dot_nt(qs[i], kks[i].astype(BF16)), 0.0) for i in range(n)]
    m, level = 1, 0
    while m < L:
        for i in range(n):
            if m < 8:
                p = jnp.exp2(decs[i][(1 + level) * L:(2 + level) * L, :])
                sub = lax.broadcasted_iota(jnp.int32, (1, 8, HEAD_DIM), 1)
                is_q = ((sub >> level) & 1) == (0 if backs[i] else 1)
                w = jnp.where(is_q, qfs[i].reshape(L // 8, 8, HEAD_DIM),
                              kks[i].reshape(L // 8, 8, HEAD_DIM)).reshape(L, HEAD_DIM)
                r = (w * p).astype(BF16)
            else:
                slabs = []
                for j in range(L // m):
                    mid = (j // 2) * 2 * m + m
                    anchor = cs[i][mid:mid + 1, :] if backs[i] else cs[i][mid - 1:mid, :]
                    rows = cs[i][j * m:(j + 1) * m, :]
                    if (j % 2 == 1) != backs[i]:
                        slabs.append(qfs[i][j * m:(j + 1) * m, :] * jnp.exp2(rows - anchor))
                    else:
                        slabs.append(kks[i][j * m:(j + 1) * m, :] * jnp.exp2(anchor - rows))
                r = jnp.concatenate(slabs, axis=0).astype(BF16)
            scores[i] = jnp.where(lvls[i] == level, _dot_nt(r, r), scores[i])
        m, level = 2 * m, level + 1

    results = []
    latest = {}
    for i in range(n):
        state = states[i] if states[i] is not None else latest[backs[i]]
        edge = 0 if backs[i] else L - 1
        c_end = cs[i][edge:edge + 1, :]
        qe = (qfs[i] * jnp.exp2(cs[i])).astype(BF16)
        out = _dot(scores[i].astype(BF16), vs[i]) + _dot_nt(qe, state.astype(BF16))
        ke = (kks[i] * jnp.exp2(c_end - cs[i])).astype(BF16)
        latest[backs[i]] = jnp.exp2(c_end) * state + _dot_tn(vs[i], ke)
        results.append((out, latest[backs[i]]))
    return results


def _hgrn_kernel(q_ref, zf_ref, zb_ref, v_ref, lb_ref, mtabf_ref, mtabb_ref, lvlf_ref, lvlb_ref,
                 out_ref, *, n_chunks, first_layer):
    L = HGRN_CHUNK
    lb = lb_ref[...]
    zero_state = jnp.zeros((HEAD_DIM, HEAD_DIM), F32)

    def rows(i):
        return pl.ds(pl.multiple_of(i * L, L), L)

    def step(i, states, accumulate):
        chains, slices = [], []
        for k in range(HGRN_PAIR):
            sf = rows(i * HGRN_PAIR + k)
            sb = rows(n_chunks - 1 - i * HGRN_PAIR - k)
            chains.append((q_ref[sf, :], zf_ref[sf, :], v_ref[sf, :], states[0] if k == 0 else None,
                           mtabf_ref[...], lvlf_ref[...], False))
            chains.append((q_ref[sb, :], zb_ref[sb, :], v_ref[sb, :], states[1] if k == 0 else None,
                           mtabb_ref[...], lvlb_ref[...], True))
            slices += [sf, sb]
        results = _hgrn_chunks(chains, lb, first_layer)
        for sl, (o, _) in zip(slices, results):
            if accumulate:
                out_ref[sl, :] += o
            else:
                out_ref[sl, :] = o
        return results[-2][1], results[-1][1]

    steps = n_chunks // HGRN_PAIR
    states = lax.fori_loop(0, steps // 2, functools.partial(step, accumulate=False),
                           (zero_state, zero_state))
    lax.fori_loop(steps // 2, steps, functools.partial(step, accumulate=True), states)


def _hgrn_tables():
    L = HGRN_CHUNK
    t = np.arange(L)[:, None]
    u = np.arange(L)[None, :]
    x = t ^ u
    lev = np.floor(np.log2(np.maximum(x, 1))).astype(np.int32)
    lvl_f = np.where(t > u, lev, np.where(x == 0, -1, -2)).astype(np.int32)
    lvl_b = np.where(t < u, lev, np.where(x == 0, -1, -2)).astype(np.int32)

    def exponent_rows(backward):
        blocks = [(u >= t) if backward else (u <= t)]
        m = 1
        while m < 8:
            mid = (t // (2 * m)) * (2 * m) + m
            if backward:
                blocks.append(np.where(t < mid, (u >= t) & (u < mid), (u >= mid) & (u < t)))
            else:
                blocks.append(np.where(t >= mid, (u >= mid) & (u <= t), (u > t) & (u < mid)))
            m *= 2
        tab = np.concatenate(blocks, axis=0).astype(np.float32)
        return np.concatenate([tab, tab], axis=1)

    return (jnp.asarray(exponent_rows(False), BF16), jnp.asarray(exponent_rows(True), BF16),
            jnp.asarray(lvl_f, jnp.int32), jnp.asarray(lvl_b, jnp.int32))


def _hgrn(proj, lb, first_layer):
    B, S, _ = proj.shape
    L = HGRN_CHUNK
    assert S % (2 * HGRN_PAIR * L) == 0, "HGRN2 walks chunks from both ends, HGRN_PAIR at a time"
    mtab_f, mtab_b, lvl_f, lvl_b = _hgrn_tables()

    def col(g):
        return pl.BlockSpec((None, S, HEAD_DIM), lambda b, h: (b, 0, g * N_HEADS + h))

    def full(a):
        return pl.BlockSpec(a.shape, lambda b, h: (0, 0))

    return pl.pallas_call(
        functools.partial(_hgrn_kernel, n_chunks=S // L, first_layer=first_layer),
        grid=(B, N_HEADS),
        in_specs=[col(4), col(5), col(6), col(7),
                  pl.BlockSpec((1, HEAD_DIM), lambda b, h: (0, h)),
                  full(mtab_f), full(mtab_b), full(lvl_f), full(lvl_b)],
        out_specs=pl.BlockSpec((None, S, HEAD_DIM), lambda b, h: (b, 0, h)),
        out_shape=jax.ShapeDtypeStruct((B, S, GROUP), F32),
        compiler_params=_cparams(2),
        name="hgrn",
    )(proj, proj, proj, proj, lb.reshape(1, GROUP), mtab_f, mtab_b, lvl_f, lvl_b)


def _outproj_kernel(r_ref, hraw_ref, hg_ref, hgn_ref, w_ref, h_ref, g2_ref, wr_ref,
                    h1_ref, xn_ref, aff_ref):
    hn = _rms(hraw_ref[...]) * hgn_ref[...] * hg_ref[...].astype(F32)
    mix = _dot(r_ref[...], w_ref[0:GROUP, :]) + _dot(hn.astype(BF16), w_ref[GROUP:2 * GROUP, :])
    h1 = h_ref[...] + mix
    h1_ref[...] = h1
    xn = (_rms(h1) * g2_ref[...]).astype(BF16)
    xn_ref[...] = xn
    logits = _dot_nt(wr_ref[...], xn)
    mx = jnp.max(logits, axis=0, keepdims=True)
    ex = jnp.exp(logits - mx)
    aff_ref[...] = ex / jnp.sum(ex, axis=0, keepdims=True)


def _outproj(r_out, h_raw, proj, hgrn_g, w_out_bf, h, g2, wr_t_bf, tm):
    B, S, D = h.shape
    E = wr_t_bf.shape[0]
    return pl.pallas_call(
        _outproj_kernel,
        grid=(B, S // tm),
        in_specs=[
            pl.BlockSpec((None, tm, GROUP), lambda b, i: (b, i, 0)),
            pl.BlockSpec((None, tm, GROUP), lambda b, i: (b, i, 0)),
            pl.BlockSpec((None, tm, GROUP), lambda b, i: (b, i, N_GROUPS - 1)),
            pl.BlockSpec((1, GROUP), lambda b, i: (0, 0)),
            pl.BlockSpec((2 * GROUP, D), lambda b, i: (0, 0)),
            pl.BlockSpec((None, tm, D), lambda b, i: (b, i, 0)),
            pl.BlockSpec((1, D), lambda b, i: (0, 0)),
            pl.BlockSpec((E, D), lambda b, i: (0, 0)),
        ],
        out_specs=[
            pl.BlockSpec((None, tm, D), lambda b, i: (b, i, 0)),
            pl.BlockSpec((None, tm, D), lambda b, i: (b, i, 0)),
            pl.BlockSpec((None, E, tm), lambda b, i: (b, 0, i)),
        ],
        out_shape=[jax.ShapeDtypeStruct((B, S, D), F32),
                   jax.ShapeDtypeStruct((B, S, D), BF16),
                   jax.ShapeDtypeStruct((B, E, S), F32)],
        compiler_params=_cparams(2),
        name="outproj",
    )(r_out, h_raw, proj, hgrn_g.reshape(1, GROUP), w_out_bf, h, g2.reshape(1, D), wr_t_bf)


def _select_kernel(aff_ref, pos_ref, gate_ref, *, cap):
    a = aff_ref[...]
    E, S = a.shape
    u = pltpu.bitcast(a, jnp.int32)
    capf = jnp.float32(cap)

    def count(mask):
        return jnp.sum(jnp.where(mask, 1.0, 0.0), axis=-1, keepdims=True)

    def value_bit(i, thr):
        cand = thr | (jnp.int32(1) << (30 - i))
        return jnp.where(count(u >= cand) >= capf, cand, thr)

    thr = lax.fori_loop(0, 31, value_bit, jnp.zeros((E, 1), jnp.int32))
    gt = u > thr
    eq = u == thr
    need = capf - count(gt)
    idx = lax.broadcasted_iota(jnp.int32, (E, S), 1)
    nbits = int(S).bit_length()

    def index_bit(i, cut):
        cand = cut | (jnp.int32(1) << (nbits - 1 - i))
        return jnp.where(count(eq & (idx < cand)) <= need, cand, cut)

    cut = lax.fori_loop(0, nbits, index_bit, jnp.zeros((E, 1), jnp.int32))
    sel = gt | (eq & (idx < cut))
    gate_ref[...] = jnp.where(sel, a, 0.0)

    li = lax.broadcasted_iota(jnp.int32, (128, 128), 0)
    lj = lax.broadcasted_iota(jnp.int32, (128, 128), 1)
    upper = jnp.where(li <= lj, 1.0, 0.0).astype(BF16)
    carry = jnp.zeros((E, 1), F32)
    for j in range(S // 128):
        sb = jnp.where(sel[:, j * 128:(j + 1) * 128], 1.0, 0.0)
        incl = _dot(sb.astype(BF16), upper)
        pos_ref[:, j * 128:(j + 1) * 128] = (incl - sb + carry).astype(jnp.int32)
        carry = carry + incl[:, 127:128]


def _select(aff, cap):
    B, E, S = aff.shape
    spec = pl.BlockSpec((None, E, S), lambda b: (b, 0, 0))
    return pl.pallas_call(
        functools.partial(_select_kernel, cap=cap),
        grid=(B,),
        in_specs=[spec],
        out_specs=[spec, spec],
        out_shape=[jax.ShapeDtypeStruct((B, E, S), jnp.int32),
                   jax.ShapeDtypeStruct((B, E, S), F32)],
        compiler_params=_cparams(1),
        name="select",
    )(aff)


def _moe_kernel(bs_ref, x_ref, pos_ref, gate_ref, wg_ref, wu_ref, wd_ref, ye_ref, xe_ref,
                *, n_tb, tk, wn, fm, cap, n_exp):
    b = pl.program_id(0)
    e = pl.program_id(1)
    base = (b * n_exp + e) * (n_tb + 1)
    lane_slot = lax.broadcasted_iota(jnp.int32, (wn, tk), 0)
    xe_ref[...] = jnp.zeros_like(xe_ref)

    def gather(tb, carry):
        lo = bs_ref[base + tb]
        hi = bs_ref[base + tb + 1]
        lo_al = (lo // 8) * 8
        n_win = jnp.where(hi > lo, (hi - lo_al + wn - 1) // wn, 0)
        key = jnp.where(gate_ref[pl.ds(tb, 1), :] > 0.0, pos_ref[pl.ds(tb, 1), :], -1)
        rows = pl.ds(pl.multiple_of(tb * tk, tk), tk)

        def window(w, c):
            first = lo_al + w * wn
            start = pl.multiple_of(jnp.minimum(first, cap - wn), 8)
            slot = start + lane_slot
            hit = jnp.logical_and(key == slot, slot >= first)
            onehot = jnp.where(hit, 1.0, 0.0).astype(BF16)
            xe_ref[pl.ds(start, wn), :] += _dot(onehot, x_ref[rows, :])
            return c

        lax.fori_loop(0, n_win, window, 0)
        return carry

    lax.fori_loop(0, n_tb, gather, 0)
    for r in range(cap // fm):
        xe = xe_ref[r * fm:(r + 1) * fm, :].astype(BF16)
        hid = (_silu(_dot(xe, wg_ref[...])) * _dot(xe, wu_ref[...])).astype(BF16)
        ye_ref[r * fm:(r + 1) * fm, :] = _dot(hid, wd_ref[...]).astype(BF16)


def _moe_ffn(bs, xn, pos4, gate4, wg, wu, wd, cap, tk, wn, fm):
    B, S, D = xn.shape
    E, _, FF = wg.shape
    n_tb = S // tk
    grid_spec = pltpu.PrefetchScalarGridSpec(
        num_scalar_prefetch=1,
        grid=(B, E),
        in_specs=[
            pl.BlockSpec((None, S, D), lambda b, e, bs: (b, 0, 0), pipeline_mode=pl.Buffered(1)),
            pl.BlockSpec((None, None, n_tb, tk), lambda b, e, bs: (b, e, 0, 0)),
            pl.BlockSpec((None, None, n_tb, tk), lambda b, e, bs: (b, e, 0, 0)),
            pl.BlockSpec((None, D, FF), lambda b, e, bs: (e, 0, 0)),
            pl.BlockSpec((None, D, FF), lambda b, e, bs: (e, 0, 0)),
            pl.BlockSpec((None, FF, D), lambda b, e, bs: (e, 0, 0)),
        ],
        out_specs=pl.BlockSpec((None, None, cap, D), lambda b, e, bs: (b, e, 0, 0)),
        scratch_shapes=[pltpu.VMEM((cap, D), F32)],
    )
    return pl.pallas_call(
        functools.partial(_moe_kernel, n_tb=n_tb, tk=tk, wn=wn, fm=fm, cap=cap, n_exp=E),
        grid_spec=grid_spec,
        out_shape=jax.ShapeDtypeStruct((B, E, cap, D), BF16),
        compiler_params=_cparams(2),
        name="moe_ffn",
    )(bs, xn, pos4, gate4, wg, wu, wd)


def _combine_kernel(bs_ref, h1_ref, ye_ref, pos_ref, gate_ref, fg_ref, out_ref, w_ref, y_ref,
                    *, n_tb, tk, wn, cap, n_exp, final_norm):
    b = pl.program_id(0)
    tb = pl.program_id(1)
    lane_slot = lax.broadcasted_iota(jnp.int32, (wn, tk), 0)

    def window_start(first):
        return pl.multiple_of(jnp.minimum(first, cap - wn), 16)

    def weights(e, start, first):
        slot = start + lane_slot
        hit = jnp.logical_and(pos_ref[e:e + 1, :] == slot, slot >= first)
        return jnp.where(hit, gate_ref[e:e + 1, :], 0.0).astype(BF16)

    firsts = []
    for e in range(n_exp):
        base = (b * n_exp + e) * (n_tb + 1)
        first = (bs_ref[base + tb] // 16) * 16
        start = window_start(first)
        firsts.append((first, bs_ref[base + tb + 1]))
        w_ref[e * wn:(e + 1) * wn, :] = weights(e, start, first)
        y_ref[e * wn:(e + 1) * wn, :] = ye_ref[e, pl.ds(start, wn), :]
    out_ref[...] = h1_ref[...] + _dot_tn(w_ref[...], y_ref[...])

    for e in range(n_exp):
        first0, end = firsts[e]
        n_more = jnp.maximum(end - first0 - 1, 0) // wn

        def window(w, carry):
            first = first0 + (w + 1) * wn
            start = window_start(first)
            out_ref[...] += _dot_tn(weights(e, start, first), ye_ref[e, pl.ds(start, wn), :])
            return carry

        lax.fori_loop(0, n_more, window, 0)
    if final_norm:
        out_ref[...] = _rms(out_ref[...]) * fg_ref[...]


def _combine(bs, h1, ye, pos_t, gate_t, final_g, cap, tk, wn, final_norm):
    B, S, D = h1.shape
    E = ye.shape[1]
    n_tb = S // tk
    grid_spec = pltpu.PrefetchScalarGridSpec(
        num_scalar_prefetch=1,
        grid=(B, n_tb),
        in_specs=[
            pl.BlockSpec((None, tk, D), lambda b, t, bs: (b, t, 0)),
            pl.BlockSpec((None, E, cap, D), lambda b, t, bs: (b, 0, 0, 0), pipeline_mode=pl.Buffered(1)),
            pl.BlockSpec((None, None, E, tk), lambda b, t, bs: (b, t, 0, 0)),
            pl.BlockSpec((None, None, E, tk), lambda b, t, bs: (b, t, 0, 0)),
            pl.BlockSpec((1, D), lambda b, t, bs: (0, 0)),
        ],
        out_specs=pl.BlockSpec((None, tk, D), lambda b, t, bs: (b, t, 0)),
        scratch_shapes=[pltpu.VMEM((E * wn, tk), BF16), pltpu.VMEM((E * wn, D), BF16)],
    )
    return pl.pallas_call(
        functools.partial(_combine_kernel, n_tb=n_tb, tk=tk, wn=wn, cap=cap, n_exp=E,
                          final_norm=final_norm),
        grid_spec=grid_spec,
        out_shape=jax.ShapeDtypeStruct((B, S, D), F32),
        compiler_params=_cparams(2),
        name="combine",
    )(bs, h1, ye, pos_t, gate_t, final_g.reshape(1, D))


def _rope_tables(S):
    half = HEAD_DIM // 2
    inv_freq = ROPE_BASE ** (-jnp.arange(half, dtype=F32) / half)
    ang = jnp.arange(S).astype(F32)[:, None] * inv_freq[None, :]
    cos, sin = jnp.cos(ang), jnp.sin(ang)
    return jnp.concatenate([cos, cos], axis=-1), jnp.concatenate([-sin, sin], axis=-1)


def _block_starts(pos, tk, cap):
    B, E, _ = pos.shape
    bs = jnp.concatenate([pos[:, :, ::tk], jnp.full((B, E, 1), cap, jnp.int32)], axis=-1)
    return bs.reshape(-1)


def kernel(x, norm1_g, w_in, ret_norm_g, hgrn_norm_g, w_out, lower_bounds, norm2_g, w_router,
           w_gate, w_up, w_down, final_norm_g):
    B, S, D = x.shape
    depth = w_in.shape[0]
    E = w_router.shape[-1]
    cap = CAPACITY_FACTOR * S // E
    tm = min(512, S)
    tk = min(512, S)
    wn = min(128, cap)
    fm = min(512, cap)
    n_tb = S // tk

    lbs = jax.nn.softmax(lower_bounds.astype(F32), axis=0)
    lbs = jnp.cumsum(lbs, axis=0) - lbs[0]
    cos, sin = _rope_tables(S)

    h = x
    for layer in range(depth):
        proj = _inproj(h, norm1_g[layer], w_in[layer].astype(BF16), cos, sin, tm)
        r_out = _retention(proj, ret_norm_g[layer])
        h_raw = _hgrn(proj, lbs[layer], first_layer=(layer == 0))
        h1, xn, aff = _outproj(r_out, h_raw, proj, hgrn_norm_g[layer], w_out[layer].astype(BF16),
                               h, norm2_g[layer], w_router[layer].T.astype(BF16), tm)
        pos, gate = _select(aff, cap)
        bs = _block_starts(pos, tk, cap)
        ye = _moe_ffn(bs, xn, pos.reshape(B, E, n_tb, tk), gate.reshape(B, E, n_tb, tk),
                      w_gate[layer].astype(BF16), w_up[layer].astype(BF16),
                      w_down[layer].astype(BF16), cap, tk, wn, fm)
        pos_t = pos.reshape(B, E, n_tb, tk).transpose(0, 2, 1, 3)
        gate_t = gate.reshape(B, E, n_tb, tk).transpose(0, 2, 1, 3)
        h = _combine(bs, h1, ye, pos_t, gate_t, final_norm_g, cap, tk, wn,
                     final_norm=(layer == depth - 1))
    return h
```

```python
import functools

import numpy as np
import jax
import jax.numpy as jnp
from jax import lax
from jax.experimental import pallas as pl
from jax.experimental.pallas import tpu as pltpu

F32 = jnp.float32
BF16 = jnp.bfloat16

HEAD_DIM = 128
N_HEADS = 4
GROUP = N_HEADS * HEAD_DIM
N_GROUPS = 9
ROPE_BASE = 10000.0
NORM_EPS = 1e-6
CAPACITY_FACTOR = 2

RET_CHUNK = 256
RET_PAIR = 4
HGRN_CHUNK = 128
HGRN_PAIR = 4
LOG2E = 1.4426950408889634
VMEM_LIMIT = 58 * 1024 * 1024


def _cparams(n_axes):
    return pltpu.CompilerParams(
        dimension_semantics=("arbitrary",) * n_axes, vmem_limit_bytes=VMEM_LIMIT)


def _dot(a, b):
    return jnp.dot(a, b, preferred_element_type=F32)


def _dot_nt(a, b):
    return lax.dot_general(a, b, (((1,), (1,)), ((), ())), preferred_element_type=F32)


def _dot_tn(a, b):
    return lax.dot_general(a, b, (((0,), (0,)), ((), ())), preferred_element_type=F32)


def _silu(x):
    return x * (1.0 / (1.0 + jnp.exp(-x)))


def _rms(x):
    return x * lax.rsqrt(jnp.mean(x * x, axis=-1, keepdims=True) + NORM_EPS)


def _inproj_kernel(h_ref, g_ref, w_ref, cos_ref, sin_ref, out_ref):
    xn = (_rms(h_ref[...]) * g_ref[...]).astype(BF16)
    cos = cos_ref[...]
    sin = sin_ref[...]
    scale = HEAD_DIM ** -0.5
    for j in range(N_GROUPS):
        acc = _dot(xn, w_ref[:, j * GROUP:(j + 1) * GROUP])
        if j in (0, 1):
            for hh in range(N_HEADS):
                sl = acc[:, hh * HEAD_DIM:(hh + 1) * HEAD_DIM]
                rot = sl * cos + pltpu.roll(sl, HEAD_DIM // 2, 1) * sin
                if j == 1:
                    rot = rot * scale
                out_ref[:, j * GROUP + hh * HEAD_DIM:j * GROUP + (hh + 1) * HEAD_DIM] = rot.astype(BF16)
            continue
        if j in (3, 8):
            acc = _silu(acc)
        elif j == 4:
            acc = _silu(acc) * scale
        out_ref[:, j * GROUP:(j + 1) * GROUP] = acc.astype(BF16)


def _inproj(h, g, w_bf, cos, sin, tm):
    B, S, D = h.shape
    ncol = w_bf.shape[1]
    return pl.pallas_call(
        _inproj_kernel,
        grid=(B, S // tm),
        in_specs=[
            pl.BlockSpec((None, tm, D), lambda b, i: (b, i, 0)),
            pl.BlockSpec((1, D), lambda b, i: (0, 0)),
            pl.BlockSpec((D, ncol), lambda b, i: (0, 0), pipeline_mode=pl.Buffered(1)),
            pl.BlockSpec((tm, HEAD_DIM), lambda b, i: (i, 0)),
            pl.BlockSpec((tm, HEAD_DIM), lambda b, i: (i, 0)),
        ],
        out_specs=pl.BlockSpec((None, tm, ncol), lambda b, i: (b, i, 0)),
        out_shape=jax.ShapeDtypeStruct((B, S, ncol), BF16),
        compiler_params=_cparams(2),
        name="inproj",
    )(h, g.reshape(1, D), w_bf, cos, sin)


def _ret_kernel(q_ref, k_ref, v_ref, g_ref, dmat_ref, qdec_ref, kdec_ref, cdec_ref, rg_ref,
                out_ref, acc_ref, *, n_chunks):
    L = RET_CHUNK
    cdec = cdec_ref[0:1, :]
    zero_state = jnp.zeros((HEAD_DIM, HEAD_DIM), F32)

    def rows(i):
        return pl.ds(pl.multiple_of(i * L, L), L)

    def finish(sl, o):
        y = _rms(o) * rg_ref[...] * g_ref[sl, :].astype(F32)
        out_ref[sl, :] = y.astype(BF16)

    def step(i, states, second):
        state_f, state_b = states
        sfs = [rows(i * RET_PAIR + j) for j in range(RET_PAIR)]
        sbs = [rows(n_chunks - 1 - i * RET_PAIR - j) for j in range(RET_PAIR)]
        qf, kf, vf = ([r[s, :] for s in sfs] for r in (q_ref, k_ref, v_ref))
        qb, kb, vb = ([r[s, :] for s in sbs] for r in (q_ref, k_ref, v_ref))
        raw = [_dot_nt(q, k) for q, k in zip(qf, kf)]
        qdf = [(q.astype(F32) * qdec_ref[...]).astype(BF16) for q in qf]
        kdf = [(k.astype(F32) * kdec_ref[...]).astype(BF16) for k in kf]
        qdb = [(q.astype(F32) * kdec_ref[...]).astype(BF16) for q in qb]
        kdb = [(k.astype(F32) * qdec_ref[...]).astype(BF16) for k in kb]
        upd_f = [_dot_tn(k, v) for k, v in zip(kdf, vf)]
        upd_b = [_dot_tn(k, v) for k, v in zip(kdb, vb)]
        intra = [_dot((s * dmat_ref[...]).astype(BF16), v) for s, v in zip(raw, vf)]
        for j in range(RET_PAIR):
            of = intra[j] + _dot(qdf[j], state_f.astype(BF16))
            ob = _dot(qdb[j], state_b.astype(BF16))
            state_f = cdec * state_f + upd_f[j]
            state_b = cdec * state_b + upd_b[j]
            if second:
                finish(sfs[j], acc_ref[sfs[j], :] + of)
                finish(sbs[j], acc_ref[sbs[j], :] + ob)
            else:
                acc_ref[sfs[j], :] = of
                acc_ref[sbs[j], :] = ob
        return state_f, state_b

    steps = n_chunks // RET_PAIR
    states = lax.fori_loop(0, steps // 2, functools.partial(step, second=False),
                           (zero_state, zero_state))
    lax.fori_loop(steps // 2, steps, functools.partial(step, second=True), states)


def _ret_tables():
    L = RET_CHUNK
    t = np.arange(L, dtype=np.float64)
    lg = np.log1p(-(2.0 ** (-5.0 - np.arange(N_HEADS, dtype=np.float64))))
    dmat = np.exp(lg[:, None, None] * np.abs(t[:, None] - t[None, :])[None])
    qdec = np.exp(lg[:, None] * (t + 1.0))[:, :, None] * np.ones((1, 1, HEAD_DIM))
    kdec = np.exp(lg[:, None] * (L - 1.0 - t))[:, :, None] * np.ones((1, 1, HEAD_DIM))
    cdec = np.exp(lg * L)[:, None, None] * np.ones((1, 8, HEAD_DIM))
    return tuple(jnp.asarray(a, F32) for a in (dmat, qdec, kdec, cdec))


def _retention(proj, ret_g):
    B, S, _ = proj.shape
    L = RET_CHUNK
    assert S % (2 * RET_PAIR * L) == 0, "retention walks chunks from both ends, RET_PAIR at a time"
    dmat, qdec, kdec, cdec = _ret_tables()

    def col(c0):
        return pl.BlockSpec((None, S, HEAD_DIM), lambda b, h: (b, 0, c0 + h))

    def tab(r):
        return pl.BlockSpec((None, r, HEAD_DIM), lambda b, h: (h, 0, 0))

    return pl.pallas_call(
        functools.partial(_ret_kernel, n_chunks=S // L),
        grid=(B, N_HEADS),
        in_specs=[col(0), col(N_HEADS), col(2 * N_HEADS), col(3 * N_HEADS),
                  pl.BlockSpec((None, L, L), lambda b, h: (h, 0, 0)),
                  tab(L), tab(L), tab(8),
                  pl.BlockSpec((1, HEAD_DIM), lambda b, h: (0, h))],
        out_specs=pl.BlockSpec((None, S, HEAD_DIM), lambda b, h: (b, 0, h)),
        out_shape=jax.ShapeDtypeStruct((B, S, GROUP), BF16),
        scratch_shapes=[pltpu.VMEM((S, HEAD_DIM), F32)],
        compiler_params=_cparams(2),
        name="retention",
    )(proj, proj, proj, proj, dmat, qdec, kdec, cdec, ret_g.reshape(1, GROUP))


def _hgrn_chunks(chains, lb, first_layer):
    L = HGRN_CHUNK
    n = len(chains)
    qs = [c[0] for c in chains]
    vs = [c[2] for c in chains]
    states = [c[3] for c in chains]
    lvls = [c[5] for c in chains]
    backs = [c[6] for c in chains]

    kks, decs = [], []
    for (_, z, _, _, mtab, _, _) in chains:
        zf = z.astype(F32)
        e = jnp.exp(-jnp.abs(zf))
        inv = 1.0 / (1.0 + e)
        pos = zf >= 0.0
        sigm = jnp.where(pos, e * inv, inv)
        if first_layer:
            logf2 = jnp.minimum(zf, 0.0) * LOG2E - jnp.log2(1.0 + e)
            kk = sigm
        else:
            sig = jnp.where(pos, inv, e * inv)
            logf2 = jnp.log2(lb + (1.0 - lb) * sig)
            kk = (1.0 - lb) * sigm
        hi = logf2.astype(BF16)
        lo = (logf2 - hi.astype(F32)).astype(BF16)
        kks.append(kk)
        decs.append(_dot(mtab, jnp.concatenate([hi, lo], axis=0)))

    qfs = [q.astype(F32) for q in qs]
    cs = [d[0:L, :] for d in decs]
    scores = [jnp.where(lvls[i] == -1, _dot_nt(qs[i], kks[i].astype(BF16)), 0.0) for i in range(n)]
    m, level = 1, 0
    while m < L:
        for i in range(n):
            if m < 8:
                p = jnp.exp2(decs[i][(1 + level) * L:(2 + level) * L, :])
                sub = lax.broadcasted_iota(jnp.int32, (1, 8, HEAD_DIM), 1)
                is_q = ((sub >> level) & 1) == (0 if backs[i] else 1)
                w = jnp.where(is_q, qfs[i].reshape(L // 8, 8, HEAD_DIM),
                              kks[i].reshape(L // 8, 8, HEAD_DIM)).reshape(L, HEAD_DIM)
                r = (w * p).astype(BF16)
            else:
                slabs = []
                for j in range(L // m):
                    mid = (j // 2) * 2 * m + m
                    anchor = cs[i][mid:mid + 1, :] if backs[i] else cs[i][mid - 1:mid, :]
                    rows = cs[i][j * m:(j + 1) * m, :]
                    if (j % 2 == 1) != backs[i]:
                        slabs.append(qfs[i][j * m:(j + 1) * m, :] * jnp.exp2(rows - anchor))
                    else:
                        slabs.append(kks[i][j * m:(j + 1) * m, :] * jnp.exp2(anchor - rows))
                r = jnp.concatenate(slabs, axis=0).astype(BF16)
            scores[i] = jnp.where(lvls[i] == level, _dot_nt(r, r), scores[i])
        m, level = 2 * m, level + 1

    results = []
    latest = {}
    for i in range(n):
        state = states[i] if states[i] is not None else latest[backs[i]]
        edge = 0 if backs[i] else L - 1
        c_end = cs[i][edge:edge + 1, :]
        qe = (qfs[i] * jnp.exp2(cs[i])).astype(BF16)
        out = _dot(scores[i].astype(BF16), vs[i]) + _dot_nt(qe, state.astype(BF16))
        ke = (kks[i] * jnp.exp2(c_end - cs[i])).astype(BF16)
        latest[backs[i]] = jnp.exp2(c_end) * state + _dot_tn(vs[i], ke)
        results.append((out, latest[backs[i]]))
    return results


def _hgrn_kernel(q_ref, zf_ref, zb_ref, v_ref, lb_ref, mtabf_ref, mtabb_ref, lvlf_ref, lvlb_ref,
                 out_ref, *, n_chunks, first_layer):
    L = HGRN_CHUNK
    lb = lb_ref[...]
    zero_state = jnp.zeros((HEAD_DIM, HEAD_DIM), F32)

    def rows(i):
        return pl.ds(pl.multiple_of(i * L, L), L)

    def step(i, states, accumulate):
        chains, slices = [], []
        for k in range(HGRN_PAIR):
            sf = rows(i * HGRN_PAIR + k)
            sb = rows(n_chunks - 1 - i * HGRN_PAIR - k)
            chains.append((q_ref[sf, :], zf_ref[sf, :], v_ref[sf, :], states[0] if k == 0 else None,
                           mtabf_ref[...], lvlf_ref[...], False))
            chains.append((q_ref[sb, :], zb_ref[sb, :], v_ref[sb, :], states[1] if k == 0 else None,
                           mtabb_ref[...], lvlb_ref[...], True))
            slices += [sf, sb]
        results = _hgrn_chunks(chains, lb, first_layer)
        for sl, (o, _) in zip(slices, results):
            if accumulate:
                out_ref[sl, :] += o
            else:
                out_ref[sl, :] = o
        return results[-2][1], results[-1][1]

    steps = n_chunks // HGRN_PAIR
    states = lax.fori_loop(0, steps // 2, functools.partial(step, accumulate=False),
                           (zero_state, zero_state))
    lax.fori_loop(steps // 2, steps, functools.partial(step, accumulate=True), states)


def _hgrn_tables():
    L = HGRN_CHUNK
    t = np.arange(L)[:, None]
    u = np.arange(L)[None, :]
    x = t ^ u
    lev = np.floor(np.log2(np.maximum(x, 1))).astype(np.int32)
    lvl_f = np.where(t > u, lev, np.where(x == 0, -1, -2)).astype(np.int32)
    lvl_b = np.where(t < u, lev, np.where(x == 0, -1, -2)).astype(np.int32)

    def exponent_rows(backward):
        blocks = [(u >= t) if backward else (u <= t)]
        m = 1
        while m < 8:
            mid = (t // (2 * m)) * (2 * m) + m
            if backward:
                blocks.append(np.where(t < mid, (u >= t) & (u < mid), (u >= mid) & (u < t)))
            else:
                blocks.append(np.where(t >= mid, (u >= mid) & (u <= t), (u > t) & (u < mid)))
            m *= 2
        tab = np.concatenate(blocks, axis=0).astype(np.float32)
        return np.concatenate([tab, tab], axis=1)

    return (jnp.asarray(exponent_rows(False), BF16), jnp.asarray(exponent_rows(True), BF16),
            jnp.asarray(lvl_f, jnp.int32), jnp.asarray(lvl_b, jnp.int32))


def _hgrn(proj, lb, first_layer):
    B, S, _ = proj.shape
    L = HGRN_CHUNK
    assert S % (2 * HGRN_PAIR * L) == 0, "HGRN2 walks chunks from both ends, HGRN_PAIR at a time"
    mtab_f, mtab_b, lvl_f, lvl_b = _hgrn_tables()

    def col(g):
        return pl.BlockSpec((None, S, HEAD_DIM), lambda b, h: (b, 0, g * N_HEADS + h))

    def full(a):
        return pl.BlockSpec(a.shape, lambda b, h: (0, 0))

    return pl.pallas_call(
        functools.partial(_hgrn_kernel, n_chunks=S // L, first_layer=first_layer),
        grid=(B, N_HEADS),
        in_specs=[col(4), col(5), col(6), col(7),
                  pl.BlockSpec((1, HEAD_DIM), lambda b, h: (0, h)),
                  full(mtab_f), full(mtab_b), full(lvl_f), full(lvl_b)],
        out_specs=pl.BlockSpec((None, S, HEAD_DIM), lambda b, h: (b, 0, h)),
        out_shape=jax.ShapeDtypeStruct((B, S, GROUP), F32),
        compiler_params=_cparams(2),
        name="hgrn",
    )(proj, proj, proj, proj, lb.reshape(1, GROUP), mtab_f, mtab_b, lvl_f, lvl_b)


def _outproj_kernel(r_ref, hraw_ref, hg_ref, hgn_ref, w_ref, h_ref, g2_ref, wr_ref,
                    h1_ref, xn_ref, aff_ref):
    hn = _rms(hraw_ref[...]) * hgn_ref[...] * hg_ref[...].astype(F32)
    mix = _dot(r_ref[...], w_ref[0:GROUP, :]) + _dot(hn.astype(BF16), w_ref[GROUP:2 * GROUP, :])
    h1 = h_ref[...] + mix
    h1_ref[...] = h1
    xn = (_rms(h1) * g2_ref[...]).astype(BF16)
    xn_ref[...] = xn
    logits = _dot_nt(wr_ref[...], xn)
    mx = jnp.max(logits, axis=0, keepdims=True)
    ex = jnp.exp(logits - mx)
    aff_ref[...] = ex / jnp.sum(ex, axis=0, keepdims=True)


def _outproj(r_out, h_raw, proj, hgrn_g, w_out_bf, h, g2, wr_t_bf, tm):
    B, S, D = h.shape
    E = wr_t_bf.shape[0]
    return pl.pallas_call(
        _outproj_kernel,
        grid=(B, S // tm),
        in_specs=[
            pl.BlockSpec((None, tm, GROUP), lambda b, i: (b, i, 0)),
            pl.BlockSpec((None, tm, GROUP), lambda b, i: (b, i, 0)),
            pl.BlockSpec((None, tm, GROUP), lambda b, i: (b, i, N_GROUPS - 1)),
            pl.BlockSpec((1, GROUP), lambda b, i: (0, 0)),
            pl.BlockSpec((2 * GROUP, D), lambda b, i: (0, 0)),
            pl.BlockSpec((None, tm, D), lambda b, i: (b, i, 0)),
            pl.BlockSpec((1, D), lambda b, i: (0, 0)),
            pl.BlockSpec((E, D), lambda b, i: (0, 0)),
        ],
        out_specs=[
            pl.BlockSpec((None, tm, D), lambda b, i: (b, i, 0)),
            pl.BlockSpec((None, tm, D), lambda b, i: (b, i, 0)),
            pl.BlockSpec((None, E, tm), lambda b, i: (b, 0, i)),
        ],
        out_shape=[jax.ShapeDtypeStruct((B, S, D), F32),
                   jax.ShapeDtypeStruct((B, S, D), BF16),
                   jax.ShapeDtypeStruct((B, E, S), F32)],
        compiler_params=_cparams(2),
        name="outproj",
    )(r_out, h_raw, proj, hgrn_g.reshape(1, GROUP), w_out_bf, h, g2.reshape(1, D), wr_t_bf)


def _select_kernel(aff_ref, pos_ref, gate_ref, *, cap):
    a = aff_ref[...]
    E, S = a.shape
    u = pltpu.bitcast(a, jnp.int32)
    capf = jnp.float32(cap)

    def count(mask):
        return jnp.sum(jnp.where(mask, 1.0, 0.0), axis=-1, keepdims=True)

    def value_bit(i, thr):
        cand = thr | (jnp.int32(1) << (30 - i))
        return jnp.where(count(u >= cand) >= capf, cand, thr)

    thr = lax.fori_loop(0, 31, value_bit, jnp.zeros((E, 1), jnp.int32))
    gt = u > thr
    eq = u == thr
    need = capf - count(gt)
    idx = lax.broadcasted_iota(jnp.int32, (E, S), 1)
    nbits = int(S).bit_length()

    def index_bit(i, cut):
        cand = cut | (jnp.int32(1) << (nbits - 1 - i))
        return jnp.where(count(eq & (idx < cand)) <= need, cand, cut)

    cut = lax.fori_loop(0, nbits, index_bit, jnp.zeros((E, 1), jnp.int32))
    sel = gt | (eq & (idx < cut))
    gate_ref[...] = jnp.where(sel, a, 0.0)

    li = lax.broadcasted_iota(jnp.int32, (128, 128), 0)
    lj = lax.broadcasted_iota(jnp.int32, (128, 128), 1)
    upper = jnp.where(li <= lj, 1.0, 0.0).astype(BF16)
    carry = jnp.zeros((E, 1), F32)
    for j in range(S // 128):
        sb = jnp.where(sel[:, j * 128:(j + 1) * 128], 1.0, 0.0)
        incl = _dot(sb.astype(BF16), upper)
        pos_ref[:, j * 128:(j + 1) * 128] = (incl - sb + carry).astype(jnp.int32)
        carry = carry + incl[:, 127:128]


def _select(aff, cap):
    B, E, S = aff.shape
    spec = pl.BlockSpec((None, E, S), lambda b: (b, 0, 0))
    return pl.pallas_call(
        functools.partial(_select_kernel, cap=cap),
        grid=(B,),
        in_specs=[spec],
        out_specs=[spec, spec],
        out_shape=[jax.ShapeDtypeStruct((B, E, S), jnp.int32),
                   jax.ShapeDtypeStruct((B, E, S), F32)],
        compiler_params=_cparams(1),
        name="select",
    )(aff)


def _moe_kernel(bs_ref, x_ref, pos_ref, gate_ref, wg_ref, wu_ref, wd_ref, ye_ref, xe_ref,
                *, n_tb, tk, wn, fm, cap, n_exp):
    b = pl.program_id(0)
    e = pl.program_id(1)
    base = (b * n_exp + e) * (n_tb + 1)
    lane_slot = lax.broadcasted_iota(jnp.int32, (wn, tk), 0)
    xe_ref[...] = jnp.zeros_like(xe_ref)

    def gather(tb, carry):
        lo = bs_ref[base + tb]
        hi = bs_ref[base + tb + 1]
        lo_al = (lo // 8) * 8
        n_win = jnp.where(hi > lo, (hi - lo_al + wn - 1) // wn, 0)
        key = jnp.where(gate_ref[pl.ds(tb, 1), :] > 0.0, pos_ref[pl.ds(tb, 1), :], -1)
        rows = pl.ds(pl.multiple_of(tb * tk, tk), tk)

        def window(w, c):
            first = lo_al + w * wn
            start = pl.multiple_of(jnp.minimum(first, cap - wn), 8)
            slot = start + lane_slot
            hit = jnp.logical_and(key == slot, slot >= first)
            onehot = jnp.where(hit, 1.0, 0.0).astype(BF16)
            xe_ref[pl.ds(start, wn), :] += _dot(onehot, x_ref[rows, :])
            return c

        lax.fori_loop(0, n_win, window, 0)
        return carry

    lax.fori_loop(0, n_tb, gather, 0)
    for r in range(cap // fm):
        xe = xe_ref[r * fm:(r + 1) * fm, :].astype(BF16)
        hid = (_silu(_dot(xe, wg_ref[...])) * _dot(xe, wu_ref[...])).astype(BF16)
        ye_ref[r * fm:(r + 1) * fm, :] = _dot(hid, wd_ref[...]).astype(BF16)


def _moe_ffn(bs, xn, pos4, gate4, wg, wu, wd, cap, tk, wn, fm):
    B, S, D = xn.shape
    E, _, FF = wg.shape
    n_tb = S // tk
    grid_spec = pltpu.PrefetchScalarGridSpec(
        num_scalar_prefetch=1,
        grid=(B, E),
        in_specs=[
            pl.BlockSpec((None, S, D), lambda b, e, bs: (b, 0, 0), pipeline_mode=pl.Buffered(1)),
            pl.BlockSpec((None, None, n_tb, tk), lambda b, e, bs: (b, e, 0, 0)),
            pl.BlockSpec((None, None, n_tb, tk), lambda b, e, bs: (b, e, 0, 0)),
            pl.BlockSpec((None, D, FF), lambda b, e, bs: (e, 0, 0)),
            pl.BlockSpec((None, D, FF), lambda b, e, bs: (e, 0, 0)),
            pl.BlockSpec((None, FF, D), lambda b, e, bs: (e, 0, 0)),
        ],
        out_specs=pl.BlockSpec((None, None, cap, D), lambda b, e, bs: (b, e, 0, 0)),
        scratch_shapes=[pltpu.VMEM((cap, D), F32)],
    )
    return pl.pallas_call(
        functools.partial(_moe_kernel, n_tb=n_tb, tk=tk, wn=wn, fm=fm, cap=cap, n_exp=E),
        grid_spec=grid_spec,
        out_shape=jax.ShapeDtypeStruct((B, E, cap, D), BF16),
        compiler_params=_cparams(2),
        name="moe_ffn",
    )(bs, xn, pos4, gate4, wg, wu, wd)


def _combine_kernel(bs_ref, h1_ref, ye_ref, pos_ref, gate_ref, fg_ref, out_ref, w_ref, y_ref,
                    *, n_tb, tk, wn, cap, n_exp, final_norm):
    b = pl.program_id(0)
    tb = pl.program_id(1)
    lane_slot = lax.broadcasted_iota(jnp.int32, (wn, tk), 0)

    def window_start(first):
        return pl.multiple_of(jnp.minimum(first, cap - wn), 16)

    def weights(e, start, first):
        slot = start + lane_slot
        hit = jnp.logical_and(pos_ref[e:e + 1, :] == slot, slot >= first)
        return jnp.where(hit, gate_ref[e:e + 1, :], 0.0).astype(BF16)

    firsts = []
    for e in range(n_exp):
        base = (b * n_exp + e) * (n_tb + 1)
        first = (bs_ref[base + tb] // 16) * 16
        start = window_start(first)
        firsts.append((first, bs_ref[base + tb + 1]))
        w_ref[e * wn:(e + 1) * wn, :] = weights(e, start, first)
        y_ref[e * wn:(e + 1) * wn, :] = ye_ref[e, pl.ds(start, wn), :]
    out_ref[...] = h1_ref[...] + _dot_tn(w_ref[...], y_ref[...])

    for e in range(n_exp):
        first0, end = firsts[e]
        n_more = jnp.maximum(end - first0 - 1, 0) // wn

        def window(w, carry):
            first = first0 + (w + 1) * wn
            start = window_start(first)
            out_ref[...] += _dot_tn(weights(e, start, first), ye_ref[e, pl.ds(start, wn), :])
            return carry

        lax.fori_loop(0, n_more, window, 0)
    if final_norm:
        out_ref[...] = _rms(out_ref[...]) * fg_ref[...]


def _combine(bs, h1, ye, pos_t, gate_t, final_g, cap, tk, wn, final_norm):
    B, S, D = h1.shape
    E = ye.shape[1]
    n_tb = S // tk
    grid_spec = pltpu.PrefetchScalarGridSpec(
        num_scalar_prefetch=1,
        grid=(B, n_tb),
        in_specs=[
            pl.BlockSpec((None, tk, D), lambda b, t, bs: (b, t, 0)),
            pl.BlockSpec((None, E, cap, D), lambda b, t, bs: (b, 0, 0, 0), pipeline_mode=pl.Buffered(1)),
            pl.BlockSpec((None, None, E, tk), lambda b, t, bs: (b, t, 0, 0)),
            pl.BlockSpec((None, None, E, tk), lambda b, t, bs: (b, t, 0, 0)),
            pl.BlockSpec((1, D), lambda b, t, bs: (0, 0)),
        ],
        out_specs=pl.BlockSpec((None, tk, D), lambda b, t, bs: (b, t, 0)),
        scratch_shapes=[pltpu.VMEM((E * wn, tk), BF16), pltpu.VMEM((E * wn, D), BF16)],
    )
    return pl.pallas_call(
        functools.partial(_combine_kernel, n_tb=n_tb, tk=tk, wn=wn, cap=cap, n_exp=E,
                          final_norm=final_norm),
        grid_spec=grid_spec,
        out_shape=jax.ShapeDtypeStruct((B, S, D), F32),
        compiler_params=_cparams(2),
        name="combine",
    )(bs, h1, ye, pos_t, gate_t, final_g.reshape(1, D))


def _rope_tables(S):
    half = HEAD_DIM // 2
    inv_freq = ROPE_BASE ** (-jnp.arange(half, dtype=F32) / half)
    ang = jnp.arange(S).astype(F32)[:, None] * inv_freq[None, :]
    cos, sin = jnp.cos(ang), jnp.sin(ang)
    return jnp.concatenate([cos, cos], axis=-1), jnp.concatenate([-sin, sin], axis=-1)


def _block_starts(pos, tk, cap):
    B, E, _ = pos.shape
    bs = jnp.concatenate([pos[:, :, ::tk], jnp.full((B, E, 1), cap, jnp.int32)], axis=-1)
    return bs.reshape(-1)


def kernel(x, norm1_g, w_in, ret_norm_g, hgrn_norm_g, w_out, lower_bounds, norm2_g, w_router,
           w_gate, w_up, w_down, final_norm_g):
    B, S, D = x.shape
    depth = w_in.shape[0]
    E = w_router.shape[-1]
    cap = CAPACITY_FACTOR * S // E
    tm = min(512, S)
    tk = min(512, S)
    wn = min(128, cap)
    fm = min(512, cap)
    n_tb = S // tk

    lbs = jax.nn.softmax(lower_bounds.astype(F32), axis=0)
    lbs = jnp.cumsum(lbs, axis=0) - lbs[0]
    cos, sin = _rope_tables(S)

    h = x
    for layer in range(depth):
        proj = _inproj(h, norm1_g[layer], w_in[layer].astype(BF16), cos, sin, tm)
        r_out = _retention(proj, ret_norm_g[layer])
        h_raw = _hgrn(proj, lbs[layer], first_layer=(layer == 0))
        h1, xn, aff = _outproj(r_out, h_raw, proj, hgrn_norm_g[layer], w_out[layer].astype(BF16),
                               h, norm2_g[layer], w_router[layer].T.astype(BF16), tm)
        pos, gate = _select(aff, cap)
        bs = _block_starts(pos, tk, cap)
        ye = _moe_ffn(bs, xn, pos.reshape(B, E, n_tb, tk), gate.reshape(B, E, n_tb, tk),
                      w_gate[layer].astype(BF16), w_up[layer].astype(BF16),
                      w_down[layer].astype(BF16), cap, tk, wn, fm)
        pos_t = pos.reshape(B, E, n_tb, tk).transpose(0, 2, 1, 3)
        gate_t = gate.reshape(B, E, n_tb, tk).transpose(0, 2, 1, 3)
        h = _combine(bs, h1, ye, pos_t, gate_t, final_norm_g, cap, tk, wn,
                     final_norm=(layer == depth - 1))
    return h
```

```python
import functools

import numpy as np
import jax
import jax.numpy as jnp
from jax import lax
from jax.experimental import pallas as pl
from jax.experimental.pallas import tpu as pltpu

F32 = jnp.float32
BF16 = jnp.bfloat16

HEAD_DIM = 128
N_HEADS = 4
GROUP = N_HEADS * HEAD_DIM
N_GROUPS = 9
ROPE_BASE = 10000.0
NORM_EPS = 1e-6
CAPACITY_FACTOR = 2

RET_CHUNK = 256
RET_PAIR = 4
HGRN_CHUNK = 128
HGRN_PAIR = 4
LOG2E = 1.4426950408889634
VMEM_LIMIT = 58 * 1024 * 1024


def _cparams(n_axes):
    return pltpu.CompilerParams(
        dimension_semantics=("arbitrary",) * n_axes, vmem_limit_bytes=VMEM_LIMIT)


def _dot(a, b):
    return jnp.dot(a, b, preferred_element_type=F32)


def _dot_nt(a, b):
    return lax.dot_general(a, b, (((1,), (1,)), ((), ())), preferred_element_type=F32)


def _dot_tn(a, b):
    return lax.dot_general(a, b, (((0,), (0,)), ((), ())), preferred_element_type=F32)


def _silu(x):
    return x * (1.0 / (1.0 + jnp.exp(-x)))


def _rms(x):
    return x * lax.rsqrt(jnp.mean(x * x, axis=-1, keepdims=True) + NORM_EPS)


def _inproj_kernel(h_ref, g_ref, w_ref, cos_ref, sin_ref, out_ref):
    xn = (_rms(h_ref[...]) * g_ref[...]).astype(BF16)
    cos = cos_ref[...]
    sin = sin_ref[...]
    scale = HEAD_DIM ** -0.5
    for j in range(N_GROUPS):
        acc = _dot(xn, w_ref[:, j * GROUP:(j + 1) * GROUP])
        if j in (0, 1):
            for hh in range(N_HEADS):
                sl = acc[:, hh * HEAD_DIM:(hh + 1) * HEAD_DIM]
                rot = sl * cos + pltpu.roll(sl, HEAD_DIM // 2, 1) * sin
                if j == 1:
                    rot = rot * scale
                out_ref[:, j * GROUP + hh * HEAD_DIM:j * GROUP + (hh + 1) * HEAD_DIM] = rot.astype(BF16)
            continue
        if j in (3, 8):
            acc = _silu(acc)
        elif j == 4:
            acc = _silu(acc) * scale
        out_ref[:, j * GROUP:(j + 1) * GROUP] = acc.astype(BF16)


def _inproj(h, g, w_bf, cos, sin, tm):
    B, S, D = h.shape
    ncol = w_bf.shape[1]
    return pl.pallas_call(
        _inproj_kernel,
        grid=(B, S // tm),
        in_specs=[
            pl.BlockSpec((None, tm, D), lambda b, i: (b, i, 0)),
            pl.BlockSpec((1, D), lambda b, i: (0, 0)),
            pl.BlockSpec((D, ncol), lambda b, i: (0, 0), pipeline_mode=pl.Buffered(1)),
            pl.BlockSpec((tm, HEAD_DIM), lambda b, i: (i, 0)),
            pl.BlockSpec((tm, HEAD_DIM), lambda b, i: (i, 0)),
        ],
        out_specs=pl.BlockSpec((None, tm, ncol), lambda b, i: (b, i, 0)),
        out_shape=jax.ShapeDtypeStruct((B, S, ncol), BF16),
        compiler_params=_cparams(2),
        name="inproj",
    )(h, g.reshape(1, D), w_bf, cos, sin)


def _ret_kernel(q_ref, k_ref, v_ref, g_ref, dmat_ref, qdec_ref, kdec_ref, cdec_ref, rg_ref,
                out_ref, acc_ref, *, n_chunks):
    L = RET_CHUNK
    cdec = cdec_ref[0:1, :]
    zero_state = jnp.zeros((HEAD_DIM, HEAD_DIM), F32)

    def rows(i):
        return pl.ds(pl.multiple_of(i * L, L), L)

    def finish(sl, o):
        y = _rms(o) * rg_ref[...] * g_ref[sl, :].astype(F32)
        out_ref[sl, :] = y.astype(BF16)

    def step(i, states, second):
        state_f, state_b = states
        sfs = [rows(i * RET_PAIR + j) for j in range(RET_PAIR)]
        sbs = [rows(n_chunks - 1 - i * RET_PAIR - j) for j in range(RET_PAIR)]
        qf, kf, vf = ([r[s, :] for s in sfs] for r in (q_ref, k_ref, v_ref))
        qb, kb, vb = ([r[s, :] for s in sbs] for r in (q_ref, k_ref, v_ref))
        raw = [_dot_nt(q, k) for q, k in zip(qf, kf)]
        qdf = [(q.astype(F32) * qdec_ref[...]).astype(BF16) for q in qf]
        kdf = [(k.astype(F32) * kdec_ref[...]).astype(BF16) for k in kf]
        qdb = [(q.astype(F32) * kdec_ref[...]).astype(BF16) for q in qb]
        kdb = [(k.astype(F32) * qdec_ref[...]).astype(BF16) for k in kb]
        upd_f = [_dot_tn(k, v) for k, v in zip(kdf, vf)]
        upd_b = [_dot_tn(k, v) for k, v in zip(kdb, vb)]
        intra = [_dot((s * dmat_ref[...]).astype(BF16), v) for s, v in zip(raw, vf)]
        for j in range(RET_PAIR):
            of = intra[j] + _dot(qdf[j], state_f.astype(BF16))
            ob = _dot(qdb[j], state_b.astype(BF16))
            state_f = cdec * state_f + upd_f[j]
            state_b = cdec * state_b + upd_b[j]
            if second:
                finish(sfs[j], acc_ref[sfs[j], :] + of)
                finish(sbs[j], acc_ref[sbs[j], :] + ob)
            else:
                acc_ref[sfs[j], :] = of
                acc_ref[sbs[j], :] = ob
        return state_f, state_b

    steps = n_chunks // RET_PAIR
    states = lax.fori_loop(0, steps // 2, functools.partial(step, second=False),
                           (zero_state, zero_state))
    lax.fori_loop(steps // 2, steps, functools.partial(step, second=True), states)


def _ret_tables():
    L = RET_CHUNK
    t = np.arange(L, dtype=np.float64)
    lg = np.log1p(-(2.0 ** (-5.0 - np.arange(N_HEADS, dtype=np.float64))))
    dmat = np.exp(lg[:, None, None] * np.abs(t[:, None] - t[None, :])[None])
    qdec = np.exp(lg[:, None] * (t + 1.0))[:, :, None] * np.ones((1, 1, HEAD_DIM))
    kdec = np.exp(lg[:, None] * (L - 1.0 - t))[:, :, None] * np.ones((1, 1, HEAD_DIM))
    cdec = np.exp(lg * L)[:, None, None] * np.ones((1, 8, HEAD_DIM))
    return tuple(jnp.asarray(a, F32) for a in (dmat, qdec, kdec, cdec))


def _retention(proj, ret_g):
    B, S, _ = proj.shape
    L = RET_CHUNK
    assert S % (2 * RET_PAIR * L) == 0, "retention walks chunks from both ends, RET_PAIR at a time"
    dmat, qdec, kdec, cdec = _ret_tables()

    def col(c0):
        return pl.BlockSpec((None, S, HEAD_DIM), lambda b, h: (b, 0, c0 + h))

    def tab(r):
        return pl.BlockSpec((None, r, HEAD_DIM), lambda b, h: (h, 0, 0))

    return pl.pallas_call(
        functools.partial(_ret_kernel, n_chunks=S // L),
        grid=(B, N_HEADS),
        in_specs=[col(0), col(N_HEADS), col(2 * N_HEADS), col(3 * N_HEADS),
                  pl.BlockSpec((None, L, L), lambda b, h: (h, 0, 0)),
                  tab(L), tab(L), tab(8),
                  pl.BlockSpec((1, HEAD_DIM), lambda b, h: (0, h))],
        out_specs=pl.BlockSpec((None, S, HEAD_DIM), lambda b, h: (b, 0, h)),
        out_shape=jax.ShapeDtypeStruct((B, S, GROUP), BF16),
        scratch_shapes=[pltpu.VMEM((S, HEAD_DIM), F32)],
        compiler_params=_cparams(2),
        name="retention",
    )(proj, proj, proj, proj, dmat, qdec, kdec, cdec, ret_g.reshape(1, GROUP))


def _hgrn_chunks(chains, lb, first_layer):
    L = HGRN_CHUNK
    n = len(chains)
    qs = [c[0] for c in chains]
    vs = [c[2] for c in chains]
    states = [c[3] for c in chains]
    lvls = [c[5] for c in chains]
    backs = [c[6] for c in chains]

    kks, decs = [], []
    for (_, z, _, _, mtab, _, _) in chains:
        zf = z.astype(F32)
        e = jnp.exp(-jnp.abs(zf))
        inv = 1.0 / (1.0 + e)
        pos = zf >= 0.0
        sigm = jnp.where(pos, e * inv, inv)
        if first_layer:
            logf2 = jnp.minimum(zf, 0.0) * LOG2E - jnp.log2(1.0 + e)
            kk = sigm
        else:
            sig = jnp.where(pos, inv, e * inv)
            logf2 = jnp.log2(lb + (1.0 - lb) * sig)
            kk = (1.0 - lb) * sigm
        hi = logf2.astype(BF16)
        lo = (logf2 - hi.astype(F32)).astype(BF16)
        kks.append(kk)
        decs.append(_dot(mtab, jnp.concatenate([hi, lo], axis=0)))

    qfs = [q.astype(F32) for q in qs]
    cs = [d[0:L, :] for d in decs]
    scores = [jnp.where(lvls[i] == -1, _dot_nt(qs[i], kks[i].astype(BF16)), 0.0) for i in range(n)]
    m, level = 1, 0
    while m < L:
        for i in range(n):
            if m < 8:
                p = jnp.exp2(decs[i][(1 + level) * L:(2 + level) * L, :])
                sub = lax.broadcasted_iota(jnp.int32, (1, 8, HEAD_DIM), 1)
                is_q = ((sub >> level) & 1) == (0 if backs[i] else 1)
                w = jnp.where(is_q, qfs[i].reshape(L // 8, 8, HEAD_DIM),
                              kks[i].reshape(L // 8, 8, HEAD_DIM)).reshape(L, HEAD_DIM)
                r = (w * p).astype(BF16)
            else:
                slabs = []
                for j in range(L // m):
                    mid = (j // 2) * 2 * m + m
                    anchor = cs[i][mid:mid + 1, :] if backs[i] else cs[i][mid - 1:mid, :]
                    rows = cs[i][j * m:(j + 1) * m, :]
                    if (j % 2 == 1) != backs[i]:
                        slabs.append(qfs[i][j * m:(j + 1) * m, :] * jnp.exp2(rows - anchor))
                    else:
                        slabs.append(kks[i][j * m:(j + 1) * m, :] * jnp.exp2(anchor - rows))
                r = jnp.concatenate(slabs, axis=0).astype(BF16)
            scores[i] = jnp.where(lvls[i] == level, _dot_nt(r, r), scores[i])
        m, level = 2 * m, level + 1

    results = []
    latest = {}
    for i in range(n):
        state = states[i] if states[i] is not None else latest[backs[i]]
        edge = 0 if backs[i] else L - 1
        c_end = cs[i][edge:edge + 1, :]
        qe = (qfs[i] * jnp.exp2(cs[i])).astype(BF16)
        out = _dot(scores[i].astype(BF16), vs[i]) + _dot_nt(qe, state.astype(BF16))
        ke = (kks[i] * jnp.exp2(c_end - cs[i])).astype(BF16)
        latest[backs[i]] = jnp.exp2(c_end) * state + _dot_tn(vs[i], ke)
        results.append((out, latest[backs[i]]))
    return results


def _hgrn_kernel(q_ref, zf_ref, zb_ref, v_ref, lb_ref, mtabf_ref, mtabb_ref, lvlf_ref, lvlb_ref,
                 out_ref, *, n_chunks, first_layer):
    L = HGRN_CHUNK
    lb = lb_ref[...]
    zero_state = jnp.zeros((HEAD_DIM, HEAD_DIM), F32)

    def rows(i):
        return pl.ds(pl.multiple_of(i * L, L), L)

    def step(i, states, accumulate):
        chains, slices = [], []
        for k in range(HGRN_PAIR):
            sf = rows(i * HGRN_PAIR + k)
            sb = rows(n_chunks - 1 - i * HGRN_PAIR - k)
            chains.append((q_ref[sf, :], zf_ref[sf, :], v_ref[sf, :], states[0] if k == 0 else None,
                           mtabf_ref[...], lvlf_ref[...], False))
            chains.append((q_ref[sb, :], zb_ref[sb, :], v_ref[sb, :], states[1] if k == 0 else None,
                           mtabb_ref[...], lvlb_ref[...], True))
            slices += [sf, sb]
        results = _hgrn_chunks(chains, lb, first_layer)
        for sl, (o, _) in zip(slices, results):
            if accumulate:
                out_ref[sl, :] += o
            else:
                out_ref[sl, :] = o
        return results[-2][1], results[-1][1]

    steps = n_chunks // HGRN_PAIR
    states = lax.fori_loop(0, steps // 2, functools.partial(step, accumulate=False),
                           (zero_state, zero_state))
    lax.fori_loop(steps // 2, steps, functools.partial(step, accumulate=True), states)


def _hgrn_tables():
    L = HGRN_CHUNK
    t = np.arange(L)[:, None]
    u = np.arange(L)[None, :]
    x = t ^ u
    lev = np.floor(np.log2(np.maximum(x, 1))).astype(np.int32)
    lvl_f = np.where(t > u, lev, np.where(x == 0, -1, -2)).astype(np.int32)
    lvl_b = np.where(t < u, lev, np.where(x == 0, -1, -2)).astype(np.int32)

    def exponent_rows(backward):
        blocks = [(u >= t) if backward else (u <= t)]
        m = 1
        while m < 8:
            mid = (t // (2 * m)) * (2 * m) + m
            if backward:
                blocks.append(np.where(t < mid, (u >= t) & (u < mid), (u >= mid) & (u < t)))
            else:
                blocks.append(np.where(t >= mid, (u >= mid) & (u <= t), (u > t) & (u < mid)))
            m *= 2
        tab = np.concatenate(blocks, axis=0).astype(np.float32)
        return np.concatenate([tab, tab], axis=1)

    return (jnp.asarray(exponent_rows(False), BF16), jnp.asarray(exponent_rows(True), BF16),
            jnp.asarray(lvl_f, jnp.int32), jnp.asarray(lvl_b, jnp.int32))


def _hgrn(proj, lb, first_layer):
    B, S, _ = proj.shape
    L = HGRN_CHUNK
    assert S % (2 * HGRN_PAIR * L) == 0, "HGRN2 walks chunks from both ends, HGRN_PAIR at a time"
    mtab_f, mtab_b, lvl_f, lvl_b = _hgrn_tables()

    def col(g):
        return pl.BlockSpec((None, S, HEAD_DIM), lambda b, h: (b, 0, g * N_HEADS + h))

    def full(a):
        return pl.BlockSpec(a.shape, lambda b, h: (0, 0))

    return pl.pallas_call(
        functools.partial(_hgrn_kernel, n_chunks=S // L, first_layer=first_layer),
        grid=(B, N_HEADS),
        in_specs=[col(4), col(5), col(6), col(7),
                  pl.BlockSpec((1, HEAD_DIM), lambda b, h: (0, h)),
                  full(mtab_f), full(mtab_b), full(lvl_f), full(lvl_b)],
        out_specs=pl.BlockSpec((None, S, HEAD_DIM), lambda b, h: (b, 0, h)),
        out_shape=jax.ShapeDtypeStruct((B, S, GROUP), F32),
        compiler_params=_cparams(2),
        name="hgrn",
    )(proj, proj, proj, proj, lb.reshape(1, GROUP), mtab_f, mtab_b, lvl_f, lvl_b)


def _outproj_kernel(r_ref, hraw_ref, hg_ref, hgn_ref, w_ref, h_ref, g2_ref, wr_ref,
                    h1_ref, xn_ref, aff_ref):
    hn = _rms(hraw_ref[...]) * hgn_ref[...] * hg_ref[...].astype(F32)
    mix = _dot(r_ref[...], w_ref[0:GROUP, :]) + _dot(hn.astype(BF16), w_ref[GROUP:2 * GROUP, :])
    h1 = h_ref[...] + mix
    h1_ref[...] = h1
    xn = (_rms(h1) * g2_ref[...]).astype(BF16)
    xn_ref[...] = xn
    logits = _dot_nt(wr_ref[...], xn)
    mx = jnp.max(logits, axis=0, keepdims=True)
    ex = jnp.exp(logits - mx)
    aff_ref[...] = ex / jnp.sum(ex, axis=0, keepdims=True)


def _outproj(r_out, h_raw, proj, hgrn_g, w_out_bf, h, g2, wr_t_bf, tm):
    B, S, D = h.shape
    E = wr_t_bf.shape[0]
    return pl.pallas_call(
        _outproj_kernel,
        grid=(B, S // tm),
        in_specs=[
            pl.BlockSpec((None, tm, GROUP), lambda b, i: (b, i, 0)),
            pl.BlockSpec((None, tm, GROUP), lambda b, i: (b, i, 0)),
            pl.BlockSpec((None, tm, GROUP), lambda b, i: (b, i, N_GROUPS - 1)),
            pl.BlockSpec((1, GROUP), lambda b, i: (0, 0)),
            pl.BlockSpec((2 * GROUP, D), lambda b, i: (0, 0)),
            pl.BlockSpec((None, tm, D), lambda b, i: (b, i, 0)),
            pl.BlockSpec((1, D), lambda b, i: (0, 0)),
            pl.BlockSpec((E, D), lambda b, i: (0, 0)),
        ],
        out_specs=[
            pl.BlockSpec((None, tm, D), lambda b, i: (b, i, 0)),
            pl.BlockSpec((None, tm, D), lambda b, i: (b, i, 0)),
            pl.BlockSpec((None, E, tm), lambda b, i: (b, 0, i)),
        ],
        out_shape=[jax.ShapeDtypeStruct((B, S, D), F32),
                   jax.ShapeDtypeStruct((B, S, D), BF16),
                   jax.ShapeDtypeStruct((B, E, S), F32)],
        compiler_params=_cparams(2),
        name="outproj",
    )(r_out, h_raw, proj, hgrn_g.reshape(1, GROUP), w_out_bf, h, g2.reshape(1, D), wr_t_bf)


def _select_kernel(aff_ref, pos_ref, gate_ref, *, cap):
    a = aff_ref[...]
    E, S = a.shape
    u = pltpu.bitcast(a, jnp.int32)
    capf = jnp.float32(cap)

    def count(mask):
        return jnp.sum(jnp.where(mask, 1.0, 0.0), axis=-1, keepdims=True)

    def value_bit(i, thr):
        cand = thr | (jnp.int32(1) << (30 - i))
        return jnp.where(count(u >= cand) >= capf, cand, thr)

    thr = lax.fori_loop(0, 31, value_bit, jnp.zeros((E, 1), jnp.int32))
    gt = u > thr
    eq = u == thr
    need = capf - count(gt)
    idx = lax.broadcasted_iota(jnp.int32, (E, S), 1)
    nbits = int(S).bit_length()

    def index_bit(i, cut):
        cand = cut | (jnp.int32(1) << (nbits - 1 - i))
        return jnp.where(count(eq & (idx < cand)) <= need, cand, cut)

    cut = lax.fori_loop(0, nbits, index_bit, jnp.zeros((E, 1), jnp.int32))
    sel = gt | (eq & (idx < cut))
    gate_ref[...] = jnp.where(sel, a, 0.0)

    li = lax.broadcasted_iota(jnp.int32, (128, 128), 0)
    lj = lax.broadcasted_iota(jnp.int32, (128, 128), 1)
    upper = jnp.where(li <= lj, 1.0, 0.0).astype(BF16)
    carry = jnp.zeros((E, 1), F32)
    for j in range(S // 128):
        sb = jnp.where(sel[:, j * 128:(j + 1) * 128], 1.0, 0.0)
        incl = _dot(sb.astype(BF16), upper)
        pos_ref[:, j * 128:(j + 1) * 128] = (incl - sb + carry).astype(jnp.int32)
        carry = carry + incl[:, 127:128]


def _select(aff, cap):
    B, E, S = aff.shape
    spec = pl.BlockSpec((None, E, S), lambda b: (b, 0, 0))
    return pl.pallas_call(
        functools.partial(_select_kernel, cap=cap),
        grid=(B,),
        in_specs=[spec],
        out_specs=[spec, spec],
        out_shape=[jax.ShapeDtypeStruct((B, E, S), jnp.int32),
                   jax.ShapeDtypeStruct((B, E, S), F32)],
        compiler_params=_cparams(1),
        name="select",
    )(aff)


def _gather_kernel(bs_ref, x_ref, pos_ref, gate_ref, xe_ref, oh_ref,
                   *, n_tb, tk, wn, cap, n_exp, eg):
    b = pl.program_id(0)
    grp = pl.program_id(1)
    tb = pl.program_id(2)
    lane_slot = lax.broadcasted_iota(jnp.int32, (wn, tk), 0)

    @pl.when(tb == 0)
    def _():
        xe_ref[...] = jnp.zeros_like(xe_ref)

    def onehot(el, start, first):
        key = jnp.where(gate_ref[el:el + 1, :] > 0.0, pos_ref[el:el + 1, :], -1)
        slot = start + lane_slot
        hit = jnp.logical_and(key == slot, slot >= first)
        return jnp.where(hit, 1.0, 0.0).astype(BF16)

    def window_start(first):
        return pl.multiple_of(jnp.minimum(first, cap - wn), 16)

    spans = []
    for el in range(eg):
        base = (b * n_exp + grp * eg + el) * (n_tb + 1)
        first = (bs_ref[base + tb] // 16) * 16
        start = window_start(first)
        spans.append((first, start, bs_ref[base + tb + 1]))
        oh_ref[el * wn:(el + 1) * wn, :] = onehot(el, start, first)
    rows = _dot(oh_ref[...], x_ref[...]).astype(BF16)
    for el in range(eg):
        xe_ref[el, pl.ds(spans[el][1], wn), :] += rows[el * wn:(el + 1) * wn, :]

    for el in range(eg):
        first0, _, end = spans[el]
        n_more = jnp.maximum(end - first0 - 1, 0) // wn

        def window(w, carry):
            first = first0 + (w + 1) * wn
            start = window_start(first)
            xe_ref[el, pl.ds(start, wn), :] += _dot(onehot(el, start, first), x_ref[...]).astype(BF16)
            return carry

        lax.fori_loop(0, n_more, window, 0)


def _moe_gather(bs, xn, pos_t, gate_t, cap, tk, wn, eg):
    B, S, D = xn.shape
    E = pos_t.shape[2]
    n_tb = S // tk
    grid_spec = pltpu.PrefetchScalarGridSpec(
        num_scalar_prefetch=1,
        grid=(B, E // eg, n_tb),
        in_specs=[
            pl.BlockSpec((None, tk, D), lambda b, g, t, bs: (b, t, 0)),
            pl.BlockSpec((None, None, eg, tk), lambda b, g, t, bs: (b, t, g, 0)),
            pl.BlockSpec((None, None, eg, tk), lambda b, g, t, bs: (b, t, g, 0)),
        ],
        out_specs=pl.BlockSpec((None, eg, cap, D), lambda b, g, t, bs: (b, g, 0, 0)),
        scratch_shapes=[pltpu.VMEM((eg * wn, tk), BF16)],
    )
    return pl.pallas_call(
        functools.partial(_gather_kernel, n_tb=n_tb, tk=tk, wn=wn, cap=cap, n_exp=E, eg=eg),
        grid_spec=grid_spec,
        out_shape=jax.ShapeDtypeStruct((B, E, cap, D), BF16),
        compiler_params=_cparams(3),
        name="moe_gather",
    )(bs, xn, pos_t, gate_t)


def _ffn_kernel(xe_ref, wg_ref, wu_ref, wd_ref, ye_ref, *, fm):
    for r in range(xe_ref.shape[0] // fm):
        xe = xe_ref[r * fm:(r + 1) * fm, :]
        hid = (_silu(_dot(xe, wg_ref[...])) * _dot(xe, wu_ref[...])).astype(BF16)
        ye_ref[r * fm:(r + 1) * fm, :] = _dot(hid, wd_ref[...]).astype(BF16)


def _moe_ffn(xe, wg, wu, wd, fm):
    B, E, cap, D = xe.shape
    FF = wg.shape[-1]
    rows = pl.BlockSpec((None, None, cap, D), lambda e, b: (b, e, 0, 0))
    return pl.pallas_call(
        functools.partial(_ffn_kernel, fm=fm),
        grid=(E, B),
        in_specs=[rows,
                  pl.BlockSpec((None, D, FF), lambda e, b: (e, 0, 0)),
                  pl.BlockSpec((None, D, FF), lambda e, b: (e, 0, 0)),
                  pl.BlockSpec((None, FF, D), lambda e, b: (e, 0, 0))],
        out_specs=rows,
        out_shape=jax.ShapeDtypeStruct((B, E, cap, D), BF16),
        compiler_params=_cparams(2),
        name="moe_ffn",
    )(xe, wg, wu, wd)


def _combine_kernel(bs_ref, h1_ref, ye_ref, pos_ref, gate_ref, fg_ref, out_ref, w_ref, y_ref,
                    *, n_tb, tk, wn, cap, n_exp, final_norm):
    b = pl.program_id(0)
    tb = pl.program_id(1)
    lane_slot = lax.broadcasted_iota(jnp.int32, (wn, tk), 0)

    def window_start(first):
        return pl.multiple_of(jnp.minimum(first, cap - wn), 16)

    def weights(e, start, first):
        slot = start + lane_slot
        hit = jnp.logical_and(pos_ref[e:e + 1, :] == slot, slot >= first)
        return jnp.where(hit, gate_ref[e:e + 1, :], 0.0).astype(BF16)

    firsts = []
    for e in range(n_exp):
        base = (b * n_exp + e) * (n_tb + 1)
        first = (bs_ref[base + tb] // 16) * 16
        start = window_start(first)
        firsts.append((first, bs_ref[base + tb + 1]))
        w_ref[e * wn:(e + 1) * wn, :] = weights(e, start, first)
        y_ref[e * wn:(e + 1) * wn, :] = ye_ref[e, pl.ds(start, wn), :]
    out_ref[...] = h1_ref[...] + _dot_tn(w_ref[...], y_ref[...])

    for e in range(n_exp):
        first0, end = firsts[e]
        n_more = jnp.maximum(end - first0 - 1, 0) // wn

        def window(w, carry):
            first = first0 + (w + 1) * wn
            start = window_start(first)
            out_ref[...] += _dot_tn(weights(e, start, first), ye_ref[e, pl.ds(start, wn), :])
            return carry

        lax.fori_loop(0, n_more, window, 0)
    if final_norm:
        out_ref[...] = _rms(out_ref[...]) * fg_ref[...]


def _combine(bs, h1, ye, pos_t, gate_t, final_g, cap, tk, wn, final_norm):
    B, S, D = h1.shape
    E = ye.shape[1]
    n_tb = S // tk
    grid_spec = pltpu.PrefetchScalarGridSpec(
        num_scalar_prefetch=1,
        grid=(B, n_tb),
        in_specs=[
            pl.BlockSpec((None, tk, D), lambda b, t, bs: (b, t, 0)),
            pl.BlockSpec((None, E, cap, D), lambda b, t, bs: (b, 0, 0, 0), pipeline_mode=pl.Buffered(1)),
            pl.BlockSpec((None, None, E, tk), lambda b, t, bs: (b, t, 0, 0)),
            pl.BlockSpec((None, None, E, tk), lambda b, t, bs: (b, t, 0, 0)),
            pl.BlockSpec((1, D), lambda b, t, bs: (0, 0)),
        ],
        out_specs=pl.BlockSpec((None, tk, D), lambda b, t, bs: (b, t, 0)),
        scratch_shapes=[pltpu.VMEM((E * wn, tk), BF16), pltpu.VMEM((E * wn, D), BF16)],
    )
    return pl.pallas_call(
        functools.partial(_combine_kernel, n_tb=n_tb, tk=tk, wn=wn, cap=cap, n_exp=E,
                          final_norm=final_norm),
        grid_spec=grid_spec,
        out_shape=jax.ShapeDtypeStruct((B, S, D), F32),
        compiler_params=_cparams(2),
        name="combine",
    )(bs, h1, ye, pos_t, gate_t, final_g.reshape(1, D))


def _rope_tables(S):
    half = HEAD_DIM // 2
    inv_freq = ROPE_BASE ** (-jnp.arange(half, dtype=F32) / half)
    ang = jnp.arange(S).astype(F32)[:, None] * inv_freq[None, :]
    cos, sin = jnp.cos(ang), jnp.sin(ang)
    return jnp.concatenate([cos, cos], axis=-1), jnp.concatenate([-sin, sin], axis=-1)


def _block_starts(pos, tk, cap):
    B, E, _ = pos.shape
    bs = jnp.concatenate([pos[:, :, ::tk], jnp.full((B, E, 1), cap, jnp.int32)], axis=-1)
    return bs.reshape(-1)


def kernel(x, norm1_g, w_in, ret_norm_g, hgrn_norm_g, w_out, lower_bounds, norm2_g, w_router,
           w_gate, w_up, w_down, final_norm_g):
    B, S, D = x.shape
    depth = w_in.shape[0]
    E = w_router.shape[-1]
    cap = CAPACITY_FACTOR * S // E
    tm = min(512, S)
    tk = min(512, S)
    wn = min(128, cap)
    fm = min(512, cap)
    eg = min(8, E)
    n_tb = S // tk

    lbs = jax.nn.softmax(lower_bounds.astype(F32), axis=0)
    lbs = jnp.cumsum(lbs, axis=0) - lbs[0]
    cos, sin = _rope_tables(S)

    h = x
    for layer in range(depth):
        proj = _inproj(h, norm1_g[layer], w_in[layer].astype(BF16), cos, sin, tm)
        r_out = _retention(proj, ret_norm_g[layer])
        h_raw = _hgrn(proj, lbs[layer], first_layer=(layer == 0))
        h1, xn, aff = _outproj(r_out, h_raw, proj, hgrn_norm_g[layer], w_out[layer].astype(BF16),
                               h, norm2_g[layer], w_router[layer].T.astype(BF16), tm)
        pos, gate = _select(aff, cap)
        bs = _block_starts(pos, tk, cap)
        pos_t = pos.reshape(B, E, n_tb, tk).transpose(0, 2, 1, 3)
        gate_t = gate.reshape(B, E, n_tb, tk).transpose(0, 2, 1, 3)
        xe = _moe_gather(bs, xn, pos_t, gate_t, cap, tk, wn, eg)
        ye = _moe_ffn(xe, w_gate[layer].astype(BF16), w_up[layer].astype(BF16),
                      w_down[layer].astype(BF16), fm)
        h = _combine(bs, h1, ye, pos_t, gate_t, final_norm_g, cap, tk, wn,
                     final_norm=(layer == depth - 1))
    return h
```

```python
import functools

import numpy as np
import jax
import jax.numpy as jnp
from jax import lax
from jax.experimental import pallas as pl
from jax.experimental.pallas import tpu as pltpu

F32 = jnp.float32
BF16 = jnp.bfloat16

HEAD_DIM = 128
N_HEADS = 4
GROUP = N_HEADS * HEAD_DIM
N_GROUPS = 9
ROPE_BASE = 10000.0
NORM_EPS = 1e-6
CAPACITY_FACTOR = 2

RET_CHUNK = 256
RET_PAIR = 4
HGRN_CHUNK = 128
HGRN_PAIR = 4
LOG2E = 1.4426950408889634
VMEM_LIMIT = 58 * 1024 * 1024


def _cparams(n_axes):
    return pltpu.CompilerParams(
        dimension_semantics=("arbitrary",) * n_axes, vmem_limit_bytes=VMEM_LIMIT)


def _dot(a, b):
    return jnp.dot(a, b, preferred_element_type=F32)


def _dot_nt(a, b):
    return lax.dot_general(a, b, (((1,), (1,)), ((), ())), preferred_element_type=F32)


def _dot_tn(a, b):
    return lax.dot_general(a, b, (((0,), (0,)), ((), ())), preferred_element_type=F32)


def _silu(x):
    return x * (1.0 / (1.0 + jnp.exp(-x)))


def _rms(x):
    return x * lax.rsqrt(jnp.mean(x * x, axis=-1, keepdims=True) + NORM_EPS)


def _inproj_kernel(h_ref, g_ref, w_ref, cos_ref, sin_ref, out_ref):
    xn = (_rms(h_ref[...]) * g_ref[...]).astype(BF16)
    cos = cos_ref[...]
    sin = sin_ref[...]
    scale = HEAD_DIM ** -0.5
    for j in range(N_GROUPS):
        acc = _dot(xn, w_ref[:, j * GROUP:(j + 1) * GROUP])
        if j in (0, 1):
            for hh in range(N_HEADS):
                sl = acc[:, hh * HEAD_DIM:(hh + 1) * HEAD_DIM]
                rot = sl * cos + pltpu.roll(sl, HEAD_DIM // 2, 1) * sin
                if j == 1:
                    rot = rot * scale
                out_ref[:, j * GROUP + hh * HEAD_DIM:j * GROUP + (hh + 1) * HEAD_DIM] = rot.astype(BF16)
            continue
        if j in (3, 8):
            acc = _silu(acc)
        elif j == 4:
            acc = _silu(acc) * scale
        out_ref[:, j * GROUP:(j + 1) * GROUP] = acc.astype(BF16)


def _inproj(h, g, w_bf, cos, sin, tm):
    B, S, D = h.shape
    ncol = w_bf.shape[1]
    return pl.pallas_call(
        _inproj_kernel,
        grid=(B, S // tm),
        in_specs=[
            pl.BlockSpec((None, tm, D), lambda b, i: (b, i, 0)),
            pl.BlockSpec((1, D), lambda b, i: (0, 0)),
            pl.BlockSpec((D, ncol), lambda b, i: (0, 0), pipeline_mode=pl.Buffered(1)),
            pl.BlockSpec((tm, HEAD_DIM), lambda b, i: (i, 0)),
            pl.BlockSpec((tm, HEAD_DIM), lambda b, i: (i, 0)),
        ],
        out_specs=pl.BlockSpec((None, tm, ncol), lambda b, i: (b, i, 0)),
        out_shape=jax.ShapeDtypeStruct((B, S, ncol), BF16),
        compiler_params=_cparams(2),
        name="inproj",
    )(h, g.reshape(1, D), w_bf, cos, sin)


def _ret_kernel(q_ref, k_ref, v_ref, g_ref, dmat_ref, qdec_ref, kdec_ref, cdec_ref, rg_ref,
                out_ref, acc_ref, *, n_chunks):
    L = RET_CHUNK
    cdec = cdec_ref[0:1, :]
    zero_state = jnp.zeros((HEAD_DIM, HEAD_DIM), F32)

    def rows(i):
        return pl.ds(pl.multiple_of(i * L, L), L)

    def finish(sl, o):
        y = _rms(o) * rg_ref[...] * g_ref[sl, :].astype(F32)
        out_ref[sl, :] = y.astype(BF16)

    def step(i, states, second):
        state_f, state_b = states
        sfs = [rows(i * RET_PAIR + j) for j in range(RET_PAIR)]
        sbs = [rows(n_chunks - 1 - i * RET_PAIR - j) for j in range(RET_PAIR)]
        qf, kf, vf = ([r[s, :] for s in sfs] for r in (q_ref, k_ref, v_ref))
        qb, kb, vb = ([r[s, :] for s in sbs] for r in (q_ref, k_ref, v_ref))
        raw = [_dot_nt(q, k) for q, k in zip(qf, kf)]
        qdf = [(q.astype(F32) * qdec_ref[...]).astype(BF16) for q in qf]
        kdf = [(k.astype(F32) * kdec_ref[...]).astype(BF16) for k in kf]
        qdb = [(q.astype(F32) * kdec_ref[...]).astype(BF16) for q in qb]
        kdb = [(k.astype(F32) * qdec_ref[...]).astype(BF16) for k in kb]
        upd_f = [_dot_tn(k, v) for k, v in zip(kdf, vf)]
        upd_b = [_dot_tn(k, v) for k, v in zip(kdb, vb)]
        intra = [_dot((s * dmat_ref[...]).astype(BF16), v) for s, v in zip(raw, vf)]
        for j in range(RET_PAIR):
            of = intra[j] + _dot(qdf[j], state_f.astype(BF16))
            ob = _dot(qdb[j], state_b.astype(BF16))
            state_f = cdec * state_f + upd_f[j]
            state_b = cdec * state_b + upd_b[j]
            if second:
                finish(sfs[j], acc_ref[sfs[j], :] + of)
                finish(sbs[j], acc_ref[sbs[j], :] + ob)
            else:
                acc_ref[sfs[j], :] = of
                acc_ref[sbs[j], :] = ob
        return state_f, state_b

    steps = n_chunks // RET_PAIR
    states = lax.fori_loop(0, steps // 2, functools.partial(step, second=False),
                           (zero_state, zero_state))
    lax.fori_loop(steps // 2, steps, functools.partial(step, second=True), states)


def _ret_tables():
    L = RET_CHUNK
    t = np.arange(L, dtype=np.float64)
    lg = np.log1p(-(2.0 ** (-5.0 - np.arange(N_HEADS, dtype=np.float64))))
    dmat = np.exp(lg[:, None, None] * np.abs(t[:, None] - t[None, :])[None])
    qdec = np.exp(lg[:, None] * (t + 1.0))[:, :, None] * np.ones((1, 1, HEAD_DIM))
    kdec = np.exp(lg[:, None] * (L - 1.0 - t))[:, :, None] * np.ones((1, 1, HEAD_DIM))
    cdec = np.exp(lg * L)[:, None, None] * np.ones((1, 8, HEAD_DIM))
    return tuple(jnp.asarray(a, F32) for a in (dmat, qdec, kdec, cdec))


def _retention(proj, ret_g):
    B, S, _ = proj.shape
    L = RET_CHUNK
    assert S % (2 * RET_PAIR * L) == 0, "retention walks chunks from both ends, RET_PAIR at a time"
    dmat, qdec, kdec, cdec = _ret_tables()

    def col(c0):
        return pl.BlockSpec((None, S, HEAD_DIM), lambda b, h: (b, 0, c0 + h))

    def tab(r):
        return pl.BlockSpec((None, r, HEAD_DIM), lambda b, h: (h, 0, 0))

    return pl.pallas_call(
        functools.partial(_ret_kernel, n_chunks=S // L),
        grid=(B, N_HEADS),
        in_specs=[col(0), col(N_HEADS), col(2 * N_HEADS), col(3 * N_HEADS),
                  pl.BlockSpec((None, L, L), lambda b, h: (h, 0, 0)),
                  tab(L), tab(L), tab(8),
                  pl.BlockSpec((1, HEAD_DIM), lambda b, h: (0, h))],
        out_specs=pl.BlockSpec((None, S, HEAD_DIM), lambda b, h: (b, 0, h)),
        out_shape=jax.ShapeDtypeStruct((B, S, GROUP), BF16),
        scratch_shapes=[pltpu.VMEM((S, HEAD_DIM), F32)],
        compiler_params=_cparams(2),
        name="retention",
    )(proj, proj, proj, proj, dmat, qdec, kdec, cdec, ret_g.reshape(1, GROUP))


def _hgrn_chunks(chains, lb, first_layer):
    L = HGRN_CHUNK
    n = len(chains)
    qs = [c[0] for c in chains]
    vs = [c[2] for c in chains]
    states = [c[3] for c in chains]
    lvls = [c[5] for c in chains]
    backs = [c[6] for c in chains]

    kks, decs = [], []
    for (_, z, _, _, mtab, _, _) in chains:
        zf = z.astype(F32)
        e = jnp.exp(-jnp.abs(zf))
        inv = 1.0 / (1.0 + e)
        pos = zf >= 0.0
        sigm = jnp.where(pos, e * inv, inv)
        if first_layer:
            logf2 = jnp.minimum(zf, 0.0) * LOG2E - jnp.log2(1.0 + e)
            kk = sigm
        else:
            sig = jnp.where(pos, inv, e * inv)
            logf2 = jnp.log2(lb + (1.0 - lb) * sig)
            kk = (1.0 - lb) * sigm
        hi = logf2.astype(BF16)
        lo = (logf2 - hi.astype(F32)).astype(BF16)
        kks.append(kk)
        decs.append(_dot(mtab, jnp.concatenate([hi, lo], axis=0)))

    qfs = [q.astype(F32) for q in qs]
    cs = [d[0:L, :] for d in decs]
    scores = [jnp.where(lvls[i] == -1, _dot_nt(qs[i], kks[i].astype(BF16)), 0.0) for i in range(n)]
    m, level = 1, 0
    while m < L:
        for i in range(n):
            if m < 8:
                p = jnp.exp2(decs[i][(1 + level) * L:(2 + level) * L, :])
                sub = lax.broadcasted_iota(jnp.int32, (1, 8, HEAD_DIM), 1)
                is_q = ((sub >> level) & 1) == (0 if backs[i] else 1)
                w = jnp.where(is_q, qfs[i].reshape(L // 8, 8, HEAD_DIM),
                              kks[i].reshape(L // 8, 8, HEAD_DIM)).reshape(L, HEAD_DIM)
                r = (w * p).astype(BF16)
            else:
                slabs = []
                for j in range(L // m):
                    mid = (j // 2) * 2 * m + m
                    anchor = cs[i][mid:mid + 1, :] if backs[i] else cs[i][mid - 1:mid, :]
                    rows = cs[i][j * m:(j + 1) * m, :]
                    if (j % 2 == 1) != backs[i]:
                        slabs.append(qfs[i][j * m:(j + 1) * m, :] * jnp.exp2(rows - anchor))
                    else:
                        slabs.append(kks[i][j * m:(j + 1) * m, :] * jnp.exp2(anchor - rows))
                r = jnp.concatenate(slabs, axis=0).astype(BF16)
            scores[i] = jnp.where(lvls[i] == level, _dot_nt(r, r), scores[i])
        m, level = 2 * m, level + 1

    results = []
    latest = {}
    for i in range(n):
        state = states[i] if states[i] is not None else latest[backs[i]]
        edge = 0 if backs[i] else L - 1
        c_end = cs[i][edge:edge + 1, :]
        qe = (qfs[i] * jnp.exp2(cs[i])).astype(BF16)
        out = _dot(scores[i].astype(BF16), vs[i]) + _dot_nt(qe, state.astype(BF16))
        ke = (kks[i] * jnp.exp2(c_end - cs[i])).astype(BF16)
        latest[backs[i]] = jnp.exp2(c_end) * state + _dot_tn(vs[i], ke)
        results.append((out, latest[backs[i]]))
    return results


def _hgrn_kernel(q_ref, zf_ref, zb_ref, v_ref, lb_ref, mtabf_ref, mtabb_ref, lvlf_ref, lvlb_ref,
                 out_ref, *, n_chunks, first_layer):
    L = HGRN_CHUNK
    lb = lb_ref[...]
    zero_state = jnp.zeros((HEAD_DIM, HEAD_DIM), F32)

    def rows(i):
        return pl.ds(pl.multiple_of(i * L, L), L)

    def step(i, states, accumulate):
        chains, slices = [], []
        for k in range(HGRN_PAIR):
            sf = rows(i * HGRN_PAIR + k)
            sb = rows(n_chunks - 1 - i * HGRN_PAIR - k)
            chains.append((q_ref[sf, :], zf_ref[sf, :], v_ref[sf, :], states[0] if k == 0 else None,
                           mtabf_ref[...], lvlf_ref[...], False))
            chains.append((q_ref[sb, :], zb_ref[sb, :], v_ref[sb, :], states[1] if k == 0 else None,
                           mtabb_ref[...], lvlb_ref[...], True))
            slices += [sf, sb]
        results = _hgrn_chunks(chains, lb, first_layer)
        for sl, (o, _) in zip(slices, results):
            if accumulate:
                out_ref[sl, :] += o
            else:
                out_ref[sl, :] = o
        return results[-2][1], results[-1][1]

    steps = n_chunks // HGRN_PAIR
    states = lax.fori_loop(0, steps // 2, functools.partial(step, accumulate=False),
                           (zero_state, zero_state))
    lax.fori_loop(steps // 2, steps, functools.partial(step, accumulate=True), states)


def _hgrn_tables():
    L = HGRN_CHUNK
    t = np.arange(L)[:, None]
    u = np.arange(L)[None, :]
    x = t ^ u
    lev = np.floor(np.log2(np.maximum(x, 1))).astype(np.int32)
    lvl_f = np.where(t > u, lev, np.where(x == 0, -1, -2)).astype(np.int32)
    lvl_b = np.where(t < u, lev, np.where(x == 0, -1, -2)).astype(np.int32)

    def exponent_rows(backward):
        blocks = [(u >= t) if backward else (u <= t)]
        m = 1
        while m < 8:
            mid = (t // (2 * m)) * (2 * m) + m
            if backward:
                blocks.append(np.where(t < mid, (u >= t) & (u < mid), (u >= mid) & (u < t)))
            else:
                blocks.append(np.where(t >= mid, (u >= mid) & (u <= t), (u > t) & (u < mid)))
            m *= 2
        tab = np.concatenate(blocks, axis=0).astype(np.float32)
        return np.concatenate([tab, tab], axis=1)

    return (jnp.asarray(exponent_rows(False), BF16), jnp.asarray(exponent_rows(True), BF16),
            jnp.asarray(lvl_f, jnp.int32), jnp.asarray(lvl_b, jnp.int32))


def _hgrn(proj, lb, first_layer):
    B, S, _ = proj.shape
    L = HGRN_CHUNK
    assert S % (2 * HGRN_PAIR * L) == 0, "HGRN2 walks chunks from both ends, HGRN_PAIR at a time"
    mtab_f, mtab_b, lvl_f, lvl_b = _hgrn_tables()

    def col(g):
        return pl.BlockSpec((None, S, HEAD_DIM), lambda b, h: (b, 0, g * N_HEADS + h))

    def full(a):
        return pl.BlockSpec(a.shape, lambda b, h: (0, 0))

    return pl.pallas_call(
        functools.partial(_hgrn_kernel, n_chunks=S // L, first_layer=first_layer),
        grid=(B, N_HEADS),
        in_specs=[col(4), col(5), col(6), col(7),
                  pl.BlockSpec((1, HEAD_DIM), lambda b, h: (0, h)),
                  full(mtab_f), full(mtab_b), full(lvl_f), full(lvl_b)],
        out_specs=pl.BlockSpec((None, S, HEAD_DIM), lambda b, h: (b, 0, h)),
        out_shape=jax.ShapeDtypeStruct((B, S, GROUP), F32),
        compiler_params=_cparams(2),
        name="hgrn",
    )(proj, proj, proj, proj, lb.reshape(1, GROUP), mtab_f, mtab_b, lvl_f, lvl_b)


def _outproj_kernel(r_ref, hraw_ref, hg_ref, hgn_ref, w_ref, h_ref, g2_ref, wr_ref,
                    h1_ref, xn_ref, aff_ref):
    hn = _rms(hraw_ref[...]) * hgn_ref[...] * hg_ref[...].astype(F32)
    mix = _dot(r_ref[...], w_ref[0:GROUP, :]) + _dot(hn.astype(BF16), w_ref[GROUP:2 * GROUP, :])
    h1 = h_ref[...] + mix
    h1_ref[...] = h1
    xn = (_rms(h1) * g2_ref[...]).astype(BF16)
    xn_ref[...] = xn
    logits = _dot_nt(wr_ref[...], xn)
    mx = jnp.max(logits, axis=0, keepdims=True)
    ex = jnp.exp(logits - mx)
    aff_ref[...] = ex / jnp.sum(ex, axis=0, keepdims=True)


def _outproj(r_out, h_raw, proj, hgrn_g, w_out_bf, h, g2, wr_t_bf, tm):
    B, S, D = h.shape
    E = wr_t_bf.shape[0]
    return pl.pallas_call(
        _outproj_kernel,
        grid=(B, S // tm),
        in_specs=[
            pl.BlockSpec((None, tm, GROUP), lambda b, i: (b, i, 0)),
            pl.BlockSpec((None, tm, GROUP), lambda b, i: (b, i, 0)),
            pl.BlockSpec((None, tm, GROUP), lambda b, i: (b, i, N_GROUPS - 1)),
            pl.BlockSpec((1, GROUP), lambda b, i: (0, 0)),
            pl.BlockSpec((2 * GROUP, D), lambda b, i: (0, 0)),
            pl.BlockSpec((None, tm, D), lambda b, i: (b, i, 0)),
            pl.BlockSpec((1, D), lambda b, i: (0, 0)),
            pl.BlockSpec((E, D), lambda b, i: (0, 0)),
        ],
        out_specs=[
            pl.BlockSpec((None, tm, D), lambda b, i: (b, i, 0)),
            pl.BlockSpec((None, tm, D), lambda b, i: (b, i, 0)),
            pl.BlockSpec((None, E, tm), lambda b, i: (b, 0, i)),
        ],
        out_shape=[jax.ShapeDtypeStruct((B, S, D), F32),
                   jax.ShapeDtypeStruct((B, S, D), BF16),
                   jax.ShapeDtypeStruct((B, E, S), F32)],
        compiler_params=_cparams(2),
        name="outproj",
    )(r_out, h_raw, proj, hgrn_g.reshape(1, GROUP), w_out_bf, h, g2.reshape(1, D), wr_t_bf)


def _select_kernel(aff_ref, pos_ref, gate_ref, *, cap):
    a = aff_ref[...]
    E, S = a.shape
    u = pltpu.bitcast(a, jnp.int32)
    capf = jnp.float32(cap)

    def count(mask):
        return jnp.sum(jnp.where(mask, 1.0, 0.0), axis=-1, keepdims=True)

    def value_bit(i, thr):
        cand = thr | (jnp.int32(1) << (30 - i))
        return jnp.where(count(u >= cand) >= capf, cand, thr)

    thr = lax.fori_loop(0, 31, value_bit, jnp.zeros((E, 1), jnp.int32))
    gt = u > thr
    eq = u == thr
    need = capf - count(gt)
    idx = lax.broadcasted_iota(jnp.int32, (E, S), 1)
    nbits = int(S).bit_length()

    def index_bit(i, cut):
        cand = cut | (jnp.int32(1) << (nbits - 1 - i))
        return jnp.where(count(eq & (idx < cand)) <= need, cand, cut)

    cut = lax.fori_loop(0, nbits, index_bit, jnp.zeros((E, 1), jnp.int32))
    sel = gt | (eq & (idx < cut))
    gate_ref[...] = jnp.where(sel, a, 0.0)

    li = lax.broadcasted_iota(jnp.int32, (128, 128), 0)
    lj = lax.broadcasted_iota(jnp.int32, (128, 128), 1)
    upper = jnp.where(li <= lj, 1.0, 0.0).astype(BF16)
    carry = jnp.zeros((E, 1), F32)
    for j in range(S // 128):
        sb = jnp.where(sel[:, j * 128:(j + 1) * 128], 1.0, 0.0)
        incl = _dot(sb.astype(BF16), upper)
        pos_ref[:, j * 128:(j + 1) * 128] = (incl - sb + carry).astype(jnp.int32)
        carry = carry + incl[:, 127:128]


def _select(aff, cap):
    B, E, S = aff.shape
    spec = pl.BlockSpec((None, E, S), lambda b: (b, 0, 0))
    return pl.pallas_call(
        functools.partial(_select_kernel, cap=cap),
        grid=(B,),
        in_specs=[spec],
        out_specs=[spec, spec],
        out_shape=[jax.ShapeDtypeStruct((B, E, S), jnp.int32),
                   jax.ShapeDtypeStruct((B, E, S), F32)],
        compiler_params=_cparams(1),
        name="select",
    )(aff)


def _gather_kernel(bs_ref, x_ref, pos_ref, gate_ref, xe_ref, oh_ref,
                   *, n_tb, tk, wn, cap, n_exp, eg):
    b = pl.program_id(0)
    grp = pl.program_id(1)
    tb = pl.program_id(2)
    lane_slot = lax.broadcasted_iota(jnp.int32, (wn, tk), 0)

    @pl.when(tb == 0)
    def _():
        xe_ref[...] = jnp.zeros_like(xe_ref)

    def onehot(el, start, first):
        key = jnp.where(gate_ref[el:el + 1, :] > 0.0, pos_ref[el:el + 1, :], -1)
        slot = start + lane_slot
        hit = jnp.logical_and(key == slot, slot >= first)
        return jnp.where(hit, 1.0, 0.0).astype(BF16)

    def window_start(first):
        return pl.multiple_of(jnp.minimum(first, cap - wn), 16)

    spans = []
    for el in range(eg):
        base = (b * n_exp + grp * eg + el) * (n_tb + 1)
        first = (bs_ref[base + tb] // 16) * 16
        start = window_start(first)
        spans.append((first, start, bs_ref[base + tb + 1]))
        oh_ref[el * wn:(el + 1) * wn, :] = onehot(el, start, first)
    rows = _dot(oh_ref[...], x_ref[...]).astype(BF16)
    for el in range(eg):
        xe_ref[el, pl.ds(spans[el][1], wn), :] += rows[el * wn:(el + 1) * wn, :]

    for el in range(eg):
        first0, _, end = spans[el]
        n_more = jnp.maximum(end - first0 - 1, 0) // wn

        def window(w, carry):
            first = first0 + (w + 1) * wn
            start = window_start(first)
            xe_ref[el, pl.ds(start, wn), :] += _dot(onehot(el, start, first), x_ref[...]).astype(BF16)
            return carry

        lax.fori_loop(0, n_more, window, 0)


def _moe_gather(bs, xn, pos_t, gate_t, cap, tk, wn, eg):
    B, S, D = xn.shape
    E = pos_t.shape[2]
    n_tb = S // tk
    grid_spec = pltpu.PrefetchScalarGridSpec(
        num_scalar_prefetch=1,
        grid=(B, E // eg, n_tb),
        in_specs=[
            pl.BlockSpec((None, tk, D), lambda b, g, t, bs: (b, t, 0)),
            pl.BlockSpec((None, None, eg, tk), lambda b, g, t, bs: (b, t, g, 0)),
            pl.BlockSpec((None, None, eg, tk), lambda b, g, t, bs: (b, t, g, 0)),
        ],
        out_specs=pl.BlockSpec((None, eg, cap, D), lambda b, g, t, bs: (b, g, 0, 0)),
        scratch_shapes=[pltpu.VMEM((eg * wn, tk), BF16)],
    )
    return pl.pallas_call(
        functools.partial(_gather_kernel, n_tb=n_tb, tk=tk, wn=wn, cap=cap, n_exp=E, eg=eg),
        grid_spec=grid_spec,
        out_shape=jax.ShapeDtypeStruct((B, E, cap, D), BF16),
        compiler_params=_cparams(3),
        name="moe_gather",
    )(bs, xn, pos_t, gate_t)


def _ffn_kernel(xe_ref, wg_ref, wu_ref, wd_ref, ye_ref, *, fm):
    for r in range(xe_ref.shape[0] // fm):
        xe = xe_ref[r * fm:(r + 1) * fm, :]
        hid = (_silu(_dot(xe, wg_ref[...])) * _dot(xe, wu_ref[...])).astype(BF16)
        ye_ref[r * fm:(r + 1) * fm, :] = _dot(hid, wd_ref[...]).astype(BF16)


def _moe_ffn(xe, wg, wu, wd, fm):
    B, E, cap, D = xe.shape
    FF = wg.shape[-1]
    rows = pl.BlockSpec((None, None, cap, D), lambda e, b: (b, e, 0, 0))
    return pl.pallas_call(
        functools.partial(_ffn_kernel, fm=fm),
        grid=(E, B),
        in_specs=[rows,
                  pl.BlockSpec((None, D, FF), lambda e, b: (e, 0, 0)),
                  pl.BlockSpec((None, D, FF), lambda e, b: (e, 0, 0)),
                  pl.BlockSpec((None, FF, D), lambda e, b: (e, 0, 0))],
        out_specs=rows,
        out_shape=jax.ShapeDtypeStruct((B, E, cap, D), BF16),
        compiler_params=_cparams(2),
        name="moe_ffn",
    )(xe, wg, wu, wd)


def _combine_kernel(bs_ref, h1_ref, ye_ref, pos_ref, gate_ref, fg_ref, out_ref, w_ref, y_ref,
                    *, n_tb, tk, wn, cap, n_exp, final_norm):
    b = pl.program_id(0)
    tb = pl.program_id(1)
    lane_slot = lax.broadcasted_iota(jnp.int32, (wn, tk), 0)

    def window_start(first):
        return pl.multiple_of(jnp.minimum(first, cap - wn), 16)

    def weights(e, start, first):
        slot = start + lane_slot
        hit = jnp.logical_and(pos_ref[e:e + 1, :] == slot, slot >= first)
        return jnp.where(hit, gate_ref[e:e + 1, :], 0.0).astype(BF16)

    firsts = []
    for e in range(n_exp):
        base = (b * n_exp + e) * (n_tb + 1)
        first = (bs_ref[base + tb] // 16) * 16
        start = window_start(first)
        firsts.append((first, bs_ref[base + tb + 1]))
        w_ref[e * wn:(e + 1) * wn, :] = weights(e, start, first)
        y_ref[e * wn:(e + 1) * wn, :] = ye_ref[e, pl.ds(start, wn), :]
    out_ref[...] = h1_ref[...] + _dot_tn(w_ref[...], y_ref[...])

    for e in range(n_exp):
        first0, end = firsts[e]
        n_more = jnp.maximum(end - first0 - 1, 0) // wn

        def window(w, carry):
            first = first0 + (w + 1) * wn
            start = window_start(first)
            out_ref[...] += _dot_tn(weights(e, start, first), ye_ref[e, pl.ds(start, wn), :])
            return carry

        lax.fori_loop(0, n_more, window, 0)
    if final_norm:
        out_ref[...] = _rms(out_ref[...]) * fg_ref[...]


def _combine(bs, h1, ye, pos_t, gate_t, final_g, cap, tk, wn, final_norm):
    B, S, D = h1.shape
    E = ye.shape[1]
    n_tb = S // tk
    grid_spec = pltpu.PrefetchScalarGridSpec(
        num_scalar_prefetch=1,
        grid=(B, n_tb),
        in_specs=[
            pl.BlockSpec((None, tk, D), lambda b, t, bs: (b, t, 0)),
            pl.BlockSpec((None, E, cap, D), lambda b, t, bs: (b, 0, 0, 0), pipeline_mode=pl.Buffered(1)),
            pl.BlockSpec((None, None, E, tk), lambda b, t, bs: (b, t, 0, 0)),
            pl.BlockSpec((None, None, E, tk), lambda b, t, bs: (b, t, 0, 0)),
            pl.BlockSpec((1, D), lambda b, t, bs: (0, 0)),
        ],
        out_specs=pl.BlockSpec((None, tk, D), lambda b, t, bs: (b, t, 0)),
        scratch_shapes=[pltpu.VMEM((E * wn, tk), BF16), pltpu.VMEM((E * wn, D), BF16)],
    )
    return pl.pallas_call(
        functools.partial(_combine_kernel, n_tb=n_tb, tk=tk, wn=wn, cap=cap, n_exp=E,
                          final_norm=final_norm),
        grid_spec=grid_spec,
        out_shape=jax.ShapeDtypeStruct((B, S, D), F32),
        compiler_params=_cparams(2),
        name="combine",
    )(bs, h1, ye, pos_t, gate_t, final_g.reshape(1, D))


def _rope_tables(S):
    half = HEAD_DIM // 2
    inv_freq = ROPE_BASE ** (-jnp.arange(half, dtype=F32) / half)
    ang = jnp.arange(S).astype(F32)[:, None] * inv_freq[None, :]
    cos, sin = jnp.cos(ang), jnp.sin(ang)
    return jnp.concatenate([cos, cos], axis=-1), jnp.concatenate([-sin, sin], axis=-1)


def _block_starts(pos, tk, cap):
    B, E, _ = pos.shape
    bs = jnp.concatenate([pos[:, :, ::tk], jnp.full((B, E, 1), cap, jnp.int32)], axis=-1)
    return bs.reshape(-1)


def kernel(x, norm1_g, w_in, ret_norm_g, hgrn_norm_g, w_out, lower_bounds, norm2_g, w_router,
           w_gate, w_up, w_down, final_norm_g):
    B, S, D = x.shape
    depth = w_in.shape[0]
    E = w_router.shape[-1]
    cap = CAPACITY_FACTOR * S // E
    tm = min(512, S)
    tk = min(256, S)
    wn = min(64, cap)
    fm = min(512, cap)
    eg = min(8, E)
    n_tb = S // tk

    lbs = jax.nn.softmax(lower_bounds.astype(F32), axis=0)
    lbs = jnp.cumsum(lbs, axis=0) - lbs[0]
    cos, sin = _rope_tables(S)

    h = x
    for layer in range(depth):
        proj = _inproj(h, norm1_g[layer], w_in[layer].astype(BF16), cos, sin, tm)
        r_out = _retention(proj, ret_norm_g[layer])
        h_raw = _hgrn(proj, lbs[layer], first_layer=(layer == 0))
        h1, xn, aff = _outproj(r_out, h_raw, proj, hgrn_norm_g[layer], w_out[layer].astype(BF16),
                               h, norm2_g[layer], w_router[layer].T.astype(BF16), tm)
        pos, gate = _select(aff, cap)
        bs = _block_starts(pos, tk, cap)
        pos_t = pos.reshape(B, E, n_tb, tk).transpose(0, 2, 1, 3)
        gate_t = gate.reshape(B, E, n_tb, tk).transpose(0, 2, 1, 3)
        xe = _moe_gather(bs, xn, pos_t, gate_t, cap, tk, wn, eg)
        ye = _moe_ffn(xe, w_gate[layer].astype(BF16), w_up[layer].astype(BF16),
                      w_down[layer].astype(BF16), fm)
        h = _combine(bs, h1, ye, pos_t, gate_t, final_norm_g, cap, tk, wn,
                     final_norm=(layer == depth - 1))
    return h
```

```python
import functools

import numpy as np
import jax
import jax.numpy as jnp
from jax import lax
from jax.experimental import pallas as pl
from jax.experimental.pallas import tpu as pltpu

F32 = jnp.float32
BF16 = jnp.bfloat16

HEAD_DIM = 128
N_HEADS = 4
GROUP = N_HEADS * HEAD_DIM
N_GROUPS = 9
ROPE_BASE = 10000.0
NORM_EPS = 1e-6
CAPACITY_FACTOR = 2

RET_CHUNK = 256
RET_PAIR = 4
HGRN_CHUNK = 128
HGRN_PAIR = 4
HGRN_DESC = 16
LOG2E = 1.4426950408889634
VMEM_LIMIT = 58 * 1024 * 1024


def _cparams(n_axes):
    return pltpu.CompilerParams(
        dimension_semantics=("arbitrary",) * n_axes, vmem_limit_bytes=VMEM_LIMIT)


def _dot(a, b):
    return jnp.dot(a, b, preferred_element_type=F32)


def _dot_nt(a, b):
    return lax.dot_general(a, b, (((1,), (1,)), ((), ())), preferred_element_type=F32)


def _dot_tn(a, b):
    return lax.dot_general(a, b, (((0,), (0,)), ((), ())), preferred_element_type=F32)


def _silu(x):
    return x * (1.0 / (1.0 + jnp.exp(-x)))


def _rms(x):
    return x * lax.rsqrt(jnp.mean(x * x, axis=-1, keepdims=True) + NORM_EPS)


def _inproj_kernel(h_ref, g_ref, w_ref, cos_ref, sin_ref, out_ref):
    xn = (_rms(h_ref[...]) * g_ref[...]).astype(BF16)
    cos = cos_ref[...]
    sin = sin_ref[...]
    scale = HEAD_DIM ** -0.5
    for j in range(N_GROUPS):
        acc = _dot(xn, w_ref[:, j * GROUP:(j + 1) * GROUP])
        if j in (0, 1):
            for hh in range(N_HEADS):
                sl = acc[:, hh * HEAD_DIM:(hh + 1) * HEAD_DIM]
                rot = sl * cos + pltpu.roll(sl, HEAD_DIM // 2, 1) * sin
                if j == 1:
                    rot = rot * scale
                out_ref[:, j * GROUP + hh * HEAD_DIM:j * GROUP + (hh + 1) * HEAD_DIM] = rot.astype(BF16)
            continue
        if j in (3, 8):
            acc = _silu(acc)
        elif j == 4:
            acc = _silu(acc) * scale
        out_ref[:, j * GROUP:(j + 1) * GROUP] = acc.astype(BF16)


def _inproj(h, g, w_bf, cos, sin, tm):
    B, S, D = h.shape
    ncol = w_bf.shape[1]
    return pl.pallas_call(
        _inproj_kernel,
        grid=(B, S // tm),
        in_specs=[
            pl.BlockSpec((None, tm, D), lambda b, i: (b, i, 0)),
            pl.BlockSpec((1, D), lambda b, i: (0, 0)),
            pl.BlockSpec((D, ncol), lambda b, i: (0, 0), pipeline_mode=pl.Buffered(1)),
            pl.BlockSpec((tm, HEAD_DIM), lambda b, i: (i, 0)),
            pl.BlockSpec((tm, HEAD_DIM), lambda b, i: (i, 0)),
        ],
        out_specs=pl.BlockSpec((None, tm, ncol), lambda b, i: (b, i, 0)),
        out_shape=jax.ShapeDtypeStruct((B, S, ncol), BF16),
        compiler_params=_cparams(2),
        name="inproj",
    )(h, g.reshape(1, D), w_bf, cos, sin)


def _ret_kernel(q_ref, k_ref, v_ref, g_ref, dmat_ref, qdec_ref, kdec_ref, cdec_ref, rg_ref,
                out_ref, acc_ref, *, n_chunks):
    L = RET_CHUNK
    cdec = cdec_ref[0:1, :]
    zero_state = jnp.zeros((HEAD_DIM, HEAD_DIM), F32)

    def rows(i):
        return pl.ds(pl.multiple_of(i * L, L), L)

    def finish(sl, o):
        y = _rms(o) * rg_ref[...] * g_ref[sl, :].astype(F32)
        out_ref[sl, :] = y.astype(BF16)

    def step(i, states, second):
        state_f, state_b = states
        sfs = [rows(i * RET_PAIR + j) for j in range(RET_PAIR)]
        sbs = [rows(n_chunks - 1 - i * RET_PAIR - j) for j in range(RET_PAIR)]
        qf, kf, vf = ([r[s, :] for s in sfs] for r in (q_ref, k_ref, v_ref))
        qb, kb, vb = ([r[s, :] for s in sbs] for r in (q_ref, k_ref, v_ref))
        raw = [_dot_nt(q, k) for q, k in zip(qf, kf)]
        qdf = [(q.astype(F32) * qdec_ref[...]).astype(BF16) for q in qf]
        kdf = [(k.astype(F32) * kdec_ref[...]).astype(BF16) for k in kf]
        qdb = [(q.astype(F32) * kdec_ref[...]).astype(BF16) for q in qb]
        kdb = [(k.astype(F32) * qdec_ref[...]).astype(BF16) for k in kb]
        upd_f = [_dot_tn(k, v) for k, v in zip(kdf, vf)]
        upd_b = [_dot_tn(k, v) for k, v in zip(kdb, vb)]
        intra = [_dot((s * dmat_ref[...]).astype(BF16), v) for s, v in zip(raw, vf)]
        for j in range(RET_PAIR):
            of = intra[j] + _dot(qdf[j], state_f.astype(BF16))
            ob = _dot(qdb[j], state_b.astype(BF16))
            state_f = cdec * state_f + upd_f[j]
            state_b = cdec * state_b + upd_b[j]
            if second:
                finish(sfs[j], acc_ref[sfs[j], :] + of)
                finish(sbs[j], acc_ref[sbs[j], :] + ob)
            else:
                acc_ref[sfs[j], :] = of
                acc_ref[sbs[j], :] = ob
        return state_f, state_b

    steps = n_chunks // RET_PAIR
    states = lax.fori_loop(0, steps // 2, functools.partial(step, second=False),
                           (zero_state, zero_state))
    lax.fori_loop(steps // 2, steps, functools.partial(step, second=True), states)


def _ret_tables():
    L = RET_CHUNK
    t = np.arange(L, dtype=np.float64)
    lg = np.log1p(-(2.0 ** (-5.0 - np.arange(N_HEADS, dtype=np.float64))))
    dmat = np.exp(lg[:, None, None] * np.abs(t[:, None] - t[None, :])[None])
    qdec = np.exp(lg[:, None] * (t + 1.0))[:, :, None] * np.ones((1, 1, HEAD_DIM))
    kdec = np.exp(lg[:, None] * (L - 1.0 - t))[:, :, None] * np.ones((1, 1, HEAD_DIM))
    cdec = np.exp(lg * L)[:, None, None] * np.ones((1, 8, HEAD_DIM))
    return tuple(jnp.asarray(a, F32) for a in (dmat, qdec, kdec, cdec))


def _retention(proj, ret_g):
    B, S, _ = proj.shape
    L = RET_CHUNK
    assert S % (2 * RET_PAIR * L) == 0, "retention walks chunks from both ends, RET_PAIR at a time"
    dmat, qdec, kdec, cdec = _ret_tables()

    def col(c0):
        return pl.BlockSpec((None, S, HEAD_DIM), lambda b, h: (b, 0, c0 + h))

    def tab(r):
        return pl.BlockSpec((None, r, HEAD_DIM), lambda b, h: (h, 0, 0))

    return pl.pallas_call(
        functools.partial(_ret_kernel, n_chunks=S // L),
        grid=(B, N_HEADS),
        in_specs=[col(0), col(N_HEADS), col(2 * N_HEADS), col(3 * N_HEADS),
                  pl.BlockSpec((None, L, L), lambda b, h: (h, 0, 0)),
                  tab(L), tab(L), tab(8),
                  pl.BlockSpec((1, HEAD_DIM), lambda b, h: (0, h))],
        out_specs=pl.BlockSpec((None, S, HEAD_DIM), lambda b, h: (b, 0, h)),
        out_shape=jax.ShapeDtypeStruct((B, S, GROUP), BF16),
        scratch_shapes=[pltpu.VMEM((S, HEAD_DIM), F32)],
        compiler_params=_cparams(2),
        name="retention",
    )(proj, proj, proj, proj, dmat, qdec, kdec, cdec, ret_g.reshape(1, GROUP))


def _hgrn_gates(z, lb, first_layer):
    zf = z.astype(F32)
    e = jnp.exp(-jnp.abs(zf))
    inv = 1.0 / (1.0 + e)
    pos = zf >= 0.0
    sigm = jnp.where(pos, e * inv, inv)
    if first_layer:
        return jnp.minimum(zf, 0.0) * LOG2E - jnp.log2(1.0 + e), sigm
    sig = jnp.where(pos, inv, e * inv)
    return jnp.log2(lb + (1.0 - lb) * sig), (1.0 - lb) * sigm


def _hgrn_intra(chunks, lb, mtab_f, mtab_b, lvl, first_layer):
    L = HGRN_CHUNK
    n = len(chunks)
    qfs = [c[0].astype(F32) for c in chunks]
    gates_f = [_hgrn_gates(c[1], lb, first_layer) for c in chunks]
    gates_b = [_hgrn_gates(c[2], lb, first_layer) for c in chunks]

    def exponents(gates, mtab):
        splits = []
        for logf2, _ in gates:
            hi = logf2.astype(BF16)
            splits.append(jnp.concatenate([hi, (logf2 - hi.astype(F32)).astype(BF16)], axis=0))
        decs = []
        for a in range(0, n, 2):
            both = _dot(mtab, jnp.concatenate(splits[a:a + 2], axis=1))
            decs += [both[:, j * HEAD_DIM:(j + 1) * HEAD_DIM] for j in range(len(splits[a:a + 2]))]
        return decs

    decs_f = exponents(gates_f, mtab_f)
    decs_b = exponents(gates_b, mtab_b)
    cfs = [d[0:L, :] for d in decs_f]
    cbs = [d[0:L, :] for d in decs_b]
    kfs = [g[1] for g in gates_f]
    kbs = [g[1] for g in gates_b]

    scores = [jnp.where(lvl == -1, _dot_nt(chunks[i][0], (kfs[i] + kbs[i]).astype(BF16)), 0.0)
              for i in range(n)]
    m, level = 1, 0
    while m < L:
        for i in range(n):
            if m < 8:
                def as3(x):
                    return x.reshape(L // 8, 8, HEAD_DIM)
                pf = as3(jnp.exp2(decs_f[i][(1 + level) * L:(2 + level) * L, :]))
                pb = as3(jnp.exp2(decs_b[i][(1 + level) * L:(2 + level) * L, :]))
                sub = lax.broadcasted_iota(jnp.int32, (1, 8, HEAD_DIM), 1)
                upper = ((sub >> level) & 1) == 1
                qside = (as3(qfs[i]) * jnp.where(upper, pf, pb)).reshape(L, HEAD_DIM)
                kside = jnp.where(upper, as3(kbs[i]) * pb, as3(kfs[i]) * pf).reshape(L, HEAD_DIM)
            else:
                qslabs, kslabs = [], []
                for j in range(L // m):
                    mid = (j // 2) * 2 * m + m
                    sl = slice(j * m, (j + 1) * m)
                    if j % 2 == 1:
                        qdec = jnp.exp2(cfs[i][sl, :] - cfs[i][mid - 1:mid, :])
                        kslabs.append(kbs[i][sl, :] * jnp.exp2(cbs[i][mid:mid + 1, :] - cbs[i][sl, :]))
                    else:
                        qdec = jnp.exp2(cbs[i][sl, :] - cbs[i][mid:mid + 1, :])
                        kslabs.append(kfs[i][sl, :] * jnp.exp2(cfs[i][mid - 1:mid, :] - cfs[i][sl, :]))
                    qslabs.append(qfs[i][sl, :] * qdec)
                qside = jnp.concatenate(qslabs, axis=0)
                kside = jnp.concatenate(kslabs, axis=0)
            s = _dot_nt(qside.astype(BF16), kside.astype(BF16))
            scores[i] = jnp.where(lvl == level, s, scores[i])
        m, level = 2 * m, level + 1

    results = []
    for i in range(n):
        intra = _dot(scores[i].astype(BF16), chunks[i][3])
        cf_end = cfs[i][L - 1:L, :]
        cb_end = cbs[i][0:1, :]
        fwd = ((qfs[i] * jnp.exp2(cfs[i])).astype(BF16),
               (kfs[i] * jnp.exp2(cf_end - cfs[i])).astype(BF16), jnp.exp2(cf_end))
        bwd = ((qfs[i] * jnp.exp2(cbs[i])).astype(BF16),
               (kbs[i] * jnp.exp2(cb_end - cbs[i])).astype(BF16), jnp.exp2(cb_end))
        results.append((intra, fwd, bwd))
    return results


def _hgrn_kernel(q_ref, zf_ref, zb_ref, v_ref, lb_ref, mtabf_ref, mtabb_ref, lvl_ref,
                 out_ref, qeb_ref, keb_ref, dend_ref, *, n_chunks, first_layer):
    L = HGRN_CHUNK
    lb = lb_ref[...]
    zero_state = jnp.zeros((HEAD_DIM, HEAD_DIM), F32)

    def rows(c):
        return pl.ds(pl.multiple_of(c * L, L), L)

    def ascend(i, state):
        cs = [i * HGRN_PAIR + j for j in range(HGRN_PAIR)]
        sls = [rows(c) for c in cs]
        res = _hgrn_intra([(q_ref[sl, :], zf_ref[sl, :], zb_ref[sl, :], v_ref[sl, :]) for sl in sls],
                          lb, mtabf_ref[...], mtabb_ref[...], lvl_ref[...], first_layer)
        updates = [_dot_tn(v_ref[sl, :], r[1][1]) for sl, r in zip(sls, res)]
        for c, sl, (intra, (qe_f, _, dend_f), (qe_b, ke_b, dend_b)), upd in zip(cs, sls, res, updates):
            out_ref[sl, :] = intra + _dot_nt(qe_f, state.astype(BF16))
            state = dend_f * state + upd
            qeb_ref[sl, :] = qe_b
            keb_ref[sl, :] = ke_b
            dend_ref[c] = jnp.broadcast_to(dend_b, (8, HEAD_DIM))
        return state

    lax.fori_loop(0, n_chunks // HGRN_PAIR, ascend, zero_state)

    def descend(i, state):
        cs = [n_chunks - 1 - i * HGRN_DESC - j for j in range(HGRN_DESC)]
        sls = [rows(c) for c in cs]
        updates = [_dot_tn(v_ref[sl, :], keb_ref[sl, :]) for sl in sls]
        for c, sl, upd in zip(cs, sls, updates):
            out_ref[sl, :] += _dot_nt(qeb_ref[sl, :], state.astype(BF16))
            state = dend_ref[c][0:1, :] * state + upd
        return state

    lax.fori_loop(0, n_chunks // HGRN_DESC, descend, zero_state)


def _hgrn_tables():
    L = HGRN_CHUNK
    t = np.arange(L)[:, None]
    u = np.arange(L)[None, :]
    x = t ^ u
    lvl = np.where(x == 0, -1, np.floor(np.log2(np.maximum(x, 1)))).astype(np.int32)

    def exponent_rows(backward):
        blocks = [(u >= t) if backward else (u <= t)]
        m = 1
        while m < 8:
            mid = (t // (2 * m)) * (2 * m) + m
            if backward:
                blocks.append(np.where(t < mid, (u >= t) & (u < mid), (u >= mid) & (u < t)))
            else:
                blocks.append(np.where(t >= mid, (u >= mid) & (u <= t), (u > t) & (u < mid)))
            m *= 2
        tab = np.concatenate(blocks, axis=0).astype(np.float32)
        return np.concatenate([tab, tab], axis=1)

    return (jnp.asarray(exponent_rows(False), BF16), jnp.asarray(exponent_rows(True), BF16),
            jnp.asarray(lvl, jnp.int32))


def _hgrn(proj, lb, first_layer):
    B, S, _ = proj.shape
    L = HGRN_CHUNK
    n_chunks = S // L
    assert S % L == 0 and n_chunks % HGRN_PAIR == 0 and n_chunks % HGRN_DESC == 0
    mtab_f, mtab_b, lvl = _hgrn_tables()

    def col(g):
        return pl.BlockSpec((None, S, HEAD_DIM), lambda b, h: (b, 0, g * N_HEADS + h))

    def full(a):
        return pl.BlockSpec(a.shape, lambda b, h: (0, 0))

    return pl.pallas_call(
        functools.partial(_hgrn_kernel, n_chunks=n_chunks, first_layer=first_layer),
        grid=(B, N_HEADS),
        in_specs=[col(4), col(5), col(6), col(7),
                  pl.BlockSpec((1, HEAD_DIM), lambda b, h: (0, h)),
                  full(mtab_f), full(mtab_b), full(lvl)],
        out_specs=pl.BlockSpec((None, S, HEAD_DIM), lambda b, h: (b, 0, h)),
        out_shape=jax.ShapeDtypeStruct((B, S, GROUP), F32),
        scratch_shapes=[pltpu.VMEM((S, HEAD_DIM), BF16), pltpu.VMEM((S, HEAD_DIM), BF16),
                        pltpu.VMEM((n_chunks, 8, HEAD_DIM), F32)],
        compiler_params=_cparams(2),
        name="hgrn",
    )(proj, proj, proj, proj, lb.reshape(1, GROUP), mtab_f, mtab_b, lvl)


def _outproj_kernel(r_ref, hraw_ref, hg_ref, hgn_ref, w_ref, h_ref, g2_ref, wr_ref,
                    h1_ref, xn_ref, aff_ref):
    hn = _rms(hraw_ref[...]) * hgn_ref[...] * hg_ref[...].astype(F32)
    mix = _dot(r_ref[...], w_ref[0:GROUP, :]) + _dot(hn.astype(BF16), w_ref[GROUP:2 * GROUP, :])
    h1 = h_ref[...] + mix
    h1_ref[...] = h1
    xn = (_rms(h1) * g2_ref[...]).astype(BF16)
    xn_ref[...] = xn
    logits = _dot_nt(wr_ref[...], xn)
    mx = jnp.max(logits, axis=0, keepdims=True)
    ex = jnp.exp(logits - mx)
    aff_ref[...] = ex / jnp.sum(ex, axis=0, keepdims=True)


def _outproj(r_out, h_raw, proj, hgrn_g, w_out_bf, h, g2, wr_t_bf, tm):
    B, S, D = h.shape
    E = wr_t_bf.shape[0]
    return pl.pallas_call(
        _outproj_kernel,
        grid=(B, S // tm),
        in_specs=[
            pl.BlockSpec((None, tm, GROUP), lambda b, i: (b, i, 0)),
            pl.BlockSpec((None, tm, GROUP), lambda b, i: (b, i, 0)),
            pl.BlockSpec((None, tm, GROUP), lambda b, i: (b, i, N_GROUPS - 1)),
            pl.BlockSpec((1, GROUP), lambda b, i: (0, 0)),
            pl.BlockSpec((2 * GROUP, D), lambda b, i: (0, 0)),
            pl.BlockSpec((None, tm, D), lambda b, i: (b, i, 0)),
            pl.BlockSpec((1, D), lambda b, i: (0, 0)),
            pl.BlockSpec((E, D), lambda b, i: (0, 0)),
        ],
        out_specs=[
            pl.BlockSpec((None, tm, D), lambda b, i: (b, i, 0)),
            pl.BlockSpec((None, tm, D), lambda b, i: (b, i, 0)),
            pl.BlockSpec((None, E, tm), lambda b, i: (b, 0, i)),
        ],
        out_shape=[jax.ShapeDtypeStruct((B, S, D), F32),
                   jax.ShapeDtypeStruct((B, S, D), BF16),
                   jax.ShapeDtypeStruct((B, E, S), F32)],
        compiler_params=_cparams(2),
        name="outproj",
    )(r_out, h_raw, proj, hgrn_g.reshape(1, GROUP), w_out_bf, h, g2.reshape(1, D), wr_t_bf)


def _select_kernel(aff_ref, pos_ref, gate_ref, *, cap):
    a = aff_ref[...]
    E, S = a.shape
    u = pltpu.bitcast(a, jnp.int32)
    capf = jnp.float32(cap)

    def count(mask):
        return jnp.sum(jnp.where(mask, 1.0, 0.0), axis=-1, keepdims=True)

    def value_bit(i, thr):
        cand = thr | (jnp.int32(1) << (30 - i))
        return jnp.where(count(u >= cand) >= capf, cand, thr)

    thr = lax.fori_loop(0, 31, value_bit, jnp.zeros((E, 1), jnp.int32))
    gt = u > thr
    eq = u == thr
    need = capf - count(gt)
    idx = lax.broadcasted_iota(jnp.int32, (E, S), 1)
    nbits = int(S).bit_length()

    def index_bit(i, cut):
        cand = cut | (jnp.int32(1) << (nbits - 1 - i))
        return jnp.where(count(eq & (idx < cand)) <= need, cand, cut)

    cut = lax.fori_loop(0, nbits, index_bit, jnp.zeros((E, 1), jnp.int32))
    sel = gt | (eq & (idx < cut))
    gate_ref[...] = jnp.where(sel, a, 0.0)

    li = lax.broadcasted_iota(jnp.int32, (128, 128), 0)
    lj = lax.broadcasted_iota(jnp.int32, (128, 128), 1)
    upper = jnp.where(li <= lj, 1.0, 0.0).astype(BF16)
    carry = jnp.zeros((E, 1), F32)
    for j in range(S // 128):
        sb = jnp.where(sel[:, j * 128:(j + 1) * 128], 1.0, 0.0)
        incl = _dot(sb.astype(BF16), upper)
        pos_ref[:, j * 128:(j + 1) * 128] = (incl - sb + carry).astype(jnp.int32)
        carry = carry + incl[:, 127:128]


def _select(aff, cap):
    B, E, S = aff.shape
    spec = pl.BlockSpec((None, E, S), lambda b: (b, 0, 0))
    return pl.pallas_call(
        functools.partial(_select_kernel, cap=cap),
        grid=(B,),
        in_specs=[spec],
        out_specs=[spec, spec],
        out_shape=[jax.ShapeDtypeStruct((B, E, S), jnp.int32),
                   jax.ShapeDtypeStruct((B, E, S), F32)],
        compiler_params=_cparams(1),
        name="select",
    )(aff)


def _gather_kernel(bs_ref, x_ref, pos_ref, gate_ref, xe_ref, oh_ref,
                   *, n_tb, tk, wn, cap, n_exp, eg):
    b = pl.program_id(0)
    grp = pl.program_id(1)
    tb = pl.program_id(2)
    lane_slot = lax.broadcasted_iota(jnp.int32, (wn, tk), 0)

    @pl.when(tb == 0)
    def _():
        xe_ref[...] = jnp.zeros_like(xe_ref)

    def onehot(el, start, first):
        key = jnp.where(gate_ref[el:el + 1, :] > 0.0, pos_ref[el:el + 1, :], -1)
        slot = start + lane_slot
        hit = jnp.logical_and(key == slot, slot >= first)
        return jnp.where(hit, 1.0, 0.0).astype(BF16)

    def window_start(first):
        return pl.multiple_of(jnp.minimum(first, cap - wn), 16)

    spans = []
    for el in range(eg):
        base = (b * n_exp + grp * eg + el) * (n_tb + 1)
        first = (bs_ref[base + tb] // 16) * 16
        start = window_start(first)
        spans.append((first, start, bs_ref[base + tb + 1]))
        oh_ref[el * wn:(el + 1) * wn, :] = onehot(el, start, first)
    rows = _dot(oh_ref[...], x_ref[...]).astype(BF16)
    for el in range(eg):
        xe_ref[el, pl.ds(spans[el][1], wn), :] += rows[el * wn:(el + 1) * wn, :]

    for el in range(eg):
        first0, _, end = spans[el]
        n_more = jnp.maximum(end - first0 - 1, 0) // wn

        def window(w, carry):
            first = first0 + (w + 1) * wn
            start = window_start(first)
            xe_ref[el, pl.ds(start, wn), :] += _dot(onehot(el, start, first), x_ref[...]).astype(BF16)
            return carry

        lax.fori_loop(0, n_more, window, 0)


def _moe_gather(bs, xn, pos_t, gate_t, cap, tk, wn, eg):
    B, S, D = xn.shape
    E = pos_t.shape[2]
    n_tb = S // tk
    grid_spec = pltpu.PrefetchScalarGridSpec(
        num_scalar_prefetch=1,
        grid=(B, E // eg, n_tb),
        in_specs=[
            pl.BlockSpec((None, tk, D), lambda b, g, t, bs: (b, t, 0)),
            pl.BlockSpec((None, None, eg, tk), lambda b, g, t, bs: (b, t, g, 0)),
            pl.BlockSpec((None, None, eg, tk), lambda b, g, t, bs: (b, t, g, 0)),
        ],
        out_specs=pl.BlockSpec((None, eg, cap, D), lambda b, g, t, bs: (b, g, 0, 0)),
        scratch_shapes=[pltpu.VMEM((eg * wn, tk), BF16)],
    )
    return pl.pallas_call(
        functools.partial(_gather_kernel, n_tb=n_tb, tk=tk, wn=wn, cap=cap, n_exp=E, eg=eg),
        grid_spec=grid_spec,
        out_shape=jax.ShapeDtypeStruct((B, E, cap, D), BF16),
        compiler_params=_cparams(3),
        name="moe_gather",
    )(bs, xn, pos_t, gate_t)


def _ffn_kernel(xe_ref, wg_ref, wu_ref, wd_ref, ye_ref, *, fm):
    for r in range(xe_ref.shape[0] // fm):
        xe = xe_ref[r * fm:(r + 1) * fm, :]
        hid = (_silu(_dot(xe, wg_ref[...])) * _dot(xe, wu_ref[...])).astype(BF16)
        ye_ref[r * fm:(r + 1) * fm, :] = _dot(hid, wd_ref[...]).astype(BF16)


def _moe_ffn(xe, wg, wu, wd, fm):
    B, E, cap, D = xe.shape
    FF = wg.shape[-1]
    rows = pl.BlockSpec((None, None, cap, D), lambda e, b: (b, e, 0, 0))
    return pl.pallas_call(
        functools.partial(_ffn_kernel, fm=fm),
        grid=(E, B),
        in_specs=[rows,
                  pl.BlockSpec((None, D, FF), lambda e, b: (e, 0, 0)),
                  pl.BlockSpec((None, D, FF), lambda e, b: (e, 0, 0)),
                  pl.BlockSpec((None, FF, D), lambda e, b: (e, 0, 0))],
        out_specs=rows,
        out_shape=jax.ShapeDtypeStruct((B, E, cap, D), BF16),
        compiler_params=_cparams(2),
        name="moe_ffn",
    )(xe, wg, wu, wd)


def _combine_kernel(bs_ref, h1_ref, ye_ref, pos_ref, gate_ref, fg_ref, out_ref, w_ref, y_ref,
                    *, n_tb, tk, wn, cap, n_exp, final_norm):
    b = pl.program_id(0)
    tb = pl.program_id(1)
    lane_slot = lax.broadcasted_iota(jnp.int32, (wn, tk), 0)

    def window_start(first):
        return pl.multiple_of(jnp.minimum(first, cap - wn), 16)

    def weights(e, start, first):
        slot = start + lane_slot
        hit = jnp.logical_and(pos_ref[e:e + 1, :] == slot, slot >= first)
        return jnp.where(hit, gate_ref[e:e + 1, :], 0.0).astype(BF16)

    firsts = []
    for e in range(n_exp):
        base = (b * n_exp + e) * (n_tb + 1)
        first = (bs_ref[base + tb] // 16) * 16
        start = window_start(first)
        firsts.append((first, bs_ref[base + tb + 1]))
        w_ref[e * wn:(e + 1) * wn, :] = weights(e, start, first)
        y_ref[e * wn:(e + 1) * wn, :] = ye_ref[e, pl.ds(start, wn), :]
    out_ref[...] = h1_ref[...] + _dot_tn(w_ref[...], y_ref[...])

    for e in range(n_exp):
        first0, end = firsts[e]
        n_more = jnp.maximum(end - first0 - 1, 0) // wn

        def window(w, carry):
            first = first0 + (w + 1) * wn
            start = window_start(first)
            out_ref[...] += _dot_tn(weights(e, start, first), ye_ref[e, pl.ds(start, wn), :])
            return carry

        lax.fori_loop(0, n_more, window, 0)
    if final_norm:
        out_ref[...] = _rms(out_ref[...]) * fg_ref[...]


def _combine(bs, h1, ye, pos_t, gate_t, final_g, cap, tk, wn, final_norm):
    B, S, D = h1.shape
    E = ye.shape[1]
    n_tb = S // tk
    grid_spec = pltpu.PrefetchScalarGridSpec(
        num_scalar_prefetch=1,
        grid=(B, n_tb),
        in_specs=[
            pl.BlockSpec((None, tk, D), lambda b, t, bs: (b, t, 0)),
            pl.BlockSpec((None, E, cap, D), lambda b, t, bs: (b, 0, 0, 0), pipeline_mode=pl.Buffered(1)),
            pl.BlockSpec((None, None, E, tk), lambda b, t, bs: (b, t, 0, 0)),
            pl.BlockSpec((None, None, E, tk), lambda b, t, bs: (b, t, 0, 0)),
            pl.BlockSpec((1, D), lambda b, t, bs: (0, 0)),
        ],
        out_specs=pl.BlockSpec((None, tk, D), lambda b, t, bs: (b, t, 0)),
        scratch_shapes=[pltpu.VMEM((E * wn, tk), BF16), pltpu.VMEM((E * wn, D), BF16)],
    )
    return pl.pallas_call(
        functools.partial(_combine_kernel, n_tb=n_tb, tk=tk, wn=wn, cap=cap, n_exp=E,
                          final_norm=final_norm),
        grid_spec=grid_spec,
        out_shape=jax.ShapeDtypeStruct((B, S, D), F32),
        compiler_params=_cparams(2),
        name="combine",
    )(bs, h1, ye, pos_t, gate_t, final_g.reshape(1, D))


def _rope_tables(S):
    half = HEAD_DIM // 2
    inv_freq = ROPE_BASE ** (-jnp.arange(half, dtype=F32) / half)
    ang = jnp.arange(S).astype(F32)[:, None] * inv_freq[None, :]
    cos, sin = jnp.cos(ang), jnp.sin(ang)
    return jnp.concatenate([cos, cos], axis=-1), jnp.concatenate([-sin, sin], axis=-1)


def _block_starts(pos, tk, cap):
    B, E, _ = pos.shape
    bs = jnp.concatenate([pos[:, :, ::tk], jnp.full((B, E, 1), cap, jnp.int32)], axis=-1)
    return bs.reshape(-1)


def kernel(x, norm1_g, w_in, ret_norm_g, hgrn_norm_g, w_out, lower_bounds, norm2_g, w_router,
           w_gate, w_up, w_down, final_norm_g):
    B, S, D = x.shape
    depth = w_in.shape[0]
    E = w_router.shape[-1]
    cap = CAPACITY_FACTOR * S // E
    tm = min(512, S)
    tk = min(512, S)
    wn = min(128, cap)
    fm = min(512, cap)
    eg = min(8, E)
    n_tb = S // tk

    lbs = jax.nn.softmax(lower_bounds.astype(F32), axis=0)
    lbs = jnp.cumsum(lbs, axis=0) - lbs[0]
    cos, sin = _rope_tables(S)

    h = x
    for layer in range(depth):
        proj = _inproj(h, norm1_g[layer], w_in[layer].astype(BF16), cos, sin, tm)
        r_out = _retention(proj, ret_norm_g[layer])
        h_raw = _hgrn(proj, lbs[layer], first_layer=(layer == 0))
        h1, xn, aff = _outproj(r_out, h_raw, proj, hgrn_norm_g[layer], w_out[layer].astype(BF16),
                               h, norm2_g[layer], w_router[layer].T.astype(BF16), tm)
        pos, gate = _select(aff, cap)
        bs = _block_starts(pos, tk, cap)
        pos_t = pos.reshape(B, E, n_tb, tk).transpose(0, 2, 1, 3)
        gate_t = gate.reshape(B, E, n_tb, tk).transpose(0, 2, 1, 3)
        xe = _moe_gather(bs, xn, pos_t, gate_t, cap, tk, wn, eg)
        ye = _moe_ffn(xe, w_gate[layer].astype(BF16), w_up[layer].astype(BF16),
                      w_down[layer].astype(BF16), fm)
        h = _combine(bs, h1, ye, pos_t, gate_t, final_norm_g, cap, tk, wn,
                     final_norm=(layer == depth - 1))
    return h
```

```python
import functools

import numpy as np
import jax
import jax.numpy as jnp
from jax import lax
from jax.experimental import pallas as pl
from jax.experimental.pallas import tpu as pltpu

F32 = jnp.float32
BF16 = jnp.bfloat16

HEAD_DIM = 128
N_HEADS = 4
GROUP = N_HEADS * HEAD_DIM
N_GROUPS = 9
ROPE_BASE = 10000.0
NORM_EPS = 1e-6
CAPACITY_FACTOR = 2

RET_CHUNK = 256
RET_PAIR = 4
HGRN_CHUNK = 128
HGRN_PAIR = 4
HGRN_DESC = 16
LOG2E = 1.4426950408889634
VMEM_LIMIT = 58 * 1024 * 1024


def _cparams(n_axes):
    return pltpu.CompilerParams(
        dimension_semantics=("arbitrary",) * n_axes, vmem_limit_bytes=VMEM_LIMIT)


def _dot(a, b):
    return jnp.dot(a, b, preferred_element_type=F32)


def _dot_nt(a, b):
    return lax.dot_general(a, b, (((1,), (1,)), ((), ())), preferred_element_type=F32)


def _dot_tn(a, b):
    return lax.dot_general(a, b, (((0,), (0,)), ((), ())), preferred_element_type=F32)


def _silu(x):
    return x * (1.0 / (1.0 + jnp.exp(-x)))


def _rms(x):
    return x * lax.rsqrt(jnp.mean(x * x, axis=-1, keepdims=True) + NORM_EPS)


def _inproj_kernel(h_ref, g_ref, wf_ref, cos_ref, sin_ref, out_ref, w_ref):
    @pl.when(jnp.logical_and(pl.program_id(0) == 0, pl.program_id(1) == 0))
    def _():
        w_ref[...] = wf_ref[...].astype(BF16)

    xn = (_rms(h_ref[...]) * g_ref[...]).astype(BF16)
    cos = cos_ref[...]
    sin = sin_ref[...]
    scale = HEAD_DIM ** -0.5
    for j in range(N_GROUPS):
        acc = _dot(xn, w_ref[:, j * GROUP:(j + 1) * GROUP])
        if j in (0, 1):
            for hh in range(N_HEADS):
                sl = acc[:, hh * HEAD_DIM:(hh + 1) * HEAD_DIM]
                rot = sl * cos + pltpu.roll(sl, HEAD_DIM // 2, 1) * sin
                if j == 1:
                    rot = rot * scale
                out_ref[:, j * GROUP + hh * HEAD_DIM:j * GROUP + (hh + 1) * HEAD_DIM] = rot.astype(BF16)
            continue
        if j in (3, 8):
            acc = _silu(acc)
        elif j == 4:
            acc = _silu(acc) * scale
        out_ref[:, j * GROUP:(j + 1) * GROUP] = acc.astype(BF16)


def _inproj(h, g, w_in, layer, cos, sin, tm):
    B, S, D = h.shape
    ncol = w_in.shape[-1]
    return pl.pallas_call(
        _inproj_kernel,
        grid=(B, S // tm),
        in_specs=[
            pl.BlockSpec((None, tm, D), lambda b, i: (b, i, 0)),
            pl.BlockSpec((1, D), lambda b, i: (0, 0)),
            pl.BlockSpec((None, D, ncol), lambda b, i: (layer, 0, 0), pipeline_mode=pl.Buffered(1)),
            pl.BlockSpec((tm, HEAD_DIM), lambda b, i: (i, 0)),
            pl.BlockSpec((tm, HEAD_DIM), lambda b, i: (i, 0)),
        ],
        out_specs=pl.BlockSpec((None, tm, ncol), lambda b, i: (b, i, 0)),
        out_shape=jax.ShapeDtypeStruct((B, S, ncol), BF16),
        scratch_shapes=[pltpu.VMEM((D, ncol), BF16)],
        compiler_params=_cparams(2),
        name="inproj",
    )(h, g.reshape(1, D), w_in, cos, sin)


def _ret_kernel(q_ref, k_ref, v_ref, g_ref, dmat_ref, qdec_ref, kdec_ref, cdec_ref, rg_ref,
                out_ref, acc_ref, *, n_chunks):
    L = RET_CHUNK
    cdec = cdec_ref[0:1, :]
    zero_state = jnp.zeros((HEAD_DIM, HEAD_DIM), F32)

    def rows(i):
        return pl.ds(pl.multiple_of(i * L, L), L)

    def finish(sl, o):
        y = _rms(o) * rg_ref[...] * g_ref[sl, :].astype(F32)
        out_ref[sl, :] = y.astype(BF16)

    def step(i, states, second):
        state_f, state_b = states
        sfs = [rows(i * RET_PAIR + j) for j in range(RET_PAIR)]
        sbs = [rows(n_chunks - 1 - i * RET_PAIR - j) for j in range(RET_PAIR)]
        qf, kf, vf = ([r[s, :] for s in sfs] for r in (q_ref, k_ref, v_ref))
        qb, kb, vb = ([r[s, :] for s in sbs] for r in (q_ref, k_ref, v_ref))
        raw = [_dot_nt(q, k) for q, k in zip(qf, kf)]
        qdf = [(q.astype(F32) * qdec_ref[...]).astype(BF16) for q in qf]
        kdf = [(k.astype(F32) * kdec_ref[...]).astype(BF16) for k in kf]
        qdb = [(q.astype(F32) * kdec_ref[...]).astype(BF16) for q in qb]
        kdb = [(k.astype(F32) * qdec_ref[...]).astype(BF16) for k in kb]
        upd_f = [_dot_tn(k, v) for k, v in zip(kdf, vf)]
        upd_b = [_dot_tn(k, v) for k, v in zip(kdb, vb)]
        intra = [_dot((s * dmat_ref[...]).astype(BF16), v) for s, v in zip(raw, vf)]
        for j in range(RET_PAIR):
            of = intra[j] + _dot(qdf[j], state_f.astype(BF16))
            ob = _dot(qdb[j], state_b.astype(BF16))
            state_f = cdec * state_f + upd_f[j]
            state_b = cdec * state_b + upd_b[j]
            if second:
                finish(sfs[j], acc_ref[sfs[j], :] + of)
                finish(sbs[j], acc_ref[sbs[j], :] + ob)
            else:
                acc_ref[sfs[j], :] = of
                acc_ref[sbs[j], :] = ob
        return state_f, state_b

    steps = n_chunks // RET_PAIR
    states = lax.fori_loop(0, steps // 2, functools.partial(step, second=False),
                           (zero_state, zero_state))
    lax.fori_loop(steps // 2, steps, functools.partial(step, second=True), states)


def _ret_tables():
    L = RET_CHUNK
    t = np.arange(L, dtype=np.float64)
    lg = np.log1p(-(2.0 ** (-5.0 - np.arange(N_HEADS, dtype=np.float64))))
    dmat = np.exp(lg[:, None, None] * np.abs(t[:, None] - t[None, :])[None])
    qdec = np.exp(lg[:, None] * (t + 1.0))[:, :, None] * np.ones((1, 1, HEAD_DIM))
    kdec = np.exp(lg[:, None] * (L - 1.0 - t))[:, :, None] * np.ones((1, 1, HEAD_DIM))
    cdec = np.exp(lg * L)[:, None, None] * np.ones((1, 8, HEAD_DIM))
    return tuple(jnp.asarray(a, F32) for a in (dmat, qdec, kdec, cdec))


def _retention(proj, ret_g):
    B, S, _ = proj.shape
    L = RET_CHUNK
    assert S % (2 * RET_PAIR * L) == 0, "retention walks chunks from both ends, RET_PAIR at a time"
    dmat, qdec, kdec, cdec = _ret_tables()

    def col(c0):
        return pl.BlockSpec((None, S, HEAD_DIM), lambda b, h: (b, 0, c0 + h))

    def tab(r):
        return pl.BlockSpec((None, r, HEAD_DIM), lambda b, h: (h, 0, 0))

    return pl.pallas_call(
        functools.partial(_ret_kernel, n_chunks=S // L),
        grid=(B, N_HEADS),
        in_specs=[col(0), col(N_HEADS), col(2 * N_HEADS), col(3 * N_HEADS),
                  pl.BlockSpec((None, L, L), lambda b, h: (h, 0, 0)),
                  tab(L), tab(L), tab(8),
                  pl.BlockSpec((1, HEAD_DIM), lambda b, h: (0, h))],
        out_specs=pl.BlockSpec((None, S, HEAD_DIM), lambda b, h: (b, 0, h)),
        out_shape=jax.ShapeDtypeStruct((B, S, GROUP), BF16),
        scratch_shapes=[pltpu.VMEM((S, HEAD_DIM), F32)],
        compiler_params=_cparams(2),
        name="retention",
    )(proj, proj, proj, proj, dmat, qdec, kdec, cdec, ret_g.reshape(1, GROUP))


def _hgrn_gates(z, lb, first_layer):
    zf = z.astype(F32)
    e = jnp.exp(-jnp.abs(zf))
    inv = 1.0 / (1.0 + e)
    pos = zf >= 0.0
    sigm = jnp.where(pos, e * inv, inv)
    if first_layer:
        return jnp.minimum(zf, 0.0) * LOG2E - jnp.log2(1.0 + e), sigm
    sig = jnp.where(pos, inv, e * inv)
    return jnp.log2(lb + (1.0 - lb) * sig), (1.0 - lb) * sigm


def _hgrn_intra(chunks, lb, mtab_f, mtab_b, lvl, first_layer):
    L = HGRN_CHUNK
    n = len(chunks)
    qfs = [c[0].astype(F32) for c in chunks]
    gates_f = [_hgrn_gates(c[1], lb, first_layer) for c in chunks]
    gates_b = [_hgrn_gates(c[2], lb, first_layer) for c in chunks]

    def exponents(gates, mtab):
        splits = []
        for logf2, _ in gates:
            hi = logf2.astype(BF16)
            splits.append(jnp.concatenate([hi, (logf2 - hi.astype(F32)).astype(BF16)], axis=0))
        decs = []
        for a in range(0, n, 2):
            both = _dot(mtab, jnp.concatenate(splits[a:a + 2], axis=1))
            decs += [both[:, j * HEAD_DIM:(j + 1) * HEAD_DIM] for j in range(len(splits[a:a + 2]))]
        return decs

    decs_f = exponents(gates_f, mtab_f)
    decs_b = exponents(gates_b, mtab_b)
    cfs = [d[0:L, :] for d in decs_f]
    cbs = [d[0:L, :] for d in decs_b]
    kfs = [g[1] for g in gates_f]
    kbs = [g[1] for g in gates_b]

    scores = [jnp.where(lvl == -1, _dot_nt(chunks[i][0], (kfs[i] + kbs[i]).astype(BF16)), 0.0)
              for i in range(n)]
    m, level = 1, 0
    while m < L:
        for i in range(n):
            if m < 8:
                def as3(x):
                    return x.reshape(L // 8, 8, HEAD_DIM)
                pf = as3(jnp.exp2(decs_f[i][(1 + level) * L:(2 + level) * L, :]))
                pb = as3(jnp.exp2(decs_b[i][(1 + level) * L:(2 + level) * L, :]))
                sub = lax.broadcasted_iota(jnp.int32, (1, 8, HEAD_DIM), 1)
                upper = ((sub >> level) & 1) == 1
                qside = (as3(qfs[i]) * jnp.where(upper, pf, pb)).reshape(L, HEAD_DIM)
                kside = jnp.where(upper, as3(kbs[i]) * pb, as3(kfs[i]) * pf).reshape(L, HEAD_DIM)
            else:
                qslabs, kslabs = [], []
                for j in range(L // m):
                    mid = (j // 2) * 2 * m + m
                    sl = slice(j * m, (j + 1) * m)
                    if j % 2 == 1:
                        qdec = jnp.exp2(cfs[i][sl, :] - cfs[i][mid - 1:mid, :])
                        kslabs.append(kbs[i][sl, :] * jnp.exp2(cbs[i][mid:mid + 1, :] - cbs[i][sl, :]))
                    else:
                        qdec = jnp.exp2(cbs[i][sl, :] - cbs[i][mid:mid + 1, :])
                        kslabs.append(kfs[i][sl, :] * jnp.exp2(cfs[i][mid - 1:mid, :] - cfs[i][sl, :]))
                    qslabs.append(qfs[i][sl, :] * qdec)
                qside = jnp.concatenate(qslabs, axis=0)
                kside = jnp.concatenate(kslabs, axis=0)
            s = _dot_nt(qside.astype(BF16), kside.astype(BF16))
            scores[i] = jnp.where(lvl == level, s, scores[i])
        m, level = 2 * m, level + 1

    results = []
    for i in range(n):
        intra = _dot(scores[i].astype(BF16), chunks[i][3])
        cf_end = cfs[i][L - 1:L, :]
        cb_end = cbs[i][0:1, :]
        fwd = ((qfs[i] * jnp.exp2(cfs[i])).astype(BF16),
               (kfs[i] * jnp.exp2(cf_end - cfs[i])).astype(BF16), jnp.exp2(cf_end))
        bwd = ((qfs[i] * jnp.exp2(cbs[i])).astype(BF16),
               (kbs[i] * jnp.exp2(cb_end - cbs[i])).astype(BF16), jnp.exp2(cb_end))
        results.append((intra, fwd, bwd))
    return results


def _hgrn_kernel(q_ref, zf_ref, zb_ref, v_ref, lb_ref, mtabf_ref, mtabb_ref, lvl_ref,
                 out_ref, qeb_ref, keb_ref, dend_ref, *, n_chunks, first_layer):
    L = HGRN_CHUNK
    lb = lb_ref[...]
    zero_state = jnp.zeros((HEAD_DIM, HEAD_DIM), F32)

    def rows(c):
        return pl.ds(pl.multiple_of(c * L, L), L)

    def ascend(i, state):
        cs = [i * HGRN_PAIR + j for j in range(HGRN_PAIR)]
        sls = [rows(c) for c in cs]
        res = _hgrn_intra([(q_ref[sl, :], zf_ref[sl, :], zb_ref[sl, :], v_ref[sl, :]) for sl in sls],
                          lb, mtabf_ref[...], mtabb_ref[...], lvl_ref[...], first_layer)
        updates = [_dot_tn(v_ref[sl, :], r[1][1]) for sl, r in zip(sls, res)]
        for c, sl, (intra, (qe_f, _, dend_f), (qe_b, ke_b, dend_b)), upd in zip(cs, sls, res, updates):
            out_ref[sl, :] = intra + _dot_nt(qe_f, state.astype(BF16))
            state = dend_f * state + upd
            qeb_ref[sl, :] = qe_b
            keb_ref[sl, :] = ke_b
            dend_ref[c] = jnp.broadcast_to(dend_b, (8, HEAD_DIM))
        return state

    lax.fori_loop(0, n_chunks // HGRN_PAIR, ascend, zero_state)

    def descend(i, state):
        cs = [n_chunks - 1 - i * HGRN_DESC - j for j in range(HGRN_DESC)]
        sls = [rows(c) for c in cs]
        updates = [_dot_tn(v_ref[sl, :], keb_ref[sl, :]) for sl in sls]
        for c, sl, upd in zip(cs, sls, updates):
            out_ref[sl, :] += _dot_nt(qeb_ref[sl, :], state.astype(BF16))
            state = dend_ref[c][0:1, :] * state + upd
        return state

    lax.fori_loop(0, n_chunks // HGRN_DESC, descend, zero_state)


def _hgrn_tables():
    L = HGRN_CHUNK
    t = np.arange(L)[:, None]
    u = np.arange(L)[None, :]
    x = t ^ u
    lvl = np.where(x == 0, -1, np.floor(np.log2(np.maximum(x, 1)))).astype(np.int32)

    def exponent_rows(backward):
        blocks = [(u >= t) if backward else (u <= t)]
        m = 1
        while m < 8:
            mid = (t // (2 * m)) * (2 * m) + m
            if backward:
                blocks.append(np.where(t < mid, (u >= t) & (u < mid), (u >= mid) & (u < t)))
            else:
                blocks.append(np.where(t >= mid, (u >= mid) & (u <= t), (u > t) & (u < mid)))
            m *= 2
        tab = np.concatenate(blocks, axis=0).astype(np.float32)
        return np.concatenate([tab, tab], axis=1)

    return (jnp.asarray(exponent_rows(False), BF16), jnp.asarray(exponent_rows(True), BF16),
            jnp.asarray(lvl, jnp.int32))


def _hgrn(proj, lb, first_layer):
    B, S, _ = proj.shape
    L = HGRN_CHUNK
    n_chunks = S // L
    assert S % L == 0 and n_chunks % HGRN_PAIR == 0 and n_chunks % HGRN_DESC == 0
    mtab_f, mtab_b, lvl = _hgrn_tables()

    def col(g):
        return pl.BlockSpec((None, S, HEAD_DIM), lambda b, h: (b, 0, g * N_HEADS + h))

    def full(a):
        return pl.BlockSpec(a.shape, lambda b, h: (0, 0))

    return pl.pallas_call(
        functools.partial(_hgrn_kernel, n_chunks=n_chunks, first_layer=first_layer),
        grid=(B, N_HEADS),
        in_specs=[col(4), col(5), col(6), col(7),
                  pl.BlockSpec((1, HEAD_DIM), lambda b, h: (0, h)),
                  full(mtab_f), full(mtab_b), full(lvl)],
        out_specs=pl.BlockSpec((None, S, HEAD_DIM), lambda b, h: (b, 0, h)),
        out_shape=jax.ShapeDtypeStruct((B, S, GROUP), F32),
        scratch_shapes=[pltpu.VMEM((S, HEAD_DIM), BF16), pltpu.VMEM((S, HEAD_DIM), BF16),
                        pltpu.VMEM((n_chunks, 8, HEAD_DIM), F32)],
        compiler_params=_cparams(2),
        name="hgrn",
    )(proj, proj, proj, proj, lb.reshape(1, GROUP), mtab_f, mtab_b, lvl)


def _outproj_kernel(r_ref, hraw_ref, hg_ref, hgn_ref, wf_ref, h_ref, g2_ref, wr_ref,
                    h1_ref, xn_ref, aff_ref, w_ref):
    @pl.when(jnp.logical_and(pl.program_id(0) == 0, pl.program_id(1) == 0))
    def _():
        w_ref[...] = wf_ref[...].astype(BF16)

    hn = _rms(hraw_ref[...]) * hgn_ref[...] * hg_ref[...].astype(F32)
    mix = _dot(r_ref[...], w_ref[0:GROUP, :]) + _dot(hn.astype(BF16), w_ref[GROUP:2 * GROUP, :])
    h1 = h_ref[...] + mix
    h1_ref[...] = h1
    xn = (_rms(h1) * g2_ref[...]).astype(BF16)
    xn_ref[...] = xn
    logits = _dot_nt(wr_ref[...], xn)
    mx = jnp.max(logits, axis=0, keepdims=True)
    ex = jnp.exp(logits - mx)
    aff_ref[...] = ex / jnp.sum(ex, axis=0, keepdims=True)


def _outproj(r_out, h_raw, proj, hgrn_g, w_out, layer, h, g2, wr_t_bf, tm):
    B, S, D = h.shape
    E = wr_t_bf.shape[0]
    return pl.pallas_call(
        _outproj_kernel,
        grid=(B, S // tm),
        in_specs=[
            pl.BlockSpec((None, tm, GROUP), lambda b, i: (b, i, 0)),
            pl.BlockSpec((None, tm, GROUP), lambda b, i: (b, i, 0)),
            pl.BlockSpec((None, tm, GROUP), lambda b, i: (b, i, N_GROUPS - 1)),
            pl.BlockSpec((1, GROUP), lambda b, i: (0, 0)),
            pl.BlockSpec((None, 2 * GROUP, D), lambda b, i: (layer, 0, 0), pipeline_mode=pl.Buffered(1)),
            pl.BlockSpec((None, tm, D), lambda b, i: (b, i, 0)),
            pl.BlockSpec((1, D), lambda b, i: (0, 0)),
            pl.BlockSpec((E, D), lambda b, i: (0, 0)),
        ],
        out_specs=[
            pl.BlockSpec((None, tm, D), lambda b, i: (b, i, 0)),
            pl.BlockSpec((None, tm, D), lambda b, i: (b, i, 0)),
            pl.BlockSpec((None, E, tm), lambda b, i: (b, 0, i)),
        ],
        out_shape=[jax.ShapeDtypeStruct((B, S, D), F32),
                   jax.ShapeDtypeStruct((B, S, D), BF16),
                   jax.ShapeDtypeStruct((B, E, S), F32)],
        scratch_shapes=[pltpu.VMEM((2 * GROUP, D), BF16)],
        compiler_params=_cparams(2),
        name="outproj",
    )(r_out, h_raw, proj, hgrn_g.reshape(1, GROUP), w_out, h, g2.reshape(1, D), wr_t_bf)


def _select_kernel(aff_ref, pos_ref, gate_ref, *, cap):
    a = aff_ref[...]
    E, S = a.shape
    u = pltpu.bitcast(a, jnp.int32)
    capf = jnp.float32(cap)

    def count(mask):
        return jnp.sum(jnp.where(mask, 1.0, 0.0), axis=-1, keepdims=True)

    def value_bit(i, thr):
        cand = thr | (jnp.int32(1) << (30 - i))
        return jnp.where(count(u >= cand) >= capf, cand, thr)

    thr = lax.fori_loop(0, 31, value_bit, jnp.zeros((E, 1), jnp.int32))
    gt = u > thr
    eq = u == thr
    need = capf - count(gt)
    idx = lax.broadcasted_iota(jnp.int32, (E, S), 1)
    nbits = int(S).bit_length()

    def index_bit(i, cut):
        cand = cut | (jnp.int32(1) << (nbits - 1 - i))
        return jnp.where(count(eq & (idx < cand)) <= need, cand, cut)

    cut = lax.fori_loop(0, nbits, index_bit, jnp.zeros((E, 1), jnp.int32))
    sel = gt | (eq & (idx < cut))
    gate_ref[...] = jnp.where(sel, a, 0.0)

    li = lax.broadcasted_iota(jnp.int32, (128, 128), 0)
    lj = lax.broadcasted_iota(jnp.int32, (128, 128), 1)
    upper = jnp.where(li <= lj, 1.0, 0.0).astype(BF16)
    carry = jnp.zeros((E, 1), F32)
    for j in range(S // 128):
        sb = jnp.where(sel[:, j * 128:(j + 1) * 128], 1.0, 0.0)
        incl = _dot(sb.astype(BF16), upper)
        pos_ref[:, j * 128:(j + 1) * 128] = (incl - sb + carry).astype(jnp.int32)
        carry = carry + incl[:, 127:128]


def _select(aff, cap):
    B, E, S = aff.shape
    spec = pl.BlockSpec((None, E, S), lambda b: (b, 0, 0))
    return pl.pallas_call(
        functools.partial(_select_kernel, cap=cap),
        grid=(B,),
        in_specs=[spec],
        out_specs=[spec, spec],
        out_shape=[jax.ShapeDtypeStruct((B, E, S), jnp.int32),
                   jax.ShapeDtypeStruct((B, E, S), F32)],
        compiler_params=_cparams(1),
        name="select",
    )(aff)


def _gather_kernel(bs_ref, x_ref, pos_ref, gate_ref, xe_ref, oh_ref,
                   *, n_tb, tk, wn, cap, n_exp, eg):
    b = pl.program_id(0)
    grp = pl.program_id(1)
    tb = pl.program_id(2)
    lane_slot = lax.broadcasted_iota(jnp.int32, (wn, tk), 0)

    @pl.when(tb == 0)
    def _():
        xe_ref[...] = jnp.zeros_like(xe_ref)

    def onehot(el, start, first):
        key = jnp.where(gate_ref[el:el + 1, :] > 0.0, pos_ref[el:el + 1, :], -1)
        slot = start + lane_slot
        hit = jnp.logical_and(key == slot, slot >= first)
        return jnp.where(hit, 1.0, 0.0).astype(BF16)

    def window_start(first):
        return pl.multiple_of(jnp.minimum(first, cap - wn), 16)

    spans = []
    for el in range(eg):
        base = (b * n_exp + grp * eg + el) * (n_tb + 1)
        first = (bs_ref[base + tb] // 16) * 16
        start = window_start(first)
        spans.append((first, start, bs_ref[base + tb + 1]))
        oh_ref[el * wn:(el + 1) * wn, :] = onehot(el, start, first)
    rows = _dot(oh_ref[...], x_ref[...]).astype(BF16)
    for el in range(eg):
        xe_ref[el, pl.ds(spans[el][1], wn), :] += rows[el * wn:(el + 1) * wn, :]

    for el in range(eg):
        first0, _, end = spans[el]
        n_more = jnp.maximum(end - first0 - 1, 0) // wn

        def window(w, carry):
            first = first0 + (w + 1) * wn
            start = window_start(first)
            xe_ref[el, pl.ds(start, wn), :] += _dot(onehot(el, start, first), x_ref[...]).astype(BF16)
            return carry

        lax.fori_loop(0, n_more, window, 0)


def _moe_gather(bs, xn, pos_t, gate_t, cap, tk, wn, eg):
    B, S, D = xn.shape
    E = pos_t.shape[2]
    n_tb = S // tk
    grid_spec = pltpu.PrefetchScalarGridSpec(
        num_scalar_prefetch=1,
        grid=(B, E // eg, n_tb),
        in_specs=[
            pl.BlockSpec((None, tk, D), lambda b, g, t, bs: (b, t, 0)),
            pl.BlockSpec((None, None, eg, tk), lambda b, g, t, bs: (b, t, g, 0)),
            pl.BlockSpec((None, None, eg, tk), lambda b, g, t, bs: (b, t, g, 0)),
        ],
        out_specs=pl.BlockSpec((None, eg, cap, D), lambda b, g, t, bs: (b, g, 0, 0)),
        scratch_shapes=[pltpu.VMEM((eg * wn, tk), BF16)],
    )
    return pl.pallas_call(
        functools.partial(_gather_kernel, n_tb=n_tb, tk=tk, wn=wn, cap=cap, n_exp=E, eg=eg),
        grid_spec=grid_spec,
        out_shape=jax.ShapeDtypeStruct((B, E, cap, D), BF16),
        compiler_params=_cparams(3),
        name="moe_gather",
    )(bs, xn, pos_t, gate_t)


def _ffn_kernel(xe_ref, wg_ref, wu_ref, wd_ref, ye_ref, wg_bf, wu_bf, wd_bf, *, fm):
    @pl.when(pl.program_id(1) == 0)
    def _():
        wg_bf[...] = wg_ref[...].astype(BF16)
        wu_bf[...] = wu_ref[...].astype(BF16)
        wd_bf[...] = wd_ref[...].astype(BF16)

    for r in range(xe_ref.shape[0] // fm):
        xe = xe_ref[r * fm:(r + 1) * fm, :]
        hid = (_silu(_dot(xe, wg_bf[...])) * _dot(xe, wu_bf[...])).astype(BF16)
        ye_ref[r * fm:(r + 1) * fm, :] = _dot(hid, wd_bf[...]).astype(BF16)


def _moe_ffn(xe, w_gate, w_up, w_down, layer, fm):
    B, E, cap, D = xe.shape
    FF = w_gate.shape[-1]
    rows = pl.BlockSpec((None, None, cap, D), lambda e, b: (b, e, 0, 0))
    return pl.pallas_call(
        functools.partial(_ffn_kernel, fm=fm),
        grid=(E, B),
        in_specs=[rows,
                  pl.BlockSpec((None, None, D, FF), lambda e, b: (layer, e, 0, 0)),
                  pl.BlockSpec((None, None, D, FF), lambda e, b: (layer, e, 0, 0)),
                  pl.BlockSpec((None, None, FF, D), lambda e, b: (layer, e, 0, 0))],
        out_specs=rows,
        out_shape=jax.ShapeDtypeStruct((B, E, cap, D), BF16),
        scratch_shapes=[pltpu.VMEM((D, FF), BF16), pltpu.VMEM((D, FF), BF16), pltpu.VMEM((FF, D), BF16)],
        compiler_params=_cparams(2),
        name="moe_ffn",
    )(xe, w_gate, w_up, w_down)


def _combine_kernel(bs_ref, h1_ref, ye_ref, pos_ref, gate_ref, fg_ref, out_ref, w_ref, y_ref,
                    *, n_tb, tk, wn, cap, n_exp, final_norm):
    b = pl.program_id(0)
    tb = pl.program_id(1)
    lane_slot = lax.broadcasted_iota(jnp.int32, (wn, tk), 0)

    def window_start(first):
        return pl.multiple_of(jnp.minimum(first, cap - wn), 16)

    def weights(e, start, first):
        slot = start + lane_slot
        hit = jnp.logical_and(pos_ref[e:e + 1, :] == slot, slot >= first)
        return jnp.where(hit, gate_ref[e:e + 1, :], 0.0).astype(BF16)

    firsts = []
    for e in range(n_exp):
        base = (b * n_exp + e) * (n_tb + 1)
        first = (bs_ref[base + tb] // 16) * 16
        start = window_start(first)
        firsts.append((first, bs_ref[base + tb + 1]))
        w_ref[e * wn:(e + 1) * wn, :] = weights(e, start, first)
        y_ref[e * wn:(e + 1) * wn, :] = ye_ref[e, pl.ds(start, wn), :]
    out_ref[...] = h1_ref[...] + _dot_tn(w_ref[...], y_ref[...])

    for e in range(n_exp):
        first0, end = firsts[e]
        n_more = jnp.maximum(end - first0 - 1, 0) // wn

        def window(w, carry):
            first = first0 + (w + 1) * wn
            start = window_start(first)
            out_ref[...] += _dot_tn(weights(e, start, first), ye_ref[e, pl.ds(start, wn), :])
            return carry

        lax.fori_loop(0, n_more, window, 0)
    if final_norm:
        out_ref[...] = _rms(out_ref[...]) * fg_ref[...]


def _combine(bs, h1, ye, pos_t, gate_t, final_g, cap, tk, wn, final_norm):
    B, S, D = h1.shape
    E = ye.shape[1]
    n_tb = S // tk
    grid_spec = pltpu.PrefetchScalarGridSpec(
        num_scalar_prefetch=1,
        grid=(B, n_tb),
        in_specs=[
            pl.BlockSpec((None, tk, D), lambda b, t, bs: (b, t, 0)),
            pl.BlockSpec((None, E, cap, D), lambda b, t, bs: (b, 0, 0, 0), pipeline_mode=pl.Buffered(1)),
            pl.BlockSpec((None, None, E, tk), lambda b, t, bs: (b, t, 0, 0)),
            pl.BlockSpec((None, None, E, tk), lambda b, t, bs: (b, t, 0, 0)),
            pl.BlockSpec((1, D), lambda b, t, bs: (0, 0)),
        ],
        out_specs=pl.BlockSpec((None, tk, D), lambda b, t, bs: (b, t, 0)),
        scratch_shapes=[pltpu.VMEM((E * wn, tk), BF16), pltpu.VMEM((E * wn, D), BF16)],
    )
    return pl.pallas_call(
        functools.partial(_combine_kernel, n_tb=n_tb, tk=tk, wn=wn, cap=cap, n_exp=E,
                          final_norm=final_norm),
        grid_spec=grid_spec,
        out_shape=jax.ShapeDtypeStruct((B, S, D), F32),
        compiler_params=_cparams(2),
        name="combine",
    )(bs, h1, ye, pos_t, gate_t, final_g.reshape(1, D))


def _rope_tables(S):
    half = HEAD_DIM // 2
    inv_freq = ROPE_BASE ** (-jnp.arange(half, dtype=F32) / half)
    ang = jnp.arange(S).astype(F32)[:, None] * inv_freq[None, :]
    cos, sin = jnp.cos(ang), jnp.sin(ang)
    return jnp.concatenate([cos, cos], axis=-1), jnp.concatenate([-sin, sin], axis=-1)


def _block_starts(pos, tk, cap):
    B, E, _ = pos.shape
    bs = jnp.concatenate([pos[:, :, ::tk], jnp.full((B, E, 1), cap, jnp.int32)], axis=-1)
    return bs.reshape(-1)


def kernel(x, norm1_g, w_in, ret_norm_g, hgrn_norm_g, w_out, lower_bounds, norm2_g, w_router,
           w_gate, w_up, w_down, final_norm_g):
    B, S, D = x.shape
    depth = w_in.shape[0]
    E = w_router.shape[-1]
    cap = CAPACITY_FACTOR * S // E
    tm = min(512, S)
    tk = min(512, S)
    wn = min(128, cap)
    fm = min(512, cap)
    eg = min(8, E)
    n_tb = S // tk

    lbs = jax.nn.softmax(lower_bounds.astype(F32), axis=0)
    lbs = jnp.cumsum(lbs, axis=0) - lbs[0]
    cos, sin = _rope_tables(S)

    h = x
    for layer in range(depth):
        proj = _inproj(h, norm1_g[layer], w_in, layer, cos, sin, tm)
        r_out = _retention(proj, ret_norm_g[layer])
        h_raw = _hgrn(proj, lbs[layer], first_layer=(layer == 0))
        h1, xn, aff = _outproj(r_out, h_raw, proj, hgrn_norm_g[layer], w_out, layer,
                               h, norm2_g[layer], w_router[layer].T.astype(BF16), tm)
        pos, gate = _select(aff, cap)
        bs = _block_starts(pos, tk, cap)
        pos_t = pos.reshape(B, E, n_tb, tk).transpose(0, 2, 1, 3)
        gate_t = gate.reshape(B, E, n_tb, tk).transpose(0, 2, 1, 3)
        xe = _moe_gather(bs, xn, pos_t, gate_t, cap, tk, wn, eg)
        ye = _moe_ffn(xe, w_gate, w_up, w_down, layer, fm)
        h = _combine(bs, h1, ye, pos_t, gate_t, final_norm_g, cap, tk, wn,
                     final_norm=(layer == depth - 1))
    return h
```

```python
import functools

import numpy as np
import jax
import jax.numpy as jnp
from jax import lax
from jax.experimental import pallas as pl
from jax.experimental.pallas import tpu as pltpu

F32 = jnp.float32
BF16 = jnp.bfloat16

HEAD_DIM = 128
N_HEADS = 4
GROUP = N_HEADS * HEAD_DIM
N_GROUPS = 9
ROPE_BASE = 10000.0
NORM_EPS = 1e-6
CAPACITY_FACTOR = 2

RET_CHUNK = 256
RET_PAIR = 4
HGRN_CHUNK = 128
HGRN_PAIR = 4
HGRN_DESC = 16
LOG2E = 1.4426950408889634
VMEM_LIMIT = 58 * 1024 * 1024


def _cparams(n_axes):
    return pltpu.CompilerParams(
        dimension_semantics=("arbitrary",) * n_axes, vmem_limit_bytes=VMEM_LIMIT)


def _dot(a, b):
    return jnp.dot(a, b, preferred_element_type=F32)


def _dot_nt(a, b):
    return lax.dot_general(a, b, (((1,), (1,)), ((), ())), preferred_element_type=F32)


def _dot_tn(a, b):
    return lax.dot_general(a, b, (((0,), (0,)), ((), ())), preferred_element_type=F32)


def _silu(x):
    return x * (1.0 / (1.0 + jnp.exp(-x)))


def _rms(x):
    return x * lax.rsqrt(jnp.mean(x * x, axis=-1, keepdims=True) + NORM_EPS)


def _inproj_kernel(h_ref, g_ref, wf_ref, cos_ref, sin_ref, out_ref, w_ref):
    @pl.when(jnp.logical_and(pl.program_id(0) == 0, pl.program_id(1) == 0))
    def _():
        w_ref[...] = wf_ref[...].astype(BF16)

    xn = (_rms(h_ref[...]) * g_ref[...]).astype(BF16)
    cos = cos_ref[...]
    sin = sin_ref[...]
    scale = HEAD_DIM ** -0.5
    for j in range(N_GROUPS):
        acc = _dot(xn, w_ref[:, j * GROUP:(j + 1) * GROUP])
        if j in (0, 1):
            for hh in range(N_HEADS):
                sl = acc[:, hh * HEAD_DIM:(hh + 1) * HEAD_DIM]
                rot = sl * cos + pltpu.roll(sl, HEAD_DIM // 2, 1) * sin
                if j == 1:
                    rot = rot * scale
                out_ref[:, j * GROUP + hh * HEAD_DIM:j * GROUP + (hh + 1) * HEAD_DIM] = rot.astype(BF16)
            continue
        if j in (3, 8):
            acc = _silu(acc)
        elif j == 4:
            acc = _silu(acc) * scale
        out_ref[:, j * GROUP:(j + 1) * GROUP] = acc.astype(BF16)


def _inproj(h, g, w_in, layer, cos, sin, tm):
    B, S, D = h.shape
    ncol = w_in.shape[-1]
    return pl.pallas_call(
        _inproj_kernel,
        grid=(B, S // tm),
        in_specs=[
            pl.BlockSpec((None, tm, D), lambda b, i: (b, i, 0)),
            pl.BlockSpec((1, D), lambda b, i: (0, 0)),
            pl.BlockSpec((None, D, ncol), lambda b, i: (layer, 0, 0), pipeline_mode=pl.Buffered(1)),
            pl.BlockSpec((tm, HEAD_DIM), lambda b, i: (i, 0)),
            pl.BlockSpec((tm, HEAD_DIM), lambda b, i: (i, 0)),
        ],
        out_specs=pl.BlockSpec((None, tm, ncol), lambda b, i: (b, i, 0)),
        out_shape=jax.ShapeDtypeStruct((B, S, ncol), BF16),
        scratch_shapes=[pltpu.VMEM((D, ncol), BF16)],
        compiler_params=_cparams(2),
        name="inproj",
    )(h, g.reshape(1, D), w_in, cos, sin)


def _ret_kernel(q_ref, k_ref, v_ref, g_ref, dmat_ref, qdec_ref, kdec_ref, cdec_ref, rg_ref,
                out_ref, acc_ref, *, n_chunks):
    L = RET_CHUNK
    cdec = cdec_ref[0:1, :]
    zero_state = jnp.zeros((HEAD_DIM, HEAD_DIM), F32)

    def rows(i):
        return pl.ds(pl.multiple_of(i * L, L), L)

    def finish(sl, o):
        y = _rms(o) * rg_ref[...] * g_ref[sl, :].astype(F32)
        out_ref[sl, :] = y.astype(BF16)

    def step(i, states, second):
        state_f, state_b = states
        sfs = [rows(i * RET_PAIR + j) for j in range(RET_PAIR)]
        sbs = [rows(n_chunks - 1 - i * RET_PAIR - j) for j in range(RET_PAIR)]
        qf, kf, vf = ([r[s, :] for s in sfs] for r in (q_ref, k_ref, v_ref))
        qb, kb, vb = ([r[s, :] for s in sbs] for r in (q_ref, k_ref, v_ref))
        raw = [_dot_nt(q, k) for q, k in zip(qf, kf)]
        qdf = [(q.astype(F32) * qdec_ref[...]).astype(BF16) for q in qf]
        kdf = [(k.astype(F32) * kdec_ref[...]).astype(BF16) for k in kf]
        qdb = [(q.astype(F32) * kdec_ref[...]).astype(BF16) for q in qb]
        kdb = [(k.astype(F32) * qdec_ref[...]).astype(BF16) for k in kb]
        upd_f = [_dot_tn(k, v) for k, v in zip(kdf, vf)]
        upd_b = [_dot_tn(k, v) for k, v in zip(kdb, vb)]
        intra = [_dot((s * dmat_ref[...]).astype(BF16), v) for s, v in zip(raw, vf)]
        for j in range(RET_PAIR):
            of = intra[j] + _dot(qdf[j], state_f.astype(BF16))
            ob = _dot(qdb[j], state_b.astype(BF16))
            state_f = cdec * state_f + upd_f[j]
            state_b = cdec * state_b + upd_b[j]
            if second:
                finish(sfs[j], acc_ref[sfs[j], :] + of)
                finish(sbs[j], acc_ref[sbs[j], :] + ob)
            else:
                acc_ref[sfs[j], :] = of
                acc_ref[sbs[j], :] = ob
        return state_f, state_b

    steps = n_chunks // RET_PAIR
    states = lax.fori_loop(0, steps // 2, functools.partial(step, second=False),
                           (zero_state, zero_state))
    lax.fori_loop(steps // 2, steps, functools.partial(step, second=True), states)


def _ret_tables():
    L = RET_CHUNK
    t = np.arange(L, dtype=np.float64)
    lg = np.log1p(-(2.0 ** (-5.0 - np.arange(N_HEADS, dtype=np.float64))))
    dmat = np.exp(lg[:, None, None] * np.abs(t[:, None] - t[None, :])[None])
    qdec = np.exp(lg[:, None] * (t + 1.0))[:, :, None] * np.ones((1, 1, HEAD_DIM))
    kdec = np.exp(lg[:, None] * (L - 1.0 - t))[:, :, None] * np.ones((1, 1, HEAD_DIM))
    cdec = np.exp(lg * L)[:, None, None] * np.ones((1, 8, HEAD_DIM))
    return tuple(jnp.asarray(a, F32) for a in (dmat, qdec, kdec, cdec))


def _retention(proj, ret_g):
    B, S, _ = proj.shape
    L = RET_CHUNK
    assert S % (2 * RET_PAIR * L) == 0, "retention walks chunks from both ends, RET_PAIR at a time"
    dmat, qdec, kdec, cdec = _ret_tables()

    def col(c0):
        return pl.BlockSpec((None, S, HEAD_DIM), lambda b, h: (b, 0, c0 + h))

    def tab(r):
        return pl.BlockSpec((None, r, HEAD_DIM), lambda b, h: (h, 0, 0))

    return pl.pallas_call(
        functools.partial(_ret_kernel, n_chunks=S // L),
        grid=(B, N_HEADS),
        in_specs=[col(0), col(N_HEADS), col(2 * N_HEADS), col(3 * N_HEADS),
                  pl.BlockSpec((None, L, L), lambda b, h: (h, 0, 0)),
                  tab(L), tab(L), tab(8),
                  pl.BlockSpec((1, HEAD_DIM), lambda b, h: (0, h))],
        out_specs=pl.BlockSpec((None, S, HEAD_DIM), lambda b, h: (b, 0, h)),
        out_shape=jax.ShapeDtypeStruct((B, S, GROUP), BF16),
        scratch_shapes=[pltpu.VMEM((S, HEAD_DIM), F32)],
        compiler_params=_cparams(2),
        name="retention",
    )(proj, proj, proj, proj, dmat, qdec, kdec, cdec, ret_g.reshape(1, GROUP))


def _hgrn_gates(z, lb, first_layer):
    zf = z.astype(F32)
    e = jnp.exp(-jnp.abs(zf))
    inv = 1.0 / (1.0 + e)
    pos = zf >= 0.0
    sigm = jnp.where(pos, e * inv, inv)
    if first_layer:
        return jnp.minimum(zf, 0.0) * LOG2E - jnp.log2(1.0 + e), sigm
    sig = jnp.where(pos, inv, e * inv)
    return jnp.log2(lb + (1.0 - lb) * sig), (1.0 - lb) * sigm


def _hgrn_intra(chunks, lb, mtab_f, mtab_b, lvl, first_layer):
    L = HGRN_CHUNK
    n = len(chunks)
    qfs = [c[0].astype(F32) for c in chunks]
    gates_f = [_hgrn_gates(c[1], lb, first_layer) for c in chunks]
    gates_b = [_hgrn_gates(c[2], lb, first_layer) for c in chunks]

    def exponents(gates, mtab):
        splits = []
        for logf2, _ in gates:
            hi = logf2.astype(BF16)
            splits.append(jnp.concatenate([hi, (logf2 - hi.astype(F32)).astype(BF16)], axis=0))
        decs = []
        for a in range(0, n, 2):
            both = _dot(mtab, jnp.concatenate(splits[a:a + 2], axis=1))
            decs += [both[:, j * HEAD_DIM:(j + 1) * HEAD_DIM] for j in range(len(splits[a:a + 2]))]
        return decs

    decs_f = exponents(gates_f, mtab_f)
    decs_b = exponents(gates_b, mtab_b)
    cfs = [d[0:L, :] for d in decs_f]
    cbs = [d[0:L, :] for d in decs_b]
    kfs = [g[1] for g in gates_f]
    kbs = [g[1] for g in gates_b]

    scores = [jnp.where(lvl == -1, _dot_nt(chunks[i][0], (kfs[i] + kbs[i]).astype(BF16)), 0.0)
              for i in range(n)]
    m, level = 1, 0
    while m < L:
        for i in range(n):
            if m < 8:
                def as3(x):
                    return x.reshape(L // 8, 8, HEAD_DIM)
                pf = as3(jnp.exp2(decs_f[i][(1 + level) * L:(2 + level) * L, :]))
                pb = as3(jnp.exp2(decs_b[i][(1 + level) * L:(2 + level) * L, :]))
                sub = lax.broadcasted_iota(jnp.int32, (1, 8, HEAD_DIM), 1)
                upper = ((sub >> level) & 1) == 1
                qside = (as3(qfs[i]) * jnp.where(upper, pf, pb)).reshape(L, HEAD_DIM)
                kside = jnp.where(upper, as3(kbs[i]) * pb, as3(kfs[i]) * pf).reshape(L, HEAD_DIM)
            else:
                qslabs, kslabs = [], []
                for j in range(L // m):
                    mid = (j // 2) * 2 * m + m
                    sl = slice(j * m, (j + 1) * m)
                    if j % 2 == 1:
                        qdec = jnp.exp2(cfs[i][sl, :] - cfs[i][mid - 1:mid, :])
                        kslabs.append(kbs[i][sl, :] * jnp.exp2(cbs[i][mid:mid + 1, :] - cbs[i][sl, :]))
                    else:
                        qdec = jnp.exp2(cbs[i][sl, :] - cbs[i][mid:mid + 1, :])
                        kslabs.append(kfs[i][sl, :] * jnp.exp2(cfs[i][mid - 1:mid, :] - cfs[i][sl, :]))
                    qslabs.append(qfs[i][sl, :] * qdec)
                qside = jnp.concatenate(qslabs, axis=0)
                kside = jnp.concatenate(kslabs, axis=0)
            s = _dot_nt(qside.astype(BF16), kside.astype(BF16))
            scores[i] = jnp.where(lvl == level, s, scores[i])
        m, level = 2 * m, level + 1

    results = []
    for i in range(n):
        intra = _dot(scores[i].astype(BF16), chunks[i][3])
        cf_end = cfs[i][L - 1:L, :]
        cb_end = cbs[i][0:1, :]
        fwd = ((qfs[i] * jnp.exp2(cfs[i])).astype(BF16),
               (kfs[i] * jnp.exp2(cf_end - cfs[i])).astype(BF16), jnp.exp2(cf_end))
        bwd = ((qfs[i] * jnp.exp2(cbs[i])).astype(BF16),
               (kbs[i] * jnp.exp2(cb_end - cbs[i])).astype(BF16), jnp.exp2(cb_end))
        results.append((intra, fwd, bwd))
    return results


def _hgrn_kernel(q_ref, zf_ref, zb_ref, v_ref, lb_ref, mtabf_ref, mtabb_ref, lvl_ref,
                 out_ref, qeb_ref, keb_ref, dend_ref, *, n_chunks, first_layer):
    L = HGRN_CHUNK
    lb = lb_ref[...]
    zero_state = jnp.zeros((HEAD_DIM, HEAD_DIM), F32)

    def rows(c):
        return pl.ds(pl.multiple_of(c * L, L), L)

    def ascend(i, state):
        cs = [i * HGRN_PAIR + j for j in range(HGRN_PAIR)]
        sls = [rows(c) for c in cs]
        res = _hgrn_intra([(q_ref[sl, :], zf_ref[sl, :], zb_ref[sl, :], v_ref[sl, :]) for sl in sls],
                          lb, mtabf_ref[...], mtabb_ref[...], lvl_ref[...], first_layer)
        updates = [_dot_tn(v_ref[sl, :], r[1][1]) for sl, r in zip(sls, res)]
        for c, sl, (intra, (qe_f, _, dend_f), (qe_b, ke_b, dend_b)), upd in zip(cs, sls, res, updates):
            out_ref[sl, :] = intra + _dot_nt(qe_f, state.astype(BF16))
            state = dend_f * state + upd
            qeb_ref[sl, :] = qe_b
            keb_ref[sl, :] = ke_b
            dend_ref[c] = jnp.broadcast_to(dend_b, (8, HEAD_DIM))
        return state

    lax.fori_loop(0, n_chunks // HGRN_PAIR, ascend, zero_state)

    def descend(i, state):
        cs = [n_chunks - 1 - i * HGRN_DESC - j for j in range(HGRN_DESC)]
        sls = [rows(c) for c in cs]
        updates = [_dot_tn(v_ref[sl, :], keb_ref[sl, :]) for sl in sls]
        for c, sl, upd in zip(cs, sls, updates):
            out_ref[sl, :] += _dot_nt(qeb_ref[sl, :], state.astype(BF16))
            state = dend_ref[c][0:1, :] * state + upd
        return state

    lax.fori_loop(0, n_chunks // HGRN_DESC, descend, zero_state)


def _hgrn_tables():
    L = HGRN_CHUNK
    t = np.arange(L)[:, None]
    u = np.arange(L)[None, :]
    x = t ^ u
    lvl = np.where(x == 0, -1, np.floor(np.log2(np.maximum(x, 1)))).astype(np.int32)

    def exponent_rows(backward):
        blocks = [(u >= t) if backward else (u <= t)]
        m = 1
        while m < 8:
            mid = (t // (2 * m)) * (2 * m) + m
            if backward:
                blocks.append(np.where(t < mid, (u >= t) & (u < mid), (u >= mid) & (u < t)))
            else:
                blocks.append(np.where(t >= mid, (u >= mid) & (u <= t), (u > t) & (u < mid)))
            m *= 2
        tab = np.concatenate(blocks, axis=0).astype(np.float32)
        return np.concatenate([tab, tab], axis=1)

    return (jnp.asarray(exponent_rows(False), BF16), jnp.asarray(exponent_rows(True), BF16),
            jnp.asarray(lvl, jnp.int32))


def _hgrn(proj, lb, first_layer):
    B, S, _ = proj.shape
    L = HGRN_CHUNK
    n_chunks = S // L
    assert S % L == 0 and n_chunks % HGRN_PAIR == 0 and n_chunks % HGRN_DESC == 0
    mtab_f, mtab_b, lvl = _hgrn_tables()

    def col(g):
        return pl.BlockSpec((None, S, HEAD_DIM), lambda b, h: (b, 0, g * N_HEADS + h))

    def full(a):
        return pl.BlockSpec(a.shape, lambda b, h: (0, 0))

    return pl.pallas_call(
        functools.partial(_hgrn_kernel, n_chunks=n_chunks, first_layer=first_layer),
        grid=(B, N_HEADS),
        in_specs=[col(4), col(5), col(6), col(7),
                  pl.BlockSpec((1, HEAD_DIM), lambda b, h: (0, h)),
                  full(mtab_f), full(mtab_b), full(lvl)],
        out_specs=pl.BlockSpec((None, S, HEAD_DIM), lambda b, h: (b, 0, h)),
        out_shape=jax.ShapeDtypeStruct((B, S, GROUP), F32),
        scratch_shapes=[pltpu.VMEM((S, HEAD_DIM), BF16), pltpu.VMEM((S, HEAD_DIM), BF16),
                        pltpu.VMEM((n_chunks, 8, HEAD_DIM), F32)],
        compiler_params=_cparams(2),
        name="hgrn",
    )(proj, proj, proj, proj, lb.reshape(1, GROUP), mtab_f, mtab_b, lvl)


def _outproj_kernel(r_ref, hraw_ref, hg_ref, hgn_ref, wf_ref, h_ref, g2_ref, wr_ref,
                    h1_ref, xn_ref, aff_ref, w_ref):
    @pl.when(jnp.logical_and(pl.program_id(0) == 0, pl.program_id(1) == 0))
    def _():
        w_ref[...] = wf_ref[...].astype(BF16)

    hn = _rms(hraw_ref[...]) * hgn_ref[...] * hg_ref[...].astype(F32)
    mix = _dot(r_ref[...], w_ref[0:GROUP, :]) + _dot(hn.astype(BF16), w_ref[GROUP:2 * GROUP, :])
    h1 = h_ref[...] + mix
    h1_ref[...] = h1
    xn = (_rms(h1) * g2_ref[...]).astype(BF16)
    xn_ref[...] = xn
    logits = _dot_nt(wr_ref[...], xn)
    mx = jnp.max(logits, axis=0, keepdims=True)
    ex = jnp.exp(logits - mx)
    aff_ref[...] = ex / jnp.sum(ex, axis=0, keepdims=True)


def _outproj(r_out, h_raw, proj, hgrn_g, w_out, layer, h, g2, wr_t_bf, tm):
    B, S, D = h.shape
    E = wr_t_bf.shape[0]
    return pl.pallas_call(
        _outproj_kernel,
        grid=(B, S // tm),
        in_specs=[
            pl.BlockSpec((None, tm, GROUP), lambda b, i: (b, i, 0)),
            pl.BlockSpec((None, tm, GROUP), lambda b, i: (b, i, 0)),
            pl.BlockSpec((None, tm, GROUP), lambda b, i: (b, i, N_GROUPS - 1)),
            pl.BlockSpec((1, GROUP), lambda b, i: (0, 0)),
            pl.BlockSpec((None, 2 * GROUP, D), lambda b, i: (layer, 0, 0), pipeline_mode=pl.Buffered(1)),
            pl.BlockSpec((None, tm, D), lambda b, i: (b, i, 0)),
            pl.BlockSpec((1, D), lambda b, i: (0, 0)),
            pl.BlockSpec((E, D), lambda b, i: (0, 0)),
        ],
        out_specs=[
            pl.BlockSpec((None, tm, D), lambda b, i: (b, i, 0)),
            pl.BlockSpec((None, tm, D), lambda b, i: (b, i, 0)),
            pl.BlockSpec((None, E, tm), lambda b, i: (b, 0, i)),
        ],
        out_shape=[jax.ShapeDtypeStruct((B, S, D), F32),
                   jax.ShapeDtypeStruct((B, S, D), BF16),
                   jax.ShapeDtypeStruct((B, E, S), F32)],
        scratch_shapes=[pltpu.VMEM((2 * GROUP, D), BF16)],
        compiler_params=_cparams(2),
        name="outproj",
    )(r_out, h_raw, proj, hgrn_g.reshape(1, GROUP), w_out, h, g2.reshape(1, D), wr_t_bf)


def _select_kernel(aff_ref, pos_ref, gate_ref, *, cap):
    a = aff_ref[...]
    E, S = a.shape
    u = pltpu.bitcast(a, jnp.int32)
    capf = jnp.float32(cap)

    def count(mask):
        return jnp.sum(jnp.where(mask, 1.0, 0.0), axis=-1, keepdims=True)

    def value_bit(i, thr):
        cand = thr | (jnp.int32(1) << (30 - i))
        return jnp.where(count(u >= cand) >= capf, cand, thr)

    thr = lax.fori_loop(0, 31, value_bit, jnp.zeros((E, 1), jnp.int32))
    gt = u > thr
    eq = u == thr
    need = capf - count(gt)
    idx = lax.broadcasted_iota(jnp.int32, (E, S), 1)
    nbits = int(S).bit_length()

    def index_bit(i, cut):
        cand = cut | (jnp.int32(1) << (nbits - 1 - i))
        return jnp.where(count(eq & (idx < cand)) <= need, cand, cut)

    cut = lax.fori_loop(0, nbits, index_bit, jnp.zeros((E, 1), jnp.int32))
    sel = gt | (eq & (idx < cut))
    gate_ref[...] = jnp.where(sel, a, 0.0)

    li = lax.broadcasted_iota(jnp.int32, (128, 128), 0)
    lj = lax.broadcasted_iota(jnp.int32, (128, 128), 1)
    upper = jnp.where(li <= lj, 1.0, 0.0).astype(BF16)
    carry = jnp.zeros((E, 1), F32)
    for j in range(S // 128):
        sb = jnp.where(sel[:, j * 128:(j + 1) * 128], 1.0, 0.0)
        incl = _dot(sb.astype(BF16), upper)
        pos_ref[:, j * 128:(j + 1) * 128] = (incl - sb + carry).astype(jnp.int32)
        carry = carry + incl[:, 127:128]


def _select(aff, cap):
    B, E, S = aff.shape
    spec = pl.BlockSpec((None, E, S), lambda b: (b, 0, 0))
    return pl.pallas_call(
        functools.partial(_select_kernel, cap=cap),
        grid=(B,),
        in_specs=[spec],
        out_specs=[spec, spec],
        out_shape=[jax.ShapeDtypeStruct((B, E, S), jnp.int32),
                   jax.ShapeDtypeStruct((B, E, S), F32)],
        compiler_params=_cparams(1),
        name="select",
    )(aff)


def _gather_kernel(bs_ref, x_ref, pos_ref, gate_ref, xe_ref, oh_ref,
                   *, n_tb, tk, wn, cap, n_exp, eg):
    b = pl.program_id(0)
    grp = pl.program_id(1)
    tb = pl.program_id(2)
    lane_slot = lax.broadcasted_iota(jnp.int32, (wn, tk), 0)

    @pl.when(tb == 0)
    def _():
        xe_ref[...] = jnp.zeros_like(xe_ref)

    def onehot(el, start, first):
        key = jnp.where(gate_ref[el:el + 1, :] > 0.0, pos_ref[el:el + 1, :], -1)
        slot = start + lane_slot
        hit = jnp.logical_and(key == slot, slot >= first)
        return jnp.where(hit, 1.0, 0.0).astype(BF16)

    def window_start(first):
        return pl.multiple_of(jnp.minimum(first, cap - wn), 16)

    spans = []
    for el in range(eg):
        base = (b * n_exp + grp * eg + el) * (n_tb + 1)
        first = (bs_ref[base + tb] // 16) * 16
        start = window_start(first)
        spans.append((first, start, bs_ref[base + tb + 1]))
        oh_ref[el * wn:(el + 1) * wn, :] = onehot(el, start, first)
    rows = _dot(oh_ref[...], x_ref[...]).astype(BF16)
    for el in range(eg):
        xe_ref[el, pl.ds(spans[el][1], wn), :] += rows[el * wn:(el + 1) * wn, :]

    for el in range(eg):
        first0, _, end = spans[el]
        n_more = jnp.maximum(end - first0 - 1, 0) // wn

        def window(w, carry):
            first = first0 + (w + 1) * wn
            start = window_start(first)
            xe_ref[el, pl.ds(start, wn), :] += _dot(onehot(el, start, first), x_ref[...]).astype(BF16)
            return carry

        lax.fori_loop(0, n_more, window, 0)


def _moe_gather(bs, xn, pos_t, gate_t, cap, tk, wn, eg):
    B, S, D = xn.shape
    E = pos_t.shape[2]
    n_tb = S // tk
    grid_spec = pltpu.PrefetchScalarGridSpec(
        num_scalar_prefetch=1,
        grid=(B, E // eg, n_tb),
        in_specs=[
            pl.BlockSpec((None, tk, D), lambda b, g, t, bs: (b, t, 0)),
            pl.BlockSpec((None, None, eg, tk), lambda b, g, t, bs: (b, t, g, 0)),
            pl.BlockSpec((None, None, eg, tk), lambda b, g, t, bs: (b, t, g, 0)),
        ],
        out_specs=pl.BlockSpec((None, eg, cap, D), lambda b, g, t, bs: (b, g, 0, 0)),
        scratch_shapes=[pltpu.VMEM((eg * wn, tk), BF16)],
    )
    return pl.pallas_call(
        functools.partial(_gather_kernel, n_tb=n_tb, tk=tk, wn=wn, cap=cap, n_exp=E, eg=eg),
        grid_spec=grid_spec,
        out_shape=jax.ShapeDtypeStruct((B, E, cap, D), BF16),
        compiler_params=_cparams(3),
        name="moe_gather",
    )(bs, xn, pos_t, gate_t)


def _ffn_kernel(xe_ref, wg_ref, wu_ref, wd_ref, ye_ref, wg_bf, wu_bf, wd_bf, *, fm):
    @pl.when(pl.program_id(1) == 0)
    def _():
        wg_bf[...] = wg_ref[...].astype(BF16)
        wu_bf[...] = wu_ref[...].astype(BF16)
        wd_bf[...] = wd_ref[...].astype(BF16)

    for r in range(xe_ref.shape[0] // fm):
        xe = xe_ref[r * fm:(r + 1) * fm, :]
        hid = (_silu(_dot(xe, wg_bf[...])) * _dot(xe, wu_bf[...])).astype(BF16)
        ye_ref[r * fm:(r + 1) * fm, :] = _dot(hid, wd_bf[...]).astype(BF16)


def _moe_ffn(xe, w_gate, w_up, w_down, layer, fm):
    B, E, cap, D = xe.shape
    FF = w_gate.shape[-1]
    rows = pl.BlockSpec((None, None, cap, D), lambda e, b: (b, e, 0, 0))
    return pl.pallas_call(
        functools.partial(_ffn_kernel, fm=fm),
        grid=(E, B),
        in_specs=[rows,
                  pl.BlockSpec((None, None, D, FF), lambda e, b: (layer, e, 0, 0)),
                  pl.BlockSpec((None, None, D, FF), lambda e, b: (layer, e, 0, 0)),
                  pl.BlockSpec((None, None, FF, D), lambda e, b: (layer, e, 0, 0))],
        out_specs=rows,
        out_shape=jax.ShapeDtypeStruct((B, E, cap, D), BF16),
        scratch_shapes=[pltpu.VMEM((D, FF), BF16), pltpu.VMEM((D, FF), BF16), pltpu.VMEM((FF, D), BF16)],
        compiler_params=_cparams(2),
        name="moe_ffn",
    )(xe, w_gate, w_up, w_down)


def _combine_kernel(bs_ref, h1_ref, ye_ref, pos_ref, gate_ref, fg_ref, out_ref, w_ref, y_ref,
                    *, n_tb, tk, wn, cap, n_exp, final_norm):
    b = pl.program_id(0)
    tb = pl.program_id(1)
    lane_slot = lax.broadcasted_iota(jnp.int32, (wn, tk), 0)

    def window_start(first):
        return pl.multiple_of(jnp.minimum(first, cap - wn), 16)

    def weights(e, start, first):
        slot = start + lane_slot
        hit = jnp.logical_and(pos_ref[e:e + 1, :] == slot, slot >= first)
        return jnp.where(hit, gate_ref[e:e + 1, :], 0.0).astype(BF16)

    firsts = []
    for e in range(n_exp):
        base = (b * n_exp + e) * (n_tb + 1)
        first = (bs_ref[base + tb] // 16) * 16
        start = window_start(first)
        firsts.append((first, bs_ref[base + tb + 1]))
        w_ref[e * wn:(e + 1) * wn, :] = weights(e, start, first)
        y_ref[e * wn:(e + 1) * wn, :] = ye_ref[e, pl.ds(start, wn), :]
    out_ref[...] = h1_ref[...] + _dot_tn(w_ref[...], y_ref[...])

    for e in range(n_exp):
        first0, end = firsts[e]
        n_more = jnp.maximum(end - first0 - 1, 0) // wn

        def window(w, carry):
            first = first0 + (w + 1) * wn
            start = window_start(first)
            out_ref[...] += _dot_tn(weights(e, start, first), ye_ref[e, pl.ds(start, wn), :])
            return carry

        lax.fori_loop(0, n_more, window, 0)
    if final_norm:
        out_ref[...] = _rms(out_ref[...]) * fg_ref[...]


def _combine(bs, h1, ye, pos_t, gate_t, final_g, cap, tk, wn, final_norm):
    B, S, D = h1.shape
    E = ye.shape[1]
    n_tb = S // tk
    grid_spec = pltpu.PrefetchScalarGridSpec(
        num_scalar_prefetch=1,
        grid=(B, n_tb),
        in_specs=[
            pl.BlockSpec((None, tk, D), lambda b, t, bs: (b, t, 0)),
            pl.BlockSpec((None, E, cap, D), lambda b, t, bs: (b, 0, 0, 0), pipeline_mode=pl.Buffered(1)),
            pl.BlockSpec((None, None, E, tk), lambda b, t, bs: (b, t, 0, 0)),
            pl.BlockSpec((None, None, E, tk), lambda b, t, bs: (b, t, 0, 0)),
            pl.BlockSpec((1, D), lambda b, t, bs: (0, 0)),
        ],
        out_specs=pl.BlockSpec((None, tk, D), lambda b, t, bs: (b, t, 0)),
        scratch_shapes=[pltpu.VMEM((E * wn, tk), BF16), pltpu.VMEM((E * wn, D), BF16)],
    )
    return pl.pallas_call(
        functools.partial(_combine_kernel, n_tb=n_tb, tk=tk, wn=wn, cap=cap, n_exp=E,
                          final_norm=final_norm),
        grid_spec=grid_spec,
        out_shape=jax.ShapeDtypeStruct((B, S, D), F32),
        compiler_params=_cparams(2),
        name="combine",
    )(bs, h1, ye, pos_t, gate_t, final_g.reshape(1, D))


def _rope_tables(S):
    half = HEAD_DIM // 2
    inv_freq = ROPE_BASE ** (-jnp.arange(half, dtype=F32) / half)
    ang = jnp.arange(S).astype(F32)[:, None] * inv_freq[None, :]
    cos, sin = jnp.cos(ang), jnp.sin(ang)
    return jnp.concatenate([cos, cos], axis=-1), jnp.concatenate([-sin, sin], axis=-1)


def _block_starts(pos, tk, cap):
    B, E, _ = pos.shape
    bs = jnp.concatenate([pos[:, :, ::tk], jnp.full((B, E, 1), cap, jnp.int32)], axis=-1)
    return bs.reshape(-1)


def kernel(x, norm1_g, w_in, ret_norm_g, hgrn_norm_g, w_out, lower_bounds, norm2_g, w_router,
           w_gate, w_up, w_down, final_norm_g):
    B, S, D = x.shape
    depth = w_in.shape[0]
    E = w_router.shape[-1]
    cap = CAPACITY_FACTOR * S // E
    tm = min(512, S)
    tk = min(512, S)
    wn = min(96, cap)
    fm = min(512, cap)
    eg = min(8, E)
    n_tb = S // tk

    lbs = jax.nn.softmax(lower_bounds.astype(F32), axis=0)
    lbs = jnp.cumsum(lbs, axis=0) - lbs[0]
    cos, sin = _rope_tables(S)

    h = x
    for layer in range(depth):
        proj = _inproj(h, norm1_g[layer], w_in, layer, cos, sin, tm)
        r_out = _retention(proj, ret_norm_g[layer])
        h_raw = _hgrn(proj, lbs[layer], first_layer=(layer == 0))
        h1, xn, aff = _outproj(r_out, h_raw, proj, hgrn_norm_g[layer], w_out, layer,
                               h, norm2_g[layer], w_router[layer].T.astype(BF16), tm)
        pos, gate = _select(aff, cap)
        bs = _block_starts(pos, tk, cap)
        pos_t = pos.reshape(B, E, n_tb, tk).transpose(0, 2, 1, 3)
        gate_t = gate.reshape(B, E, n_tb, tk).transpose(0, 2, 1, 3)
        xe = _moe_gather(bs, xn, pos_t, gate_t, cap, tk, wn, eg)
        ye = _moe_ffn(xe, w_gate, w_up, w_down, layer, fm)
        h = _combine(bs, h1, ye, pos_t, gate_t, final_norm_g, cap, tk, wn,
                     final_norm=(layer == depth - 1))
    return h
```

```python
import functools

import numpy as np
import jax
import jax.numpy as jnp
from jax import lax
from jax.experimental import pallas as pl
from jax.experimental.pallas import tpu as pltpu

F32 = jnp.float32
BF16 = jnp.bfloat16

HEAD_DIM = 128
N_HEADS = 4
GROUP = N_HEADS * HEAD_DIM
N_GROUPS = 9
ROPE_BASE = 10000.0
NORM_EPS = 1e-6
CAPACITY_FACTOR = 2

RET_CHUNK = 256
RET_PAIR = 4
HGRN_CHUNK = 128
HGRN_PAIR = 4
HGRN_DESC = 16
LOG2E = 1.4426950408889634
VMEM_LIMIT = 58 * 1024 * 1024


def _cparams(n_axes):
    return pltpu.CompilerParams(
        dimension_semantics=("arbitrary",) * n_axes, vmem_limit_bytes=VMEM_LIMIT)


def _dot(a, b):
    return jnp.dot(a, b, preferred_element_type=F32)


def _dot_nt(a, b):
    return lax.dot_general(a, b, (((1,), (1,)), ((), ())), preferred_element_type=F32)


def _dot_tn(a, b):
    return lax.dot_general(a, b, (((0,), (0,)), ((), ())), preferred_element_type=F32)


def _silu(x):
    return x * (1.0 / (1.0 + jnp.exp(-x)))


def _rms(x):
    return x * lax.rsqrt(jnp.mean(x * x, axis=-1, keepdims=True) + NORM_EPS)


def _inproj_kernel(h_ref, g_ref, wf_ref, cos_ref, sin_ref, out_ref, w_ref):
    @pl.when(jnp.logical_and(pl.program_id(0) == 0, pl.program_id(1) == 0))
    def _():
        w_ref[...] = wf_ref[...].astype(BF16)

    xn = (_rms(h_ref[...]) * g_ref[...]).astype(BF16)
    cos = cos_ref[...]
    sin = sin_ref[...]
    scale = HEAD_DIM ** -0.5
    for j in range(N_GROUPS):
        acc = _dot(xn, w_ref[:, j * GROUP:(j + 1) * GROUP])
        if j in (0, 1):
            for hh in range(N_HEADS):
                sl = acc[:, hh * HEAD_DIM:(hh + 1) * HEAD_DIM]
                rot = sl * cos + pltpu.roll(sl, HEAD_DIM // 2, 1) * sin
                if j == 1:
                    rot = rot * scale
                out_ref[:, j * GROUP + hh * HEAD_DIM:j * GROUP + (hh + 1) * HEAD_DIM] = rot.astype(BF16)
            continue
        if j in (3, 8):
            acc = _silu(acc)
        elif j == 4:
            acc = _silu(acc) * scale
        out_ref[:, j * GROUP:(j + 1) * GROUP] = acc.astype(BF16)


def _inproj(h, g, w_in, layer, cos, sin, tm):
    B, S, D = h.shape
    ncol = w_in.shape[-1]
    return pl.pallas_call(
        _inproj_kernel,
        grid=(B, S // tm),
        in_specs=[
            pl.BlockSpec((None, tm, D), lambda b, i: (b, i, 0)),
            pl.BlockSpec((1, D), lambda b, i: (0, 0)),
            pl.BlockSpec((None, D, ncol), lambda b, i: (layer, 0, 0), pipeline_mode=pl.Buffered(1)),
            pl.BlockSpec((tm, HEAD_DIM), lambda b, i: (i, 0)),
            pl.BlockSpec((tm, HEAD_DIM), lambda b, i: (i, 0)),
        ],
        out_specs=pl.BlockSpec((None, tm, ncol), lambda b, i: (b, i, 0)),
        out_shape=jax.ShapeDtypeStruct((B, S, ncol), BF16),
        scratch_shapes=[pltpu.VMEM((D, ncol), BF16)],
        compiler_params=_cparams(2),
        name="inproj",
    )(h, g.reshape(1, D), w_in, cos, sin)


def _ret_kernel(q_ref, k_ref, v_ref, g_ref, dmat_ref, qdec_ref, kdec_ref, cdec_ref, rg_ref,
                out_ref, acc_ref, *, n_chunks):
    L = RET_CHUNK
    cdec = cdec_ref[0:1, :]
    zero_state = jnp.zeros((HEAD_DIM, HEAD_DIM), F32)

    def rows(i):
        return pl.ds(pl.multiple_of(i * L, L), L)

    def finish(sl, o):
        y = _rms(o) * rg_ref[...] * g_ref[sl, :].astype(F32)
        out_ref[sl, :] = y.astype(BF16)

    def step(i, states, second):
        state_f, state_b = states
        sfs = [rows(i * RET_PAIR + j) for j in range(RET_PAIR)]
        sbs = [rows(n_chunks - 1 - i * RET_PAIR - j) for j in range(RET_PAIR)]
        qf, kf, vf = ([r[s, :] for s in sfs] for r in (q_ref, k_ref, v_ref))
        qb, kb, vb = ([r[s, :] for s in sbs] for r in (q_ref, k_ref, v_ref))
        raw = [_dot_nt(q, k) for q, k in zip(qf, kf)]
        qdf = [(q.astype(F32) * qdec_ref[...]).astype(BF16) for q in qf]
        kdf = [(k.astype(F32) * kdec_ref[...]).astype(BF16) for k in kf]
        qdb = [(q.astype(F32) * kdec_ref[...]).astype(BF16) for q in qb]
        kdb = [(k.astype(F32) * qdec_ref[...]).astype(BF16) for k in kb]
        upd_f = [_dot_tn(k, v) for k, v in zip(kdf, vf)]
        upd_b = [_dot_tn(k, v) for k, v in zip(kdb, vb)]
        intra = [_dot((s * dmat_ref[...]).astype(BF16), v) for s, v in zip(raw, vf)]
        for j in range(RET_PAIR):
            of = intra[j] + _dot(qdf[j], state_f.astype(BF16))
            ob = _dot(qdb[j], state_b.astype(BF16))
            state_f = cdec * state_f + upd_f[j]
            state_b = cdec * state_b + upd_b[j]
            if second:
                finish(sfs[j], acc_ref[sfs[j], :] + of)
                finish(sbs[j], acc_ref[sbs[j], :] + ob)
            else:
                acc_ref[sfs[j], :] = of
                acc_ref[sbs[j], :] = ob
        return state_f, state_b

    steps = n_chunks // RET_PAIR
    states = lax.fori_loop(0, steps // 2, functools.partial(step, second=False),
                           (zero_state, zero_state))
    lax.fori_loop(steps // 2, steps, functools.partial(step, second=True), states)


def _ret_tables():
    L = RET_CHUNK
    t = np.arange(L, dtype=np.float64)
    lg = np.log1p(-(2.0 ** (-5.0 - np.arange(N_HEADS, dtype=np.float64))))
    dmat = np.exp(lg[:, None, None] * np.abs(t[:, None] - t[None, :])[None])
    qdec = np.exp(lg[:, None] * (t + 1.0))[:, :, None] * np.ones((1, 1, HEAD_DIM))
    kdec = np.exp(lg[:, None] * (L - 1.0 - t))[:, :, None] * np.ones((1, 1, HEAD_DIM))
    cdec = np.exp(lg * L)[:, None, None] * np.ones((1, 8, HEAD_DIM))
    return tuple(jnp.asarray(a, F32) for a in (dmat, qdec, kdec, cdec))


def _retention(proj, ret_g):
    B, S, _ = proj.shape
    L = RET_CHUNK
    assert S % (2 * RET_PAIR * L) == 0, "retention walks chunks from both ends, RET_PAIR at a time"
    dmat, qdec, kdec, cdec = _ret_tables()

    def col(c0):
        return pl.BlockSpec((None, S, HEAD_DIM), lambda b, h: (b, 0, c0 + h))

    def tab(r):
        return pl.BlockSpec((None, r, HEAD_DIM), lambda b, h: (h, 0, 0))

    return pl.pallas_call(
        functools.partial(_ret_kernel, n_chunks=S // L),
        grid=(B, N_HEADS),
        in_specs=[col(0), col(N_HEADS), col(2 * N_HEADS), col(3 * N_HEADS),
                  pl.BlockSpec((None, L, L), lambda b, h: (h, 0, 0)),
                  tab(L), tab(L), tab(8),
                  pl.BlockSpec((1, HEAD_DIM), lambda b, h: (0, h))],
        out_specs=pl.BlockSpec((None, S, HEAD_DIM), lambda b, h: (b, 0, h)),
        out_shape=jax.ShapeDtypeStruct((B, S, GROUP), BF16),
        scratch_shapes=[pltpu.VMEM((S, HEAD_DIM), F32)],
        compiler_params=_cparams(2),
        name="retention",
    )(proj, proj, proj, proj, dmat, qdec, kdec, cdec, ret_g.reshape(1, GROUP))


def _hgrn_gates(z, lb, first_layer):
    zf = z.astype(F32)
    e = jnp.exp(-jnp.abs(zf))
    inv = 1.0 / (1.0 + e)
    pos = zf >= 0.0
    sigm = jnp.where(pos, e * inv, inv)
    if first_layer:
        return jnp.minimum(zf, 0.0) * LOG2E - jnp.log2(1.0 + e), sigm
    sig = jnp.where(pos, inv, e * inv)
    return jnp.log2(lb + (1.0 - lb) * sig), (1.0 - lb) * sigm


def _hgrn_intra(chunks, lb, mtab_f, mtab_b, lvl, first_layer):
    L = HGRN_CHUNK
    n = len(chunks)
    qfs = [c[0].astype(F32) for c in chunks]
    gates_f = [_hgrn_gates(c[1], lb, first_layer) for c in chunks]
    gates_b = [_hgrn_gates(c[2], lb, first_layer) for c in chunks]

    def exponents(gates, mtab):
        splits = []
        for logf2, _ in gates:
            hi = logf2.astype(BF16)
            splits.append(jnp.concatenate([hi, (logf2 - hi.astype(F32)).astype(BF16)], axis=0))
        decs = []
        for a in range(0, n, 2):
            both = _dot(mtab, jnp.concatenate(splits[a:a + 2], axis=1))
            decs += [both[:, j * HEAD_DIM:(j + 1) * HEAD_DIM] for j in range(len(splits[a:a + 2]))]
        return decs

    decs_f = exponents(gates_f, mtab_f)
    decs_b = exponents(gates_b, mtab_b)
    cfs = [d[0:L, :] for d in decs_f]
    cbs = [d[0:L, :] for d in decs_b]
    kfs = [g[1] for g in gates_f]
    kbs = [g[1] for g in gates_b]

    scores = [jnp.where(lvl == -1, _dot_nt(chunks[i][0], (kfs[i] + kbs[i]).astype(BF16)), 0.0)
              for i in range(n)]
    m, level = 1, 0
    while m < L:
        for i in range(n):
            if m < 8:
                def as3(x):
                    return x.reshape(L // 8, 8, HEAD_DIM)
                pf = as3(jnp.exp2(decs_f[i][(1 + level) * L:(2 + level) * L, :]))
                pb = as3(jnp.exp2(decs_b[i][(1 + level) * L:(2 + level) * L, :]))
                sub = lax.broadcasted_iota(jnp.int32, (1, 8, HEAD_DIM), 1)
                upper = ((sub >> level) & 1) == 1
                qside = (as3(qfs[i]) * jnp.where(upper, pf, pb)).reshape(L, HEAD_DIM)
                kside = jnp.where(upper, as3(kbs[i]) * pb, as3(kfs[i]) * pf).reshape(L, HEAD_DIM)
            else:
                qslabs, kslabs = [], []
                for j in range(L // m):
                    mid = (j // 2) * 2 * m + m
                    sl = slice(j * m, (j + 1) * m)
                    if j % 2 == 1:
                        qdec = jnp.exp2(cfs[i][sl, :] - cfs[i][mid - 1:mid, :])
                        kslabs.append(kbs[i][sl, :] * jnp.exp2(cbs[i][mid:mid + 1, :] - cbs[i][sl, :]))
                    else:
                        qdec = jnp.exp2(cbs[i][sl, :] - cbs[i][mid:mid + 1, :])
                        kslabs.append(kfs[i][sl, :] * jnp.exp2(cfs[i][mid - 1:mid, :] - cfs[i][sl, :]))
                    qslabs.append(qfs[i][sl, :] * qdec)
                qside = jnp.concatenate(qslabs, axis=0)
                kside = jnp.concatenate(kslabs, axis=0)
            s = _dot_nt(qside.astype(BF16), kside.astype(BF16))
            scores[i] = jnp.where(lvl == level, s, scores[i])
        m, level = 2 * m, level + 1

    results = []
    for i in range(n):
        intra = _dot(scores[i].astype(BF16), chunks[i][3])
        cf_end = cfs[i][L - 1:L, :]
        cb_end = cbs[i][0:1, :]
        fwd = ((qfs[i] * jnp.exp2(cfs[i])).astype(BF16),
               (kfs[i] * jnp.exp2(cf_end - cfs[i])).astype(BF16), jnp.exp2(cf_end))
        bwd = ((qfs[i] * jnp.exp2(cbs[i])).astype(BF16),
               (kbs[i] * jnp.exp2(cb_end - cbs[i])).astype(BF16), jnp.exp2(cb_end))
        results.append((intra, fwd, bwd))
    return results


def _hgrn_kernel(q_ref, zf_ref, zb_ref, v_ref, lb_ref, mtabf_ref, mtabb_ref, lvl_ref,
                 out_ref, qeb_ref, keb_ref, dend_ref, *, n_chunks, first_layer):
    L = HGRN_CHUNK
    lb = lb_ref[...]
    zero_state = jnp.zeros((HEAD_DIM, HEAD_DIM), F32)

    def rows(c):
        return pl.ds(pl.multiple_of(c * L, L), L)

    def ascend(i, state):
        cs = [i * HGRN_PAIR + j for j in range(HGRN_PAIR)]
        sls = [rows(c) for c in cs]
        res = _hgrn_intra([(q_ref[sl, :], zf_ref[sl, :], zb_ref[sl, :], v_ref[sl, :]) for sl in sls],
                          lb, mtabf_ref[...], mtabb_ref[...], lvl_ref[...], first_layer)
        updates = [_dot_tn(v_ref[sl, :], r[1][1]) for sl, r in zip(sls, res)]
        for c, sl, (intra, (qe_f, _, dend_f), (qe_b, ke_b, dend_b)), upd in zip(cs, sls, res, updates):
            out_ref[sl, :] = intra + _dot_nt(qe_f, state.astype(BF16))
            state = dend_f * state + upd
            qeb_ref[sl, :] = qe_b
            keb_ref[sl, :] = ke_b
            dend_ref[c] = jnp.broadcast_to(dend_b, (8, HEAD_DIM))
        return state

    lax.fori_loop(0, n_chunks // HGRN_PAIR, ascend, zero_state)

    def descend(i, state):
        cs = [n_chunks - 1 - i * HGRN_DESC - j for j in range(HGRN_DESC)]
        sls = [rows(c) for c in cs]
        updates = [_dot_tn(v_ref[sl, :], keb_ref[sl, :]) for sl in sls]
        for c, sl, upd in zip(cs, sls, updates):
            out_ref[sl, :] += _dot_nt(qeb_ref[sl, :], state.astype(BF16))
            state = dend_ref[c][0:1, :] * state + upd
        return state

    lax.fori_loop(0, n_chunks // HGRN_DESC, descend, zero_state)


def _hgrn_tables():
    L = HGRN_CHUNK
    t = np.arange(L)[:, None]
    u = np.arange(L)[None, :]
    x = t ^ u
    lvl = np.where(x == 0, -1, np.floor(np.log2(np.maximum(x, 1)))).astype(np.int32)

    def exponent_rows(backward):
        blocks = [(u >= t) if backward else (u <= t)]
        m = 1
        while m < 8:
            mid = (t // (2 * m)) * (2 * m) + m
            if backward:
                blocks.append(np.where(t < mid, (u >= t) & (u < mid), (u >= mid) & (u < t)))
            else:
                blocks.append(np.where(t >= mid, (u >= mid) & (u <= t), (u > t) & (u < mid)))
            m *= 2
        tab = np.concatenate(blocks, axis=0).astype(np.float32)
        return np.concatenate([tab, tab], axis=1)

    return (jnp.asarray(exponent_rows(False), BF16), jnp.asarray(exponent_rows(True), BF16),
            jnp.asarray(lvl, jnp.int32))


def _hgrn(proj, lb, first_layer):
    B, S, _ = proj.shape
    L = HGRN_CHUNK
    n_chunks = S // L
    assert S % L == 0 and n_chunks % HGRN_PAIR == 0 and n_chunks % HGRN_DESC == 0
    mtab_f, mtab_b, lvl = _hgrn_tables()

    def col(g):
        return pl.BlockSpec((None, S, HEAD_DIM), lambda b, h: (b, 0, g * N_HEADS + h))

    def full(a):
        return pl.BlockSpec(a.shape, lambda b, h: (0, 0))

    return pl.pallas_call(
        functools.partial(_hgrn_kernel, n_chunks=n_chunks, first_layer=first_layer),
        grid=(B, N_HEADS),
        in_specs=[col(4), col(5), col(6), col(7),
                  pl.BlockSpec((1, HEAD_DIM), lambda b, h: (0, h)),
                  full(mtab_f), full(mtab_b), full(lvl)],
        out_specs=pl.BlockSpec((None, S, HEAD_DIM), lambda b, h: (b, 0, h)),
        out_shape=jax.ShapeDtypeStruct((B, S, GROUP), F32),
        scratch_shapes=[pltpu.VMEM((S, HEAD_DIM), BF16), pltpu.VMEM((S, HEAD_DIM), BF16),
                        pltpu.VMEM((n_chunks, 8, HEAD_DIM), F32)],
        compiler_params=_cparams(2),
        name="hgrn",
    )(proj, proj, proj, proj, lb.reshape(1, GROUP), mtab_f, mtab_b, lvl)


def _outproj_kernel(r_ref, hraw_ref, hg_ref, hgn_ref, wf_ref, h_ref, g2_ref, wr_ref,
                    h1_ref, xn_ref, aff_ref, w_ref):
    @pl.when(jnp.logical_and(pl.program_id(0) == 0, pl.program_id(1) == 0))
    def _():
        w_ref[...] = wf_ref[...].astype(BF16)

    hn = _rms(hraw_ref[...]) * hgn_ref[...] * hg_ref[...].astype(F32)
    mix = _dot(r_ref[...], w_ref[0:GROUP, :]) + _dot(hn.astype(BF16), w_ref[GROUP:2 * GROUP, :])
    h1 = h_ref[...] + mix
    h1_ref[...] = h1
    xn = (_rms(h1) * g2_ref[...]).astype(BF16)
    xn_ref[...] = xn
    logits = _dot_nt(wr_ref[...], xn)
    mx = jnp.max(logits, axis=0, keepdims=True)
    ex = jnp.exp(logits - mx)
    aff_ref[...] = ex / jnp.sum(ex, axis=0, keepdims=True)


def _outproj(r_out, h_raw, proj, hgrn_g, w_out, layer, h, g2, wr_t_bf, tm):
    B, S, D = h.shape
    E = wr_t_bf.shape[0]
    return pl.pallas_call(
        _outproj_kernel,
        grid=(B, S // tm),
        in_specs=[
            pl.BlockSpec((None, tm, GROUP), lambda b, i: (b, i, 0)),
            pl.BlockSpec((None, tm, GROUP), lambda b, i: (b, i, 0)),
            pl.BlockSpec((None, tm, GROUP), lambda b, i: (b, i, N_GROUPS - 1)),
            pl.BlockSpec((1, GROUP), lambda b, i: (0, 0)),
            pl.BlockSpec((None, 2 * GROUP, D), lambda b, i: (layer, 0, 0), pipeline_mode=pl.Buffered(1)),
            pl.BlockSpec((None, tm, D), lambda b, i: (b, i, 0)),
            pl.BlockSpec((1, D), lambda b, i: (0, 0)),
            pl.BlockSpec((E, D), lambda b, i: (0, 0)),
        ],
        out_specs=[
            pl.BlockSpec((None, tm, D), lambda b, i: (b, i, 0)),
            pl.BlockSpec((None, tm, D), lambda b, i: (b, i, 0)),
            pl.BlockSpec((None, E, tm), lambda b, i: (b, 0, i)),
        ],
        out_shape=[jax.ShapeDtypeStruct((B, S, D), F32),
                   jax.ShapeDtypeStruct((B, S, D), BF16),
                   jax.ShapeDtypeStruct((B, E, S), F32)],
        scratch_shapes=[pltpu.VMEM((2 * GROUP, D), BF16)],
        compiler_params=_cparams(2),
        name="outproj",
    )(r_out, h_raw, proj, hgrn_g.reshape(1, GROUP), w_out, h, g2.reshape(1, D), wr_t_bf)


def _select_kernel(aff_ref, pos_ref, gate_ref, *, cap):
    a = aff_ref[...]
    E, S = a.shape
    u = pltpu.bitcast(a, jnp.int32)
    capf = jnp.float32(cap)

    def count(mask):
        return jnp.sum(jnp.where(mask, 1.0, 0.0), axis=-1, keepdims=True)

    def value_bit(i, thr):
        cand = thr | (jnp.int32(1) << (30 - i))
        return jnp.where(count(u >= cand) >= capf, cand, thr)

    thr = lax.fori_loop(0, 31, value_bit, jnp.zeros((E, 1), jnp.int32))
    gt = u > thr
    eq = u == thr
    need = capf - count(gt)
    idx = lax.broadcasted_iota(jnp.int32, (E, S), 1)
    nbits = int(S).bit_length()

    def index_bit(i, cut):
        cand = cut | (jnp.int32(1) << (nbits - 1 - i))
        return jnp.where(count(eq & (idx < cand)) <= need, cand, cut)

    cut = lax.fori_loop(0, nbits, index_bit, jnp.zeros((E, 1), jnp.int32))
    sel = gt | (eq & (idx < cut))
    gate_ref[...] = jnp.where(sel, a, 0.0)

    li = lax.broadcasted_iota(jnp.int32, (128, 128), 0)
    lj = lax.broadcasted_iota(jnp.int32, (128, 128), 1)
    upper = jnp.where(li <= lj, 1.0, 0.0).astype(BF16)
    carry = jnp.zeros((E, 1), F32)
    for j in range(S // 128):
        sb = jnp.where(sel[:, j * 128:(j + 1) * 128], 1.0, 0.0)
        incl = _dot(sb.astype(BF16), upper)
        pos_ref[:, j * 128:(j + 1) * 128] = (incl - sb + carry).astype(jnp.int32)
        carry = carry + incl[:, 127:128]


def _select(aff, cap):
    B, E, S = aff.shape
    spec = pl.BlockSpec((None, E, S), lambda b: (b, 0, 0))
    return pl.pallas_call(
        functools.partial(_select_kernel, cap=cap),
        grid=(B,),
        in_specs=[spec],
        out_specs=[spec, spec],
        out_shape=[jax.ShapeDtypeStruct((B, E, S), jnp.int32),
                   jax.ShapeDtypeStruct((B, E, S), F32)],
        compiler_params=_cparams(1),
        name="select",
    )(aff)


def _gather_kernel(bs_ref, x_ref, pos_ref, gate_ref, xe_ref, oh_ref,
                   *, n_sb, ts, sub, wn, cap, n_exp, eg):
    b = pl.program_id(0)
    grp = pl.program_id(1)
    tb = pl.program_id(2)
    lane_slot = lax.broadcasted_iota(jnp.int32, (wn, ts), 0)

    @pl.when(tb == 0)
    def _():
        xe_ref[...] = jnp.zeros_like(xe_ref)

    def window_start(first):
        return pl.multiple_of(jnp.minimum(first, cap - wn), 16)

    for s in range(sub):
        toks = slice(s * ts, (s + 1) * ts)
        sb = tb * sub + s

        def onehot(el, start, first):
            pos = pos_ref[el:el + 1, toks]
            wanted = jnp.where(gate_ref[el:el + 1, toks] > 0.0, pos, -1)
            key = jnp.where(pos >= first, wanted, -1)
            return jnp.where(key == start + lane_slot, 1.0, 0.0).astype(BF16)

        spans = []
        for el in range(eg):
            base = (b * n_exp + grp * eg + el) * (n_sb + 1)
            first = (bs_ref[base + sb] // 16) * 16
            start = window_start(first)
            spans.append((first, start, bs_ref[base + sb + 1]))
            oh_ref[s, el * wn:(el + 1) * wn, :] = onehot(el, start, first)
        rows = _dot(oh_ref[s], x_ref[toks, :]).astype(BF16)
        for el in range(eg):
            xe_ref[el, pl.ds(spans[el][1], wn), :] += rows[el * wn:(el + 1) * wn, :]

        for el in range(eg):
            first0, _, end = spans[el]
            n_more = jnp.maximum(end - first0 - 1, 0) // wn

            def window(w, carry):
                first = first0 + (w + 1) * wn
                start = window_start(first)
                xe_ref[el, pl.ds(start, wn), :] += _dot(onehot(el, start, first),
                                                        x_ref[toks, :]).astype(BF16)
                return carry

            lax.fori_loop(0, n_more, window, 0)


def _moe_gather(bs, xn, pos_t, gate_t, cap, tk, ts, wn, eg):
    B, S, D = xn.shape
    E = pos_t.shape[2]
    grid_spec = pltpu.PrefetchScalarGridSpec(
        num_scalar_prefetch=1,
        grid=(B, E // eg, S // tk),
        in_specs=[
            pl.BlockSpec((None, tk, D), lambda b, g, t, bs: (b, t, 0)),
            pl.BlockSpec((None, None, eg, tk), lambda b, g, t, bs: (b, t, g, 0)),
            pl.BlockSpec((None, None, eg, tk), lambda b, g, t, bs: (b, t, g, 0)),
        ],
        out_specs=pl.BlockSpec((None, eg, cap, D), lambda b, g, t, bs: (b, g, 0, 0)),
        scratch_shapes=[pltpu.VMEM((tk // ts, eg * wn, ts), BF16)],
    )
    return pl.pallas_call(
        functools.partial(_gather_kernel, n_sb=S // ts, ts=ts, sub=tk // ts, wn=wn, cap=cap,
                          n_exp=E, eg=eg),
        grid_spec=grid_spec,
        out_shape=jax.ShapeDtypeStruct((B, E, cap, D), BF16),
        compiler_params=_cparams(3),
        name="moe_gather",
    )(bs, xn, pos_t, gate_t)


def _ffn_kernel(xe_ref, wg_ref, wu_ref, wd_ref, ye_ref, wg_bf, wu_bf, wd_bf, *, fm):
    @pl.when(pl.program_id(1) == 0)
    def _():
        wg_bf[...] = wg_ref[...].astype(BF16)
        wu_bf[...] = wu_ref[...].astype(BF16)
        wd_bf[...] = wd_ref[...].astype(BF16)

    for r in range(xe_ref.shape[0] // fm):
        xe = xe_ref[r * fm:(r + 1) * fm, :]
        hid = (_silu(_dot(xe, wg_bf[...])) * _dot(xe, wu_bf[...])).astype(BF16)
        ye_ref[r * fm:(r + 1) * fm, :] = _dot(hid, wd_bf[...]).astype(BF16)


def _moe_ffn(xe, w_gate, w_up, w_down, layer, fm):
    B, E, cap, D = xe.shape
    FF = w_gate.shape[-1]
    rows = pl.BlockSpec((None, None, cap, D), lambda e, b: (b, e, 0, 0))
    return pl.pallas_call(
        functools.partial(_ffn_kernel, fm=fm),
        grid=(E, B),
        in_specs=[rows,
                  pl.BlockSpec((None, None, D, FF), lambda e, b: (layer, e, 0, 0)),
                  pl.BlockSpec((None, None, D, FF), lambda e, b: (layer, e, 0, 0)),
                  pl.BlockSpec((None, None, FF, D), lambda e, b: (layer, e, 0, 0))],
        out_specs=rows,
        out_shape=jax.ShapeDtypeStruct((B, E, cap, D), BF16),
        scratch_shapes=[pltpu.VMEM((D, FF), BF16), pltpu.VMEM((D, FF), BF16), pltpu.VMEM((FF, D), BF16)],
        compiler_params=_cparams(2),
        name="moe_ffn",
    )(xe, w_gate, w_up, w_down)


def _combine_kernel(bs_ref, h1_ref, ye_ref, pos_ref, gate_ref, fg_ref, out_ref, w_ref, y_ref,
                    *, n_sb, ts, sub, wn, cap, n_exp, final_norm):
    b = pl.program_id(0)
    tb = pl.program_id(1)
    lane_slot = lax.broadcasted_iota(jnp.int32, (wn, ts), 0)

    def window_start(first):
        return pl.multiple_of(jnp.minimum(first, cap - wn), 16)

    for s in range(sub):
        toks = slice(s * ts, (s + 1) * ts)
        sb = tb * sub + s

        def weights(e, start, first):
            pos = pos_ref[e:e + 1, toks]
            key = jnp.where(pos >= first, pos, -1)
            return jnp.where(key == start + lane_slot, gate_ref[e:e + 1, toks], 0.0).astype(BF16)

        firsts = []
        for e in range(n_exp):
            base = (b * n_exp + e) * (n_sb + 1)
            first = (bs_ref[base + sb] // 16) * 16
            start = window_start(first)
            firsts.append((first, bs_ref[base + sb + 1]))
            w_ref[s, e * wn:(e + 1) * wn, :] = weights(e, start, first)
            y_ref[s, e * wn:(e + 1) * wn, :] = ye_ref[e, pl.ds(start, wn), :]
        out_ref[toks, :] = h1_ref[toks, :] + _dot_tn(w_ref[s], y_ref[s])

        for e in range(n_exp):
            first0, end = firsts[e]
            n_more = jnp.maximum(end - first0 - 1, 0) // wn

            def window(w, carry):
                first = first0 + (w + 1) * wn
                start = window_start(first)
                out_ref[toks, :] += _dot_tn(weights(e, start, first), ye_ref[e, pl.ds(start, wn), :])
                return carry

            lax.fori_loop(0, n_more, window, 0)
    if final_norm:
        out_ref[...] = _rms(out_ref[...]) * fg_ref[...]


def _combine(bs, h1, ye, pos_t, gate_t, final_g, cap, tk, ts, wn, final_norm):
    B, S, D = h1.shape
    E = ye.shape[1]
    sub = tk // ts
    grid_spec = pltpu.PrefetchScalarGridSpec(
        num_scalar_prefetch=1,
        grid=(B, S // tk),
        in_specs=[
            pl.BlockSpec((None, tk, D), lambda b, t, bs: (b, t, 0)),
            pl.BlockSpec((None, E, cap, D), lambda b, t, bs: (b, 0, 0, 0), pipeline_mode=pl.Buffered(1)),
            pl.BlockSpec((None, None, E, tk), lambda b, t, bs: (b, t, 0, 0)),
            pl.BlockSpec((None, None, E, tk), lambda b, t, bs: (b, t, 0, 0)),
            pl.BlockSpec((1, D), lambda b, t, bs: (0, 0)),
        ],
        out_specs=pl.BlockSpec((None, tk, D), lambda b, t, bs: (b, t, 0)),
        scratch_shapes=[pltpu.VMEM((sub, E * wn, ts), BF16), pltpu.VMEM((sub, E * wn, D), BF16)],
    )
    return pl.pallas_call(
        functools.partial(_combine_kernel, n_sb=S // ts, ts=ts, sub=sub, wn=wn, cap=cap, n_exp=E,
                          final_norm=final_norm),
        grid_spec=grid_spec,
        out_shape=jax.ShapeDtypeStruct((B, S, D), F32),
        compiler_params=_cparams(2),
        name="combine",
    )(bs, h1, ye, pos_t, gate_t, final_g.reshape(1, D))


def _rope_tables(S):
    half = HEAD_DIM // 2
    inv_freq = ROPE_BASE ** (-jnp.arange(half, dtype=F32) / half)
    ang = jnp.arange(S).astype(F32)[:, None] * inv_freq[None, :]
    cos, sin = jnp.cos(ang), jnp.sin(ang)
    return jnp.concatenate([cos, cos], axis=-1), jnp.concatenate([-sin, sin], axis=-1)


def _block_starts(pos, tk, cap):
    B, E, _ = pos.shape
    bs = jnp.concatenate([pos[:, :, ::tk], jnp.full((B, E, 1), cap, jnp.int32)], axis=-1)
    return bs.reshape(-1)


def kernel(x, norm1_g, w_in, ret_norm_g, hgrn_norm_g, w_out, lower_bounds, norm2_g, w_router,
           w_gate, w_up, w_down, final_norm_g):
    B, S, D = x.shape
    depth = w_in.shape[0]
    E = w_router.shape[-1]
    cap = CAPACITY_FACTOR * S // E
    tm = min(512, S)
    tmo = min(1024, S)
    tk = min(512, S)
    ts = min(256, S)
    wn = min(64, cap)
    fm = min(512, cap)
    eg = min(8, E)
    n_tb = S // tk

    lbs = jax.nn.softmax(lower_bounds.astype(F32), axis=0)
    lbs = jnp.cumsum(lbs, axis=0) - lbs[0]
    cos, sin = _rope_tables(S)

    h = x
    for layer in range(depth):
        proj = _inproj(h, norm1_g[layer], w_in, layer, cos, sin, tm)
        r_out = _retention(proj, ret_norm_g[layer])
        h_raw = _hgrn(proj, lbs[layer], first_layer=(layer == 0))
        h1, xn, aff = _outproj(r_out, h_raw, proj, hgrn_norm_g[layer], w_out, layer,
                               h, norm2_g[layer], w_router[layer].T.astype(BF16), tmo)
        pos, gate = _select(aff, cap)
        bs = _block_starts(pos, ts, cap)
        pos_t = pos.reshape(B, E, n_tb, tk).transpose(0, 2, 1, 3)
        gate_t = gate.reshape(B, E, n_tb, tk).transpose(0, 2, 1, 3)
        xe = _moe_gather(bs, xn, pos_t, gate_t, cap, tk, ts, wn, eg)
        ye = _moe_ffn(xe, w_gate, w_up, w_down, layer, fm)
        h = _combine(bs, h1, ye, pos_t, gate_t, final_norm_g, cap, tk, ts, wn,
                     final_norm=(layer == depth - 1))
    return h
```

```python
import functools

import numpy as np
import jax
import jax.numpy as jnp
from jax import lax
from jax.experimental import pallas as pl
from jax.experimental.pallas import tpu as pltpu

F32 = jnp.float32
BF16 = jnp.bfloat16

HEAD_DIM = 128
N_HEADS = 4
GROUP = N_HEADS * HEAD_DIM
N_GROUPS = 9
ROPE_BASE = 10000.0
NORM_EPS = 1e-6
CAPACITY_FACTOR = 2

RET_CHUNK = 256
RET_PAIR = 4
HGRN_CHUNK = 128
HGRN_PAIR = 16
HGRN_LOCK = 2
HGRN_DESC = 16
LOG2E = 1.4426950408889634
VMEM_LIMIT = 58 * 1024 * 1024


def _cparams(n_axes):
    return pltpu.CompilerParams(
        dimension_semantics=("arbitrary",) * n_axes, vmem_limit_bytes=VMEM_LIMIT)


def _dot(a, b):
    return jnp.dot(a, b, preferred_element_type=F32)


def _dot_nt(a, b):
    return lax.dot_general(a, b, (((1,), (1,)), ((), ())), preferred_element_type=F32)


def _dot_tn(a, b):
    return lax.dot_general(a, b, (((0,), (0,)), ((), ())), preferred_element_type=F32)


def _silu(x):
    return x * (1.0 / (1.0 + jnp.exp(-x)))


def _rms(x):
    return x * lax.rsqrt(jnp.mean(x * x, axis=-1, keepdims=True) + NORM_EPS)


def _inproj_kernel(h_ref, g_ref, wf_ref, cos_ref, sin_ref, out_ref, w_ref):
    @pl.when(jnp.logical_and(pl.program_id(0) == 0, pl.program_id(1) == 0))
    def _():
        w_ref[...] = wf_ref[...].astype(BF16)

    xn = (_rms(h_ref[...]) * g_ref[...]).astype(BF16)
    cos = cos_ref[...]
    sin = sin_ref[...]
    scale = HEAD_DIM ** -0.5
    for j in range(N_GROUPS):
        acc = _dot(xn, w_ref[:, j * GROUP:(j + 1) * GROUP])
        if j in (0, 1):
            for hh in range(N_HEADS):
                sl = acc[:, hh * HEAD_DIM:(hh + 1) * HEAD_DIM]
                rot = sl * cos + pltpu.roll(sl, HEAD_DIM // 2, 1) * sin
                if j == 1:
                    rot = rot * scale
                out_ref[:, j * GROUP + hh * HEAD_DIM:j * GROUP + (hh + 1) * HEAD_DIM] = rot.astype(BF16)
            continue
        if j in (3, 8):
            acc = _silu(acc)
        elif j == 4:
            acc = _silu(acc) * scale
        out_ref[:, j * GROUP:(j + 1) * GROUP] = acc.astype(BF16)


def _inproj(h, g, w_in, layer, cos, sin, tm):
    B, S, D = h.shape
    ncol = w_in.shape[-1]
    return pl.pallas_call(
        _inproj_kernel,
        grid=(B, S // tm),
        in_specs=[
            pl.BlockSpec((None, tm, D), lambda b, i: (b, i, 0)),
            pl.BlockSpec((1, D), lambda b, i: (0, 0)),
            pl.BlockSpec((None, D, ncol), lambda b, i: (layer, 0, 0), pipeline_mode=pl.Buffered(1)),
            pl.BlockSpec((tm, HEAD_DIM), lambda b, i: (i, 0)),
            pl.BlockSpec((tm, HEAD_DIM), lambda b, i: (i, 0)),
        ],
        out_specs=pl.BlockSpec((None, tm, ncol), lambda b, i: (b, i, 0)),
        out_shape=jax.ShapeDtypeStruct((B, S, ncol), BF16),
        scratch_shapes=[pltpu.VMEM((D, ncol), BF16)],
        compiler_params=_cparams(2),
        name="inproj",
    )(h, g.reshape(1, D), w_in, cos, sin)


def _ret_kernel(q_ref, k_ref, v_ref, g_ref, dmat_ref, qdec_ref, kdec_ref, cdec_ref, rg_ref,
                out_ref, acc_ref, *, n_chunks):
    L = RET_CHUNK
    cdec = cdec_ref[0:1, :]
    zero_state = jnp.zeros((HEAD_DIM, HEAD_DIM), F32)

    def rows(i):
        return pl.ds(pl.multiple_of(i * L, L), L)

    def finish(sl, o):
        y = _rms(o) * rg_ref[...] * g_ref[sl, :].astype(F32)
        out_ref[sl, :] = y.astype(BF16)

    def step(i, states, second):
        state_f, state_b = states
        sfs = [rows(i * RET_PAIR + j) for j in range(RET_PAIR)]
        sbs = [rows(n_chunks - 1 - i * RET_PAIR - j) for j in range(RET_PAIR)]
        qf, kf, vf = ([r[s, :] for s in sfs] for r in (q_ref, k_ref, v_ref))
        qb, kb, vb = ([r[s, :] for s in sbs] for r in (q_ref, k_ref, v_ref))
        raw = [_dot_nt(q, k) for q, k in zip(qf, kf)]
        qdf = [(q.astype(F32) * qdec_ref[...]).astype(BF16) for q in qf]
        kdf = [(k.astype(F32) * kdec_ref[...]).astype(BF16) for k in kf]
        qdb = [(q.astype(F32) * kdec_ref[...]).astype(BF16) for q in qb]
        kdb = [(k.astype(F32) * qdec_ref[...]).astype(BF16) for k in kb]
        upd_f = [_dot_tn(k, v) for k, v in zip(kdf, vf)]
        upd_b = [_dot_tn(k, v) for k, v in zip(kdb, vb)]
        intra = [_dot((s * dmat_ref[...]).astype(BF16), v) for s, v in zip(raw, vf)]
        for j in range(RET_PAIR):
            of = intra[j] + _dot(qdf[j], state_f.astype(BF16))
            ob = _dot(qdb[j], state_b.astype(BF16))
            state_f = cdec * state_f + upd_f[j]
            state_b = cdec * state_b + upd_b[j]
            if second:
                finish(sfs[j], acc_ref[sfs[j], :] + of)
                finish(sbs[j], acc_ref[sbs[j], :] + ob)
            else:
                acc_ref[sfs[j], :] = of
                acc_ref[sbs[j], :] = ob
        return state_f, state_b

    steps = n_chunks // RET_PAIR
    states = lax.fori_loop(0, steps // 2, functools.partial(step, second=False),
                           (zero_state, zero_state))
    lax.fori_loop(steps // 2, steps, functools.partial(step, second=True), states)


def _ret_tables():
    L = RET_CHUNK
    t = np.arange(L, dtype=np.float64)
    lg = np.log1p(-(2.0 ** (-5.0 - np.arange(N_HEADS, dtype=np.float64))))
    dmat = np.exp(lg[:, None, None] * np.abs(t[:, None] - t[None, :])[None])
    qdec = np.exp(lg[:, None] * (t + 1.0))[:, :, None] * np.ones((1, 1, HEAD_DIM))
    kdec = np.exp(lg[:, None] * (L - 1.0 - t))[:, :, None] * np.ones((1, 1, HEAD_DIM))
    cdec = np.exp(lg * L)[:, None, None] * np.ones((1, 8, HEAD_DIM))
    return tuple(jnp.asarray(a, F32) for a in (dmat, qdec, kdec, cdec))


def _retention(proj, ret_g):
    B, S, _ = proj.shape
    L = RET_CHUNK
    assert S % (2 * RET_PAIR * L) == 0, "retention walks chunks from both ends, RET_PAIR at a time"
    dmat, qdec, kdec, cdec = _ret_tables()

    def col(c0):
        return pl.BlockSpec((None, S, HEAD_DIM), lambda b, h: (b, 0, c0 + h))

    def tab(r):
        return pl.BlockSpec((None, r, HEAD_DIM), lambda b, h: (h, 0, 0))

    return pl.pallas_call(
        functools.partial(_ret_kernel, n_chunks=S // L),
        grid=(B, N_HEADS),
        in_specs=[col(0), col(N_HEADS), col(2 * N_HEADS), col(3 * N_HEADS),
                  pl.BlockSpec((None, L, L), lambda b, h: (h, 0, 0)),
                  tab(L), tab(L), tab(8),
                  pl.BlockSpec((1, HEAD_DIM), lambda b, h: (0, h))],
        out_specs=pl.BlockSpec((None, S, HEAD_DIM), lambda b, h: (b, 0, h)),
        out_shape=jax.ShapeDtypeStruct((B, S, GROUP), BF16),
        scratch_shapes=[pltpu.VMEM((S, HEAD_DIM), F32)],
        compiler_params=_cparams(2),
        name="retention",
    )(proj, proj, proj, proj, dmat, qdec, kdec, cdec, ret_g.reshape(1, GROUP))


def _hgrn_gates(z, lb, first_layer):
    zf = z.astype(F32)
    e = jnp.exp(-jnp.abs(zf))
    inv = 1.0 / (1.0 + e)
    pos = zf >= 0.0
    sigm = jnp.where(pos, e * inv, inv)
    if first_layer:
        return jnp.minimum(zf, 0.0) * LOG2E - jnp.log2(1.0 + e), sigm
    sig = jnp.where(pos, inv, e * inv)
    return jnp.log2(lb + (1.0 - lb) * sig), (1.0 - lb) * sigm


def _hgrn_intra(chunks, lb, mtab_f, mtab_b, lvl, first_layer):
    L = HGRN_CHUNK
    n = len(chunks)
    qfs = [c[0].astype(F32) for c in chunks]
    gates_f = [_hgrn_gates(c[1], lb, first_layer) for c in chunks]
    gates_b = [_hgrn_gates(c[2], lb, first_layer) for c in chunks]

    def exponents(gates, mtab):
        splits = []
        for logf2, _ in gates:
            hi = logf2.astype(BF16)
            splits.append(jnp.concatenate([hi, (logf2 - hi.astype(F32)).astype(BF16)], axis=0))
        decs = []
        for a in range(0, n, 2):
            both = _dot(mtab, jnp.concatenate(splits[a:a + 2], axis=1))
            decs += [both[:, j * HEAD_DIM:(j + 1) * HEAD_DIM] for j in range(len(splits[a:a + 2]))]
        return decs

    decs_f = exponents(gates_f, mtab_f)
    decs_b = exponents(gates_b, mtab_b)
    cfs = [d[0:L, :] for d in decs_f]
    cbs = [d[0:L, :] for d in decs_b]
    kfs = [g[1] for g in gates_f]
    kbs = [g[1] for g in gates_b]

    scores = [jnp.where(lvl == -1, _dot_nt(chunks[i][0], (kfs[i] + kbs[i]).astype(BF16)), 0.0)
              for i in range(n)]
    m, level = 1, 0
    while m < L:
        for i in range(n):
            if m < 8:
                def as3(x):
                    return x.reshape(L // 8, 8, HEAD_DIM)
                pf = as3(jnp.exp2(decs_f[i][(1 + level) * L:(2 + level) * L, :]))
                pb = as3(jnp.exp2(decs_b[i][(1 + level) * L:(2 + level) * L, :]))
                sub = lax.broadcasted_iota(jnp.int32, (1, 8, HEAD_DIM), 1)
                upper = ((sub >> level) & 1) == 1
                qside = (as3(qfs[i]) * jnp.where(upper, pf, pb)).reshape(L, HEAD_DIM)
                kside = jnp.where(upper, as3(kbs[i]) * pb, as3(kfs[i]) * pf).reshape(L, HEAD_DIM)
            else:
                qslabs, kslabs = [], []
                for j in range(L // m):
                    mid = (j // 2) * 2 * m + m
                    sl = slice(j * m, (j + 1) * m)
                    if j % 2 == 1:
                        qdec = jnp.exp2(cfs[i][sl, :] - cfs[i][mid - 1:mid, :])
                        kslabs.append(kbs[i][sl, :] * jnp.exp2(cbs[i][mid:mid + 1, :] - cbs[i][sl, :]))
                    else:
                        qdec = jnp.exp2(cbs[i][sl, :] - cbs[i][mid:mid + 1, :])
                        kslabs.append(kfs[i][sl, :] * jnp.exp2(cfs[i][mid - 1:mid, :] - cfs[i][sl, :]))
                    qslabs.append(qfs[i][sl, :] * qdec)
                qside = jnp.concatenate(qslabs, axis=0)
                kside = jnp.concatenate(kslabs, axis=0)
            s = _dot_nt(qside.astype(BF16), kside.astype(BF16))
            scores[i] = jnp.where(lvl == level, s, scores[i])
        m, level = 2 * m, level + 1

    results = []
    for i in range(n):
        intra = _dot(scores[i].astype(BF16), chunks[i][3])
        cf_end = cfs[i][L - 1:L, :]
        cb_end = cbs[i][0:1, :]
        fwd = ((qfs[i] * jnp.exp2(cfs[i])).astype(BF16),
               (kfs[i] * jnp.exp2(cf_end - cfs[i])).astype(BF16), jnp.exp2(cf_end))
        bwd = ((qfs[i] * jnp.exp2(cbs[i])).astype(BF16),
               (kbs[i] * jnp.exp2(cb_end - cbs[i])).astype(BF16), jnp.exp2(cb_end))
        results.append((intra, fwd, bwd))
    return results


def _hgrn_kernel(q_ref, zf_ref, zb_ref, v_ref, lb_ref, mtabf_ref, mtabb_ref, lvl_ref,
                 out_ref, qeb_ref, keb_ref, dend_ref, *, n_chunks, first_layer):
    L = HGRN_CHUNK
    lb = lb_ref[...]
    zero_state = jnp.zeros((HEAD_DIM, HEAD_DIM), F32)

    def rows(c):
        return pl.ds(pl.multiple_of(c * L, L), L)

    def ascend(i, state):
        cs = [i * HGRN_PAIR + j for j in range(HGRN_PAIR)]
        sls = [rows(c) for c in cs]
        res = []
        for a in range(0, HGRN_PAIR, HGRN_LOCK):
            res += _hgrn_intra([(q_ref[sl, :], zf_ref[sl, :], zb_ref[sl, :], v_ref[sl, :])
                                for sl in sls[a:a + HGRN_LOCK]],
                               lb, mtabf_ref[...], mtabb_ref[...], lvl_ref[...], first_layer)
        updates = [_dot_tn(v_ref[sl, :], r[1][1]) for sl, r in zip(sls, res)]
        for c, sl, (intra, (qe_f, _, dend_f), (qe_b, ke_b, dend_b)), upd in zip(cs, sls, res, updates):
            out_ref[sl, :] = intra + _dot_nt(qe_f, state.astype(BF16))
            state = dend_f * state + upd
            qeb_ref[sl, :] = qe_b
            keb_ref[sl, :] = ke_b
            dend_ref[c] = jnp.broadcast_to(dend_b, (8, HEAD_DIM))
        return state

    lax.fori_loop(0, n_chunks // HGRN_PAIR, ascend, zero_state)

    def descend(i, state):
        cs = [n_chunks - 1 - i * HGRN_DESC - j for j in range(HGRN_DESC)]
        sls = [rows(c) for c in cs]
        updates = [_dot_tn(v_ref[sl, :], keb_ref[sl, :]) for sl in sls]
        for c, sl, upd in zip(cs, sls, updates):
            out_ref[sl, :] += _dot_nt(qeb_ref[sl, :], state.astype(BF16))
            state = dend_ref[c][0:1, :] * state + upd
        return state

    lax.fori_loop(0, n_chunks // HGRN_DESC, descend, zero_state)


def _hgrn_tables():
    L = HGRN_CHUNK
    t = np.arange(L)[:, None]
    u = np.arange(L)[None, :]
    x = t ^ u
    lvl = np.where(x == 0, -1, np.floor(np.log2(np.maximum(x, 1)))).astype(np.int32)

    def exponent_rows(backward):
        blocks = [(u >= t) if backward else (u <= t)]
        m = 1
        while m < 8:
            mid = (t // (2 * m)) * (2 * m) + m
            if backward:
                blocks.append(np.where(t < mid, (u >= t) & (u < mid), (u >= mid) & (u < t)))
            else:
                blocks.append(np.where(t >= mid, (u >= mid) & (u <= t), (u > t) & (u < mid)))
            m *= 2
        tab = np.concatenate(blocks, axis=0).astype(np.float32)
        return np.concatenate([tab, tab], axis=1)

    return (jnp.asarray(exponent_rows(False), BF16), jnp.asarray(exponent_rows(True), BF16),
            jnp.asarray(lvl, jnp.int32))


def _hgrn(proj, lb, first_layer):
    B, S, _ = proj.shape
    L = HGRN_CHUNK
    n_chunks = S // L
    assert S % L == 0 and n_chunks % HGRN_PAIR == 0 and n_chunks % HGRN_DESC == 0
    assert HGRN_PAIR % HGRN_LOCK == 0
    mtab_f, mtab_b, lvl = _hgrn_tables()

    def col(g):
        return pl.BlockSpec((None, S, HEAD_DIM), lambda b, h: (b, 0, g * N_HEADS + h))

    def full(a):
        return pl.BlockSpec(a.shape, lambda b, h: (0, 0))

    return pl.pallas_call(
        functools.partial(_hgrn_kernel, n_chunks=n_chunks, first_layer=first_layer),
        grid=(B, N_HEADS),
        in_specs=[col(4), col(5), col(6), col(7),
                  pl.BlockSpec((1, HEAD_DIM), lambda b, h: (0, h)),
                  full(mtab_f), full(mtab_b), full(lvl)],
        out_specs=pl.BlockSpec((None, S, HEAD_DIM), lambda b, h: (b, 0, h)),
        out_shape=jax.ShapeDtypeStruct((B, S, GROUP), F32),
        scratch_shapes=[pltpu.VMEM((S, HEAD_DIM), BF16), pltpu.VMEM((S, HEAD_DIM), BF16),
                        pltpu.VMEM((n_chunks, 8, HEAD_DIM), F32)],
        compiler_params=_cparams(2),
        name="hgrn",
    )(proj, proj, proj, proj, lb.reshape(1, GROUP), mtab_f, mtab_b, lvl)


def _outproj_kernel(r_ref, hraw_ref, hg_ref, hgn_ref, wf_ref, h_ref, g2_ref, wr_ref,
                    h1_ref, xn_ref, aff_ref, w_ref):
    @pl.when(jnp.logical_and(pl.program_id(0) == 0, pl.program_id(1) == 0))
    def _():
        w_ref[...] = wf_ref[...].astype(BF16)

    hn = _rms(hraw_ref[...]) * hgn_ref[...] * hg_ref[...].astype(F32)
    mix = _dot(r_ref[...], w_ref[0:GROUP, :]) + _dot(hn.astype(BF16), w_ref[GROUP:2 * GROUP, :])
    h1 = h_ref[...] + mix
    h1_ref[...] = h1
    xn = (_rms(h1) * g2_ref[...]).astype(BF16)
    xn_ref[...] = xn
    logits = _dot_nt(wr_ref[...], xn)
    mx = jnp.max(logits, axis=0, keepdims=True)
    ex = jnp.exp(logits - mx)
    aff_ref[...] = ex / jnp.sum(ex, axis=0, keepdims=True)


def _outproj(r_out, h_raw, proj, hgrn_g, w_out, layer, h, g2, wr_t_bf, tm):
    B, S, D = h.shape
    E = wr_t_bf.shape[0]
    return pl.pallas_call(
        _outproj_kernel,
        grid=(B, S // tm),
        in_specs=[
            pl.BlockSpec((None, tm, GROUP), lambda b, i: (b, i, 0)),
            pl.BlockSpec((None, tm, GROUP), lambda b, i: (b, i, 0)),
            pl.BlockSpec((None, tm, GROUP), lambda b, i: (b, i, N_GROUPS - 1)),
            pl.BlockSpec((1, GROUP), lambda b, i: (0, 0)),
            pl.BlockSpec((None, 2 * GROUP, D), lambda b, i: (layer, 0, 0), pipeline_mode=pl.Buffered(1)),
            pl.BlockSpec((None, tm, D), lambda b, i: (b, i, 0)),
            pl.BlockSpec((1, D), lambda b, i: (0, 0)),
            pl.BlockSpec((E, D), lambda b, i: (0, 0)),
        ],
        out_specs=[
            pl.BlockSpec((None, tm, D), lambda b, i: (b, i, 0)),
            pl.BlockSpec((None, tm, D), lambda b, i: (b, i, 0)),
            pl.BlockSpec((None, E, tm), lambda b, i: (b, 0, i)),
        ],
        out_shape=[jax.ShapeDtypeStruct((B, S, D), F32),
                   jax.ShapeDtypeStruct((B, S, D), BF16),
                   jax.ShapeDtypeStruct((B, E, S), F32)],
        scratch_shapes=[pltpu.VMEM((2 * GROUP, D), BF16)],
        compiler_params=_cparams(2),
        name="outproj",
    )(r_out, h_raw, proj, hgrn_g.reshape(1, GROUP), w_out, h, g2.reshape(1, D), wr_t_bf)


def _select_kernel(aff_ref, pos_ref, gate_ref, *, cap):
    a = aff_ref[...]
    E, S = a.shape
    u = pltpu.bitcast(a, jnp.int32)
    capf = jnp.float32(cap)

    def count(mask):
        return jnp.sum(jnp.where(mask, 1.0, 0.0), axis=-1, keepdims=True)

    def value_bit(i, thr):
        cand = thr | (jnp.int32(1) << (30 - i))
        return jnp.where(count(u >= cand) >= capf, cand, thr)

    thr = lax.fori_loop(0, 31, value_bit, jnp.zeros((E, 1), jnp.int32))
    gt = u > thr
    eq = u == thr
    need = capf - count(gt)
    idx = lax.broadcasted_iota(jnp.int32, (E, S), 1)
    nbits = int(S).bit_length()

    def index_bit(i, cut):
        cand = cut | (jnp.int32(1) << (nbits - 1 - i))
        return jnp.where(count(eq & (idx < cand)) <= need, cand, cut)

    cut = lax.fori_loop(0, nbits, index_bit, jnp.zeros((E, 1), jnp.int32))
    sel = gt | (eq & (idx < cut))
    gate_ref[...] = jnp.where(sel, a, 0.0)

    li = lax.broadcasted_iota(jnp.int32, (128, 128), 0)
    lj = lax.broadcasted_iota(jnp.int32, (128, 128), 1)
    upper = jnp.where(li <= lj, 1.0, 0.0).astype(BF16)
    carry = jnp.zeros((E, 1), F32)
    for j in range(S // 128):
        sb = jnp.where(sel[:, j * 128:(j + 1) * 128], 1.0, 0.0)
        incl = _dot(sb.astype(BF16), upper)
        pos_ref[:, j * 128:(j + 1) * 128] = (incl - sb + carry).astype(jnp.int32)
        carry = carry + incl[:, 127:128]


def _select(aff, cap):
    B, E, S = aff.shape
    spec = pl.BlockSpec((None, E, S), lambda b: (b, 0, 0))
    return pl.pallas_call(
        functools.partial(_select_kernel, cap=cap),
        grid=(B,),
        in_specs=[spec],
        out_specs=[spec, spec],
        out_shape=[jax.ShapeDtypeStruct((B, E, S), jnp.int32),
                   jax.ShapeDtypeStruct((B, E, S), F32)],
        compiler_params=_cparams(1),
        name="select",
    )(aff)


def _gather_kernel(bs_ref, x_ref, pos_ref, gate_ref, xe_ref, oh_ref,
                   *, n_sb, ts, sub, wn, cap, n_exp, eg):
    b = pl.program_id(0)
    grp = pl.program_id(1)
    tb = pl.program_id(2)
    lane_slot = lax.broadcasted_iota(jnp.int32, (wn, ts), 0)

    @pl.when(tb == 0)
    def _():
        xe_ref[...] = jnp.zeros_like(xe_ref)

    def window_start(first):
        return pl.multiple_of(jnp.minimum(first, cap - wn), 16)

    for s in range(sub):
        toks = slice(s * ts, (s + 1) * ts)
        sb = tb * sub + s

        def onehot(el, start, first):
            pos = pos_ref[el:el + 1, toks]
            wanted = jnp.where(gate_ref[el:el + 1, toks] > 0.0, pos, -1)
            key = jnp.where(pos >= first, wanted, -1)
            return jnp.where(key == start + lane_slot, 1.0, 0.0).astype(BF16)

        spans = []
        for el in range(eg):
            base = (b * n_exp + grp * eg + el) * (n_sb + 1)
            first = (bs_ref[base + sb] // 16) * 16
            start = window_start(first)
            spans.append((first, start, bs_ref[base + sb + 1]))
            oh_ref[s, el * wn:(el + 1) * wn, :] = onehot(el, start, first)
        rows = _dot(oh_ref[s], x_ref[toks, :]).astype(BF16)
        for el in range(eg):
            xe_ref[el, pl.ds(spans[el][1], wn), :] += rows[el * wn:(el + 1) * wn, :]

        for el in range(eg):
            first0, _, end = spans[el]
            n_more = jnp.maximum(end - first0 - 1, 0) // wn

            def window(w, carry):
                first = first0 + (w + 1) * wn
                start = window_start(first)
                xe_ref[el, pl.ds(start, wn), :] += _dot(onehot(el, start, first),
                                                        x_ref[toks, :]).astype(BF16)
                return carry

            lax.fori_loop(0, n_more, window, 0)


def _moe_gather(bs, xn, pos_t, gate_t, cap, tk, ts, wn, eg):
    B, S, D = xn.shape
    E = pos_t.shape[2]
    grid_spec = pltpu.PrefetchScalarGridSpec(
        num_scalar_prefetch=1,
        grid=(B, E // eg, S // tk),
        in_specs=[
            pl.BlockSpec((None, tk, D), lambda b, g, t, bs: (b, t, 0)),
            pl.BlockSpec((None, None, eg, tk), lambda b, g, t, bs: (b, t, g, 0)),
            pl.BlockSpec((None, None, eg, tk), lambda b, g, t, bs: (b, t, g, 0)),
        ],
        out_specs=pl.BlockSpec((None, eg, cap, D), lambda b, g, t, bs: (b, g, 0, 0)),
        scratch_shapes=[pltpu.VMEM((tk // ts, eg * wn, ts), BF16)],
    )
    return pl.pallas_call(
        functools.partial(_gather_kernel, n_sb=S // ts, ts=ts, sub=tk // ts, wn=wn, cap=cap,
                          n_exp=E, eg=eg),
        grid_spec=grid_spec,
        out_shape=jax.ShapeDtypeStruct((B, E, cap, D), BF16),
        compiler_params=_cparams(3),
        name="moe_gather",
    )(bs, xn, pos_t, gate_t)


def _ffn_kernel(xe_ref, wg_ref, wu_ref, wd_ref, ye_ref, wg_bf, wu_bf, wd_bf, *, fm):
    @pl.when(pl.program_id(1) == 0)
    def _():
        wg_bf[...] = wg_ref[...].astype(BF16)
        wu_bf[...] = wu_ref[...].astype(BF16)
        wd_bf[...] = wd_ref[...].astype(BF16)

    for r in range(xe_ref.shape[0] // fm):
        xe = xe_ref[r * fm:(r + 1) * fm, :]
        hid = (_silu(_dot(xe, wg_bf[...])) * _dot(xe, wu_bf[...])).astype(BF16)
        ye_ref[r * fm:(r + 1) * fm, :] = _dot(hid, wd_bf[...]).astype(BF16)


def _moe_ffn(xe, w_gate, w_up, w_down, layer, fm):
    B, E, cap, D = xe.shape
    FF = w_gate.shape[-1]
    rows = pl.BlockSpec((None, None, cap, D), lambda e, b: (b, e, 0, 0))
    return pl.pallas_call(
        functools.partial(_ffn_kernel, fm=fm),
        grid=(E, B),
        in_specs=[rows,
                  pl.BlockSpec((None, None, D, FF), lambda e, b: (layer, e, 0, 0)),
                  pl.BlockSpec((None, None, D, FF), lambda e, b: (layer, e, 0, 0)),
                  pl.BlockSpec((None, None, FF, D), lambda e, b: (layer, e, 0, 0))],
        out_specs=rows,
        out_shape=jax.ShapeDtypeStruct((B, E, cap, D), BF16),
        scratch_shapes=[pltpu.VMEM((D, FF), BF16), pltpu.VMEM((D, FF), BF16), pltpu.VMEM((FF, D), BF16)],
        compiler_params=_cparams(2),
        name="moe_ffn",
    )(xe, w_gate, w_up, w_down)


def _combine_kernel(bs_ref, h1_ref, ye_ref, pos_ref, gate_ref, fg_ref, out_ref, w_ref, y_ref,
                    *, n_sb, ts, sub, wn, cap, n_exp, final_norm):
    b = pl.program_id(0)
    tb = pl.program_id(1)
    lane_slot = lax.broadcasted_iota(jnp.int32, (wn, ts), 0)

    def window_start(first):
        return pl.multiple_of(jnp.minimum(first, cap - wn), 16)

    for s in range(sub):
        toks = slice(s * ts, (s + 1) * ts)
        sb = tb * sub + s

        def weights(e, start, first):
            pos = pos_ref[e:e + 1, toks]
            key = jnp.where(pos >= first, pos, -1)
            return jnp.where(key == start + lane_slot, gate_ref[e:e + 1, toks], 0.0).astype(BF16)

        firsts = []
        for e in range(n_exp):
            base = (b * n_exp + e) * (n_sb + 1)
            first = (bs_ref[base + sb] // 16) * 16
            start = window_start(first)
            firsts.append((first, bs_ref[base + sb + 1]))
            w_ref[s, e * wn:(e + 1) * wn, :] = weights(e, start, first)
            y_ref[s, e * wn:(e + 1) * wn, :] = ye_ref[e, pl.ds(start, wn), :]
        out_ref[toks, :] = h1_ref[toks, :] + _dot_tn(w_ref[s], y_ref[s])

        for e in range(n_exp):
            first0, end = firsts[e]
            n_more = jnp.maximum(end - first0 - 1, 0) // wn

            def window(w, carry):
                first = first0 + (w + 1) * wn
                start = window_start(first)
                out_ref[toks, :] += _dot_tn(weights(e, start, first), ye_ref[e, pl.ds(start, wn), :])
                return carry

            lax.fori_loop(0, n_more, window, 0)
    if final_norm:
        out_ref[...] = _rms(out_ref[...]) * fg_ref[...]


def _combine(bs, h1, ye, pos_t, gate_t, final_g, cap, tk, ts, wn, final_norm):
    B, S, D = h1.shape
    E = ye.shape[1]
    sub = tk // ts
    grid_spec = pltpu.PrefetchScalarGridSpec(
        num_scalar_prefetch=1,
        grid=(B, S // tk),
        in_specs=[
            pl.BlockSpec((None, tk, D), lambda b, t, bs: (b, t, 0)),
            pl.BlockSpec((None, E, cap, D), lambda b, t, bs: (b, 0, 0, 0), pipeline_mode=pl.Buffered(1)),
            pl.BlockSpec((None, None, E, tk), lambda b, t, bs: (b, t, 0, 0)),
            pl.BlockSpec((None, None, E, tk), lambda b, t, bs: (b, t, 0, 0)),
            pl.BlockSpec((1, D), lambda b, t, bs: (0, 0)),
        ],
        out_specs=pl.BlockSpec((None, tk, D), lambda b, t, bs: (b, t, 0)),
        scratch_shapes=[pltpu.VMEM((sub, E * wn, ts), BF16), pltpu.VMEM((sub, E * wn, D), BF16)],
    )
    return pl.pallas_call(
        functools.partial(_combine_kernel, n_sb=S // ts, ts=ts, sub=sub, wn=wn, cap=cap, n_exp=E,
                          final_norm=final_norm),
        grid_spec=grid_spec,
        out_shape=jax.ShapeDtypeStruct((B, S, D), F32),
        compiler_params=_cparams(2),
        name="combine",
    )(bs, h1, ye, pos_t, gate_t, final_g.reshape(1, D))


def _rope_tables(S):
    half = HEAD_DIM // 2
    inv_freq = ROPE_BASE ** (-jnp.arange(half, dtype=F32) / half)
    ang = jnp.arange(S).astype(F32)[:, None] * inv_freq[None, :]
    cos, sin = jnp.cos(ang), jnp.sin(ang)
    return jnp.concatenate([cos, cos], axis=-1), jnp.concatenate([-sin, sin], axis=-1)


def _block_starts(pos, tk, cap):
    B, E, _ = pos.shape
    bs = jnp.concatenate([pos[:, :, ::tk], jnp.full((B, E, 1), cap, jnp.int32)], axis=-1)
    return bs.reshape(-1)


def kernel(x, norm1_g, w_in, ret_norm_g, hgrn_norm_g, w_out, lower_bounds, norm2_g, w_router,
           w_gate, w_up, w_down, final_norm_g):
    B, S, D = x.shape
    depth = w_in.shape[0]
    E = w_router.shape[-1]
    cap = CAPACITY_FACTOR * S // E
    tm = min(512, S)
    tmo = min(1024, S)
    tk = min(512, S)
    ts = min(256, S)
    wn = min(64, cap)
    fm = min(512, cap)
    eg = min(8, E)
    n_tb = S // tk

    lbs = jax.nn.softmax(lower_bounds.astype(F32), axis=0)
    lbs = jnp.cumsum(lbs, axis=0) - lbs[0]
    cos, sin = _rope_tables(S)

    h = x
    for layer in range(depth):
        proj = _inproj(h, norm1_g[layer], w_in, layer, cos, sin, tm)
        r_out = _retention(proj, ret_norm_g[layer])
        h_raw = _hgrn(proj, lbs[layer], first_layer=(layer == 0))
        h1, xn, aff = _outproj(r_out, h_raw, proj, hgrn_norm_g[layer], w_out, layer,
                               h, norm2_g[layer], w_router[layer].T.astype(BF16), tmo)
        pos, gate = _select(aff, cap)
        bs = _block_starts(pos, ts, cap)
        pos_t = pos.reshape(B, E, n_tb, tk).transpose(0, 2, 1, 3)
        gate_t = gate.reshape(B, E, n_tb, tk).transpose(0, 2, 1, 3)
        xe = _moe_gather(bs, xn, pos_t, gate_t, cap, tk, ts, wn, eg)
        ye = _moe_ffn(xe, w_gate, w_up, w_down, layer, fm)
        h = _combine(bs, h1, ye, pos_t, gate_t, final_norm_g, cap, tk, ts, wn,
                     final_norm=(layer == depth - 1))
    return h
```

```python
import functools

import numpy as np
import jax
import jax.numpy as jnp
from jax import lax
from jax.experimental import pallas as pl
from jax.experimental.pallas import tpu as pltpu

F32 = jnp.float32
BF16 = jnp.bfloat16

HEAD_DIM = 128
N_HEADS = 4
GROUP = N_HEADS * HEAD_DIM
N_GROUPS = 9
ROPE_BASE = 10000.0
NORM_EPS = 1e-6
CAPACITY_FACTOR = 2

RET_CHUNK = 256
RET_PAIR = 4
HGRN_CHUNK = 128
HGRN_PAIR = 16
HGRN_LOCK = 2
HGRN_DESC = 16
LOG2E = 1.4426950408889634
VMEM_LIMIT = 58 * 1024 * 1024


def _cparams(n_axes):
    return pltpu.CompilerParams(
        dimension_semantics=("arbitrary",) * n_axes, vmem_limit_bytes=VMEM_LIMIT)


def _dot(a, b):
    return jnp.dot(a, b, preferred_element_type=F32)


def _dot_nt(a, b):
    return lax.dot_general(a, b, (((1,), (1,)), ((), ())), preferred_element_type=F32)


def _dot_tn(a, b):
    return lax.dot_general(a, b, (((0,), (0,)), ((), ())), preferred_element_type=F32)


def _silu(x):
    return x * (1.0 / (1.0 + jnp.exp(-x)))


def _rms(x):
    return x * lax.rsqrt(jnp.mean(x * x, axis=-1, keepdims=True) + NORM_EPS)


def _inproj_kernel(h_ref, g_ref, wf_ref, cos_ref, sin_ref, out_ref, w_ref):
    @pl.when(jnp.logical_and(pl.program_id(0) == 0, pl.program_id(1) == 0))
    def _():
        w_ref[...] = wf_ref[...].astype(BF16)

    xn = (_rms(h_ref[...]) * g_ref[...]).astype(BF16)
    cos = cos_ref[...]
    sin = sin_ref[...]
    scale = HEAD_DIM ** -0.5
    for j in range(N_GROUPS):
        acc = _dot(xn, w_ref[:, j * GROUP:(j + 1) * GROUP])
        if j in (0, 1):
            for hh in range(N_HEADS):
                sl = acc[:, hh * HEAD_DIM:(hh + 1) * HEAD_DIM]
                rot = sl * cos + pltpu.roll(sl, HEAD_DIM // 2, 1) * sin
                if j == 1:
                    rot = rot * scale
                out_ref[:, j * GROUP + hh * HEAD_DIM:j * GROUP + (hh + 1) * HEAD_DIM] = rot.astype(BF16)
            continue
        if j in (3, 8):
            acc = _silu(acc)
        elif j == 4:
            acc = _silu(acc) * scale
        out_ref[:, j * GROUP:(j + 1) * GROUP] = acc.astype(BF16)


def _inproj(h, g, w_in, layer, cos, sin, tm):
    B, S, D = h.shape
    ncol = w_in.shape[-1]
    return pl.pallas_call(
        _inproj_kernel,
        grid=(B, S // tm),
        in_specs=[
            pl.BlockSpec((None, tm, D), lambda b, i: (b, i, 0)),
            pl.BlockSpec((1, D), lambda b, i: (0, 0)),
            pl.BlockSpec((None, D, ncol), lambda b, i: (layer, 0, 0), pipeline_mode=pl.Buffered(1)),
            pl.BlockSpec((tm, HEAD_DIM), lambda b, i: (i, 0)),
            pl.BlockSpec((tm, HEAD_DIM), lambda b, i: (i, 0)),
        ],
        out_specs=pl.BlockSpec((None, tm, ncol), lambda b, i: (b, i, 0)),
        out_shape=jax.ShapeDtypeStruct((B, S, ncol), BF16),
        scratch_shapes=[pltpu.VMEM((D, ncol), BF16)],
        compiler_params=_cparams(2),
        name="inproj",
    )(h, g.reshape(1, D), w_in, cos, sin)


def _ret_kernel(q_ref, k_ref, v_ref, g_ref, dmat_ref, qdec_ref, kdec_ref, cdec_ref, rg_ref,
                out_ref, acc_ref, *, n_chunks):
    L = RET_CHUNK
    cdec = cdec_ref[0:1, :]
    zero_state = jnp.zeros((HEAD_DIM, HEAD_DIM), F32)

    def rows(i):
        return pl.ds(pl.multiple_of(i * L, L), L)

    def finish(sl, o):
        y = _rms(o) * rg_ref[...] * g_ref[sl, :].astype(F32)
        out_ref[sl, :] = y.astype(BF16)

    def step(i, states, second):
        state_f, state_b = states
        sfs = [rows(i * RET_PAIR + j) for j in range(RET_PAIR)]
        sbs = [rows(n_chunks - 1 - i * RET_PAIR - j) for j in range(RET_PAIR)]
        qf, kf, vf = ([r[s, :] for s in sfs] for r in (q_ref, k_ref, v_ref))
        qb, kb, vb = ([r[s, :] for s in sbs] for r in (q_ref, k_ref, v_ref))
        raw = [_dot_nt(q, k) for q, k in zip(qf, kf)]
        qdf = [(q.astype(F32) * qdec_ref[...]).astype(BF16) for q in qf]
        kdf = [(k.astype(F32) * kdec_ref[...]).astype(BF16) for k in kf]
        qdb = [(q.astype(F32) * kdec_ref[...]).astype(BF16) for q in qb]
        kdb = [(k.astype(F32) * qdec_ref[...]).astype(BF16) for k in kb]
        upd_f = [_dot_tn(k, v) for k, v in zip(kdf, vf)]
        upd_b = [_dot_tn(k, v) for k, v in zip(kdb, vb)]
        intra = [_dot((s * dmat_ref[...]).astype(BF16), v) for s, v in zip(raw, vf)]
        for j in range(RET_PAIR):
            of = intra[j] + _dot(qdf[j], state_f.astype(BF16))
            ob = _dot(qdb[j], state_b.astype(BF16))
            state_f = cdec * state_f + upd_f[j]
            state_b = cdec * state_b + upd_b[j]
            if second:
                finish(sfs[j], acc_ref[sfs[j], :] + of)
                finish(sbs[j], acc_ref[sbs[j], :] + ob)
            else:
                acc_ref[sfs[j], :] = of
                acc_ref[sbs[j], :] = ob
        return state_f, state_b

    steps = n_chunks // RET_PAIR
    states = lax.fori_loop(0, steps // 2, functools.partial(step, second=False),
                           (zero_state, zero_state))
    lax.fori_loop(steps // 2, steps, functools.partial(step, second=True), states)


def _ret_tables():
    L = RET_CHUNK
    t = np.arange(L, dtype=np.float64)
    lg = np.log1p(-(2.0 ** (-5.0 - np.arange(N_HEADS, dtype=np.float64))))
    dmat = np.exp(lg[:, None, None] * np.abs(t[:, None] - t[None, :])[None])
    qdec = np.exp(lg[:, None] * (t + 1.0))[:, :, None] * np.ones((1, 1, HEAD_DIM))
    kdec = np.exp(lg[:, None] * (L - 1.0 - t))[:, :, None] * np.ones((1, 1, HEAD_DIM))
    cdec = np.exp(lg * L)[:, None, None] * np.ones((1, 8, HEAD_DIM))
    return tuple(jnp.asarray(a, F32) for a in (dmat, qdec, kdec, cdec))


def _retention(proj, ret_g):
    B, S, _ = proj.shape
    L = RET_CHUNK
    assert S % (2 * RET_PAIR * L) == 0, "retention walks chunks from both ends, RET_PAIR at a time"
    dmat, qdec, kdec, cdec = _ret_tables()

    def col(c0):
        return pl.BlockSpec((None, S, HEAD_DIM), lambda b, h: (b, 0, c0 + h))

    def tab(r):
        return pl.BlockSpec((None, r, HEAD_DIM), lambda b, h: (h, 0, 0))

    return pl.pallas_call(
        functools.partial(_ret_kernel, n_chunks=S // L),
        grid=(B, N_HEADS),
        in_specs=[col(0), col(N_HEADS), col(2 * N_HEADS), col(3 * N_HEADS),
                  pl.BlockSpec((None, L, L), lambda b, h: (h, 0, 0)),
                  tab(L), tab(L), tab(8),
                  pl.BlockSpec((1, HEAD_DIM), lambda b, h: (0, h))],
        out_specs=pl.BlockSpec((None, S, HEAD_DIM), lambda b, h: (b, 0, h)),
        out_shape=jax.ShapeDtypeStruct((B, S, GROUP), BF16),
        scratch_shapes=[pltpu.VMEM((S, HEAD_DIM), F32)],
        compiler_params=_cparams(2),
        name="retention",
    )(proj, proj, proj, proj, dmat, qdec, kdec, cdec, ret_g.reshape(1, GROUP))


def _hgrn_gates(z, lb, first_layer):
    zf = z.astype(F32)
    e = jnp.exp(-jnp.abs(zf))
    inv = 1.0 / (1.0 + e)
    pos = zf >= 0.0
    sigm = jnp.where(pos, e * inv, inv)
    if first_layer:
        return jnp.minimum(zf, 0.0) * LOG2E - jnp.log2(1.0 + e), sigm
    sig = jnp.where(pos, inv, e * inv)
    return jnp.log2(lb + (1.0 - lb) * sig), (1.0 - lb) * sigm


def _hgrn_intra(chunks, lb, mtab_f, mtab_b, lvl, first_layer):
    L = HGRN_CHUNK
    n = len(chunks)
    qfs = [c[0].astype(F32) for c in chunks]
    gates_f = [_hgrn_gates(c[1], lb, first_layer) for c in chunks]
    gates_b = [_hgrn_gates(c[2], lb, first_layer) for c in chunks]

    def exponents(gates, mtab):
        splits = []
        for logf2, _ in gates:
            hi = logf2.astype(BF16)
            splits.append(jnp.concatenate([hi, (logf2 - hi.astype(F32)).astype(BF16)], axis=0))
        decs = []
        for a in range(0, n, 2):
            both = _dot(mtab, jnp.concatenate(splits[a:a + 2], axis=1))
            decs += [both[:, j * HEAD_DIM:(j + 1) * HEAD_DIM] for j in range(len(splits[a:a + 2]))]
        return decs

    decs_f = exponents(gates_f, mtab_f)
    decs_b = exponents(gates_b, mtab_b)
    cfs = [d[0:L, :] for d in decs_f]
    cbs = [d[0:L, :] for d in decs_b]
    kfs = [g[1] for g in gates_f]
    kbs = [g[1] for g in gates_b]

    scores = [jnp.where(lvl == -1, _dot_nt(chunks[i][0], (kfs[i] + kbs[i]).astype(BF16)), 0.0)
              for i in range(n)]
    m, level = 1, 0
    while m < L:
        for i in range(n):
            if m < 8:
                def as3(x):
                    return x.reshape(L // 8, 8, HEAD_DIM)
                pf = as3(jnp.exp2(decs_f[i][(1 + level) * L:(2 + level) * L, :]))
                pb = as3(jnp.exp2(decs_b[i][(1 + level) * L:(2 + level) * L, :]))
                sub = lax.broadcasted_iota(jnp.int32, (1, 8, HEAD_DIM), 1)
                upper = ((sub >> level) & 1) == 1
                qside = (as3(qfs[i]) * jnp.where(upper, pf, pb)).reshape(L, HEAD_DIM)
                kside = jnp.where(upper, as3(kbs[i]) * pb, as3(kfs[i]) * pf).reshape(L, HEAD_DIM)
            else:
                qslabs, kslabs = [], []
                for j in range(L // m):
                    mid = (j // 2) * 2 * m + m
                    sl = slice(j * m, (j + 1) * m)
                    if j % 2 == 1:
                        qdec = jnp.exp2(cfs[i][sl, :] - cfs[i][mid - 1:mid, :])
                        kslabs.append(kbs[i][sl, :] * jnp.exp2(cbs[i][mid:mid + 1, :] - cbs[i][sl, :]))
                    else:
                        qdec = jnp.exp2(cbs[i][sl, :] - cbs[i][mid:mid + 1, :])
                        kslabs.append(kfs[i][sl, :] * jnp.exp2(cfs[i][mid - 1:mid, :] - cfs[i][sl, :]))
                    qslabs.append(qfs[i][sl, :] * qdec)
                qside = jnp.concatenate(qslabs, axis=0)
                kside = jnp.concatenate(kslabs, axis=0)
            s = _dot_nt(qside.astype(BF16), kside.astype(BF16))
            scores[i] = jnp.where(lvl == level, s, scores[i])
        m, level = 2 * m, level + 1

    results = []
    for i in range(n):
        intra = _dot(scores[i].astype(BF16), chunks[i][3])
        cf_end = cfs[i][L - 1:L, :]
        cb_end = cbs[i][0:1, :]
        fwd = ((qfs[i] * jnp.exp2(cfs[i])).astype(BF16),
               (kfs[i] * jnp.exp2(cf_end - cfs[i])).astype(BF16), jnp.exp2(cf_end))
        bwd = ((qfs[i] * jnp.exp2(cbs[i])).astype(BF16),
               (kbs[i] * jnp.exp2(cb_end - cbs[i])).astype(BF16), jnp.exp2(cb_end))
        results.append((intra, fwd, bwd))
    return results


def _hgrn_kernel(q_ref, zf_ref, zb_ref, v_ref, lb_ref, mtabf_ref, mtabb_ref, lvl_ref,
                 out_ref, qeb_ref, keb_ref, dend_ref, *, n_chunks, first_layer):
    L = HGRN_CHUNK
    lb = lb_ref[...]
    zero_state = jnp.zeros((HEAD_DIM, HEAD_DIM), F32)

    def rows(c):
        return pl.ds(pl.multiple_of(c * L, L), L)

    def ascend(i, state):
        cs = [i * HGRN_PAIR + j for j in range(HGRN_PAIR)]
        sls = [rows(c) for c in cs]
        res = []
        for a in range(0, HGRN_PAIR, HGRN_LOCK):
            res += _hgrn_intra([(q_ref[sl, :], zf_ref[sl, :], zb_ref[sl, :], v_ref[sl, :])
                                for sl in sls[a:a + HGRN_LOCK]],
                               lb, mtabf_ref[...], mtabb_ref[...], lvl_ref[...], first_layer)
        updates = [_dot_tn(v_ref[sl, :], r[1][1]) for sl, r in zip(sls, res)]
        for c, sl, (intra, (qe_f, _, dend_f), (qe_b, ke_b, dend_b)), upd in zip(cs, sls, res, updates):
            out_ref[sl, :] = intra + _dot_nt(qe_f, state.astype(BF16))
            state = dend_f * state + upd
            qeb_ref[sl, :] = qe_b
            keb_ref[sl, :] = ke_b
            dend_ref[c] = jnp.broadcast_to(dend_b, (8, HEAD_DIM))
        return state

    lax.fori_loop(0, n_chunks // HGRN_PAIR, ascend, zero_state)

    def descend(i, state):
        cs = [n_chunks - 1 - i * HGRN_DESC - j for j in range(HGRN_DESC)]
        sls = [rows(c) for c in cs]
        updates = [_dot_tn(v_ref[sl, :], keb_ref[sl, :]) for sl in sls]
        for c, sl, upd in zip(cs, sls, updates):
            out_ref[sl, :] += _dot_nt(qeb_ref[sl, :], state.astype(BF16))
            state = dend_ref[c][0:1, :] * state + upd
        return state

    lax.fori_loop(0, n_chunks // HGRN_DESC, descend, zero_state)


def _hgrn_tables():
    L = HGRN_CHUNK
    t = np.arange(L)[:, None]
    u = np.arange(L)[None, :]
    x = t ^ u
    lvl = np.where(x == 0, -1, np.floor(np.log2(np.maximum(x, 1)))).astype(np.int32)

    def exponent_rows(backward):
        blocks = [(u >= t) if backward else (u <= t)]
        m = 1
        while m < 8:
            mid = (t // (2 * m)) * (2 * m) + m
            if backward:
                blocks.append(np.where(t < mid, (u >= t) & (u < mid), (u >= mid) & (u < t)))
            else:
                blocks.append(np.where(t >= mid, (u >= mid) & (u <= t), (u > t) & (u < mid)))
            m *= 2
        tab = np.concatenate(blocks, axis=0).astype(np.float32)
        return np.concatenate([tab, tab], axis=1)

    return (jnp.asarray(exponent_rows(False), BF16), jnp.asarray(exponent_rows(True), BF16),
            jnp.asarray(lvl, jnp.int32))


def _hgrn(proj, lb, first_layer):
    B, S, _ = proj.shape
    L = HGRN_CHUNK
    n_chunks = S // L
    assert S % L == 0 and n_chunks % HGRN_PAIR == 0 and n_chunks % HGRN_DESC == 0
    assert HGRN_PAIR % HGRN_LOCK == 0
    mtab_f, mtab_b, lvl = _hgrn_tables()

    def col(g):
        return pl.BlockSpec((None, S, HEAD_DIM), lambda b, h: (b, 0, g * N_HEADS + h))

    def full(a):
        return pl.BlockSpec(a.shape, lambda b, h: (0, 0))

    return pl.pallas_call(
        functools.partial(_hgrn_kernel, n_chunks=n_chunks, first_layer=first_layer),
        grid=(B, N_HEADS),
        in_specs=[col(4), col(5), col(6), col(7),
                  pl.BlockSpec((1, HEAD_DIM), lambda b, h: (0, h)),
                  full(mtab_f), full(mtab_b), full(lvl)],
        out_specs=pl.BlockSpec((None, S, HEAD_DIM), lambda b, h: (b, 0, h)),
        out_shape=jax.ShapeDtypeStruct((B, S, GROUP), F32),
        scratch_shapes=[pltpu.VMEM((S, HEAD_DIM), BF16), pltpu.VMEM((S, HEAD_DIM), BF16),
                        pltpu.VMEM((n_chunks, 8, HEAD_DIM), F32)],
        compiler_params=_cparams(2),
        name="hgrn",
    )(proj, proj, proj, proj, lb.reshape(1, GROUP), mtab_f, mtab_b, lvl)


def _outproj_kernel(r_ref, hraw_ref, hg_ref, hgn_ref, wf_ref, h_ref, g2_ref, wr_ref,
                    h1_ref, xn_ref, aff_ref, w_ref):
    @pl.when(jnp.logical_and(pl.program_id(0) == 0, pl.program_id(1) == 0))
    def _():
        w_ref[...] = wf_ref[...].astype(BF16)

    hn = _rms(hraw_ref[...]) * hgn_ref[...] * hg_ref[...].astype(F32)
    mix = _dot(r_ref[...], w_ref[0:GROUP, :]) + _dot(hn.astype(BF16), w_ref[GROUP:2 * GROUP, :])
    h1 = h_ref[...] + mix
    h1_ref[...] = h1
    xn = (_rms(h1) * g2_ref[...]).astype(BF16)
    xn_ref[...] = xn
    logits = _dot_nt(wr_ref[...], xn)
    mx = jnp.max(logits, axis=0, keepdims=True)
    ex = jnp.exp(logits - mx)
    aff_ref[...] = ex / jnp.sum(ex, axis=0, keepdims=True)


def _outproj(r_out, h_raw, proj, hgrn_g, w_out, layer, h, g2, wr_t_bf, tm):
    B, S, D = h.shape
    E = wr_t_bf.shape[0]
    return pl.pallas_call(
        _outproj_kernel,
        grid=(B, S // tm),
        in_specs=[
            pl.BlockSpec((None, tm, GROUP), lambda b, i: (b, i, 0)),
            pl.BlockSpec((None, tm, GROUP), lambda b, i: (b, i, 0)),
            pl.BlockSpec((None, tm, GROUP), lambda b, i: (b, i, N_GROUPS - 1)),
            pl.BlockSpec((1, GROUP), lambda b, i: (0, 0)),
            pl.BlockSpec((None, 2 * GROUP, D), lambda b, i: (layer, 0, 0), pipeline_mode=pl.Buffered(1)),
            pl.BlockSpec((None, tm, D), lambda b, i: (b, i, 0)),
            pl.BlockSpec((1, D), lambda b, i: (0, 0)),
            pl.BlockSpec((E, D), lambda b, i: (0, 0)),
        ],
        out_specs=[
            pl.BlockSpec((None, tm, D), lambda b, i: (b, i, 0)),
            pl.BlockSpec((None, tm, D), lambda b, i: (b, i, 0)),
            pl.BlockSpec((None, E, tm), lambda b, i: (b, 0, i)),
        ],
        out_shape=[jax.ShapeDtypeStruct((B, S, D), F32),
                   jax.ShapeDtypeStruct((B, S, D), BF16),
                   jax.ShapeDtypeStruct((B, E, S), F32)],
        scratch_shapes=[pltpu.VMEM((2 * GROUP, D), BF16)],
        compiler_params=_cparams(2),
        name="outproj",
    )(r_out, h_raw, proj, hgrn_g.reshape(1, GROUP), w_out, h, g2.reshape(1, D), wr_t_bf)


def _select_kernel(aff_ref, pos_ref, gate_ref, *, cap):
    a = aff_ref[...]
    E, S = a.shape
    u = pltpu.bitcast(a, jnp.int32)
    capf = jnp.float32(cap)

    def count(mask):
        return jnp.sum(jnp.where(mask, 1.0, 0.0), axis=-1, keepdims=True)

    def value_bit(i, thr):
        cand = thr | (jnp.int32(1) << (30 - i))
        return jnp.where(count(u >= cand) >= capf, cand, thr)

    thr = lax.fori_loop(0, 31, value_bit, jnp.zeros((E, 1), jnp.int32))
    gt = u > thr
    eq = u == thr
    need = capf - count(gt)
    idx = lax.broadcasted_iota(jnp.int32, (E, S), 1)
    nbits = int(S).bit_length()

    def index_bit(i, cut):
        cand = cut | (jnp.int32(1) << (nbits - 1 - i))
        return jnp.where(count(eq & (idx < cand)) <= need, cand, cut)

    cut = lax.fori_loop(0, nbits, index_bit, jnp.zeros((E, 1), jnp.int32))
    sel = gt | (eq & (idx < cut))
    gate_ref[...] = jnp.where(sel, a, 0.0)

    li = lax.broadcasted_iota(jnp.int32, (128, 128), 0)
    lj = lax.broadcasted_iota(jnp.int32, (128, 128), 1)
    upper = jnp.where(li <= lj, 1.0, 0.0).astype(BF16)
    carry = jnp.zeros((E, 1), F32)
    for j in range(S // 128):
        sb = jnp.where(sel[:, j * 128:(j + 1) * 128], 1.0, 0.0)
        incl = _dot(sb.astype(BF16), upper)
        pos_ref[:, j * 128:(j + 1) * 128] = (incl - sb + carry).astype(jnp.int32)
        carry = carry + incl[:, 127:128]


def _select(aff, cap):
    B, E, S = aff.shape
    spec = pl.BlockSpec((None, E, S), lambda b: (b, 0, 0))
    return pl.pallas_call(
        functools.partial(_select_kernel, cap=cap),
        grid=(B,),
        in_specs=[spec],
        out_specs=[spec, spec],
        out_shape=[jax.ShapeDtypeStruct((B, E, S), jnp.int32),
                   jax.ShapeDtypeStruct((B, E, S), F32)],
        compiler_params=_cparams(1),
        name="select",
    )(aff)


def _gather_kernel(bs_ref, x_ref, pos_ref, gate_ref, xe_ref, oh_ref,
                   *, n_sb, ts, sub, wn, cap, n_exp, eg):
    b = pl.program_id(0)
    grp = pl.program_id(1)
    tb = pl.program_id(2)
    lane_slot = lax.broadcasted_iota(jnp.int32, (wn, ts), 0)

    @pl.when(tb == 0)
    def _():
        xe_ref[...] = jnp.zeros_like(xe_ref)

    def window_start(first):
        return pl.multiple_of(jnp.minimum(first, cap - wn), 16)

    for s in range(sub):
        toks = slice(s * ts, (s + 1) * ts)
        sb = tb * sub + s

        def onehot(el, start, first):
            pos = pos_ref[el:el + 1, toks]
            wanted = jnp.where(gate_ref[el:el + 1, toks] > 0.0, pos, -1)
            key = jnp.where(pos >= first, wanted, -1)
            return jnp.where(key == start + lane_slot, 1.0, 0.0).astype(BF16)

        spans = []
        for el in range(eg):
            base = (b * n_exp + grp * eg + el) * (n_sb + 1)
            first = (bs_ref[base + sb] // 16) * 16
            start = window_start(first)
            spans.append((first, start, bs_ref[base + sb + 1]))
            oh_ref[s, el * wn:(el + 1) * wn, :] = onehot(el, start, first)
        rows = _dot(oh_ref[s], x_ref[toks, :]).astype(BF16)
        for el in range(eg):
            xe_ref[el, pl.ds(spans[el][1], wn), :] += rows[el * wn:(el + 1) * wn, :]

        for el in range(eg):
            first0, _, end = spans[el]
            n_more = jnp.maximum(end - first0 - 1, 0) // wn

            def window(w, carry):
                first = first0 + (w + 1) * wn
                start = window_start(first)
                xe_ref[el, pl.ds(start, wn), :] += _dot(onehot(el, start, first),
                                                        x_ref[toks, :]).astype(BF16)
                return carry

            lax.fori_loop(0, n_more, window, 0)


def _moe_gather(bs, xn, pos_t, gate_t, cap, tk, ts, wn, eg):
    B, S, D = xn.shape
    E = pos_t.shape[2]
    grid_spec = pltpu.PrefetchScalarGridSpec(
        num_scalar_prefetch=1,
        grid=(B, E // eg, S // tk),
        in_specs=[
            pl.BlockSpec((None, tk, D), lambda b, g, t, bs: (b, t, 0)),
            pl.BlockSpec((None, None, eg, tk), lambda b, g, t, bs: (b, t, g, 0)),
            pl.BlockSpec((None, None, eg, tk), lambda b, g, t, bs: (b, t, g, 0)),
        ],
        out_specs=pl.BlockSpec((None, eg, cap, D), lambda b, g, t, bs: (b, g, 0, 0)),
        scratch_shapes=[pltpu.VMEM((tk // ts, eg * wn, ts), BF16)],
    )
    return pl.pallas_call(
        functools.partial(_gather_kernel, n_sb=S // ts, ts=ts, sub=tk // ts, wn=wn, cap=cap,
                          n_exp=E, eg=eg),
        grid_spec=grid_spec,
        out_shape=jax.ShapeDtypeStruct((B, E, cap, D), BF16),
        compiler_params=_cparams(3),
        name="moe_gather",
    )(bs, xn, pos_t, gate_t)


def _ffn_kernel(xe_ref, wg_ref, wu_ref, wd_ref, ye_ref, wg_bf, wu_bf, wd_bf, *, fm):
    @pl.when(pl.program_id(1) == 0)
    def _():
        wg_bf[...] = wg_ref[...].astype(BF16)
        wu_bf[...] = wu_ref[...].astype(BF16)
        wd_bf[...] = wd_ref[...].astype(BF16)

    for r in range(xe_ref.shape[0] // fm):
        xe = xe_ref[r * fm:(r + 1) * fm, :]
        hid = (_silu(_dot(xe, wg_bf[...])) * _dot(xe, wu_bf[...])).astype(BF16)
        ye_ref[r * fm:(r + 1) * fm, :] = _dot(hid, wd_bf[...]).astype(BF16)


def _moe_ffn(xe, w_gate, w_up, w_down, layer, fm):
    B, E, cap, D = xe.shape
    FF = w_gate.shape[-1]
    rows = pl.BlockSpec((None, None, cap, D), lambda e, b: (b, e, 0, 0))
    return pl.pallas_call(
        functools.partial(_ffn_kernel, fm=fm),
        grid=(E, B),
        in_specs=[rows,
                  pl.BlockSpec((None, None, D, FF), lambda e, b: (layer, e, 0, 0)),
                  pl.BlockSpec((None, None, D, FF), lambda e, b: (layer, e, 0, 0)),
                  pl.BlockSpec((None, None, FF, D), lambda e, b: (layer, e, 0, 0))],
        out_specs=rows,
        out_shape=jax.ShapeDtypeStruct((B, E, cap, D), BF16),
        scratch_shapes=[pltpu.VMEM((D, FF), BF16), pltpu.VMEM((D, FF), BF16), pltpu.VMEM((FF, D), BF16)],
        compiler_params=_cparams(2),
        name="moe_ffn",
    )(xe, w_gate, w_up, w_down)


def _combine_kernel(bs_ref, h1_ref, ye_ref, pos_ref, gate_ref, fg_ref, out_ref, w_ref, y_ref,
                    *, n_sb, ts, sub, wn, cap, n_exp, final_norm):
    b = pl.program_id(0)
    tb = pl.program_id(1)
    lane_slot = lax.broadcasted_iota(jnp.int32, (wn, ts), 0)

    def window_start(first):
        return pl.multiple_of(jnp.minimum(first, cap - wn), 16)

    for s in range(sub):
        toks = slice(s * ts, (s + 1) * ts)
        sb = tb * sub + s

        def weights(e, start, first):
            pos = pos_ref[e:e + 1, toks]
            key = jnp.where(pos >= first, pos, -1)
            return jnp.where(key == start + lane_slot, gate_ref[e:e + 1, toks], 0.0).astype(BF16)

        firsts = []
        for e in range(n_exp):
            base = (b * n_exp + e) * (n_sb + 1)
            first = (bs_ref[base + sb] // 16) * 16
            start = window_start(first)
            firsts.append((first, bs_ref[base + sb + 1]))
            w_ref[s, e * wn:(e + 1) * wn, :] = weights(e, start, first)
            y_ref[s, e * wn:(e + 1) * wn, :] = ye_ref[e, pl.ds(start, wn), :]
        out_ref[toks, :] = h1_ref[toks, :] + _dot_tn(w_ref[s], y_ref[s])

        for e in range(n_exp):
            first0, end = firsts[e]
            n_more = jnp.maximum(end - first0 - 1, 0) // wn

            def window(w, carry):
                first = first0 + (w + 1) * wn
                start = window_start(first)
                out_ref[toks, :] += _dot_tn(weights(e, start, first), ye_ref[e, pl.ds(start, wn), :])
                return carry

            lax.fori_loop(0, n_more, window, 0)
    if final_norm:
        out_ref[...] = _rms(out_ref[...]) * fg_ref[...]


def _combine(bs, h1, ye, pos_t, gate_t, final_g, cap, tk, ts, wn, final_norm):
    B, S, D = h1.shape
    E = ye.shape[1]
    sub = tk // ts
    grid_spec = pltpu.PrefetchScalarGridSpec(
        num_scalar_prefetch=1,
        grid=(B, S // tk),
        in_specs=[
            pl.BlockSpec((None, tk, D), lambda b, t, bs: (b, t, 0)),
            pl.BlockSpec((None, E, cap, D), lambda b, t, bs: (b, 0, 0, 0), pipeline_mode=pl.Buffered(1)),
            pl.BlockSpec((None, None, E, tk), lambda b, t, bs: (b, t, 0, 0)),
            pl.BlockSpec((None, None, E, tk), lambda b, t, bs: (b, t, 0, 0)),
            pl.BlockSpec((1, D), lambda b, t, bs: (0, 0)),
        ],
        out_specs=pl.BlockSpec((None, tk, D), lambda b, t, bs: (b, t, 0)),
        scratch_shapes=[pltpu.VMEM((sub, E * wn, ts), BF16), pltpu.VMEM((sub, E * wn, D), BF16)],
    )
    return pl.pallas_call(
        functools.partial(_combine_kernel, n_sb=S // ts, ts=ts, sub=sub, wn=wn, cap=cap, n_exp=E,
                          final_norm=final_norm),
        grid_spec=grid_spec,
        out_shape=jax.ShapeDtypeStruct((B, S, D), F32),
        compiler_params=_cparams(2),
        name="combine",
    )(bs, h1, ye, pos_t, gate_t, final_g.reshape(1, D))


def _rope_tables(S):
    half = HEAD_DIM // 2
    inv_freq = ROPE_BASE ** (-jnp.arange(half, dtype=F32) / half)
    ang = jnp.arange(S).astype(F32)[:, None] * inv_freq[None, :]
    cos, sin = jnp.cos(ang), jnp.sin(ang)
    return jnp.concatenate([cos, cos], axis=-1), jnp.concatenate([-sin, sin], axis=-1)


def _block_starts(pos, tk, cap):
    B, E, _ = pos.shape
    bs = jnp.concatenate([pos[:, :, ::tk], jnp.full((B, E, 1), cap, jnp.int32)], axis=-1)
    return bs.reshape(-1)


def kernel(x, norm1_g, w_in, ret_norm_g, hgrn_norm_g, w_out, lower_bounds, norm2_g, w_router,
           w_gate, w_up, w_down, final_norm_g):
    B, S, D = x.shape
    depth = w_in.shape[0]
    E = w_router.shape[-1]
    cap = CAPACITY_FACTOR * S // E
    tm = min(512, S)
    tmo = min(1024, S)
    tk = min(512, S)
    tkg = min(1024, S)
    ts = min(256, S)
    wn = min(64, cap)
    fm = min(512, cap)
    eg = min(8, E)

    lbs = jax.nn.softmax(lower_bounds.astype(F32), axis=0)
    lbs = jnp.cumsum(lbs, axis=0) - lbs[0]
    cos, sin = _rope_tables(S)

    h = x
    for layer in range(depth):
        proj = _inproj(h, norm1_g[layer], w_in, layer, cos, sin, tm)
        r_out = _retention(proj, ret_norm_g[layer])
        h_raw = _hgrn(proj, lbs[layer], first_layer=(layer == 0))
        h1, xn, aff = _outproj(r_out, h_raw, proj, hgrn_norm_g[layer], w_out, layer,
                               h, norm2_g[layer], w_router[layer].T.astype(BF16), tmo)
        pos, gate = _select(aff, cap)
        bs = _block_starts(pos, ts, cap)

        def by_block(a, t):
            return a.reshape(B, E, S // t, t).transpose(0, 2, 1, 3)

        xe = _moe_gather(bs, xn, by_block(pos, tkg), by_block(gate, tkg), cap, tkg, ts, wn, eg)
        pos_t = by_block(pos, tk)
        gate_t = by_block(gate, tk)
        ye = _moe_ffn(xe, w_gate, w_up, w_down, layer, fm)
        h = _combine(bs, h1, ye, pos_t, gate_t, final_norm_g, cap, tk, ts, wn,
                     final_norm=(layer == depth - 1))
    return h
```

```python
import functools

import numpy as np
import jax
import jax.numpy as jnp
from jax import lax
from jax.experimental import pallas as pl
from jax.experimental.pallas import tpu as pltpu

F32 = jnp.float32
BF16 = jnp.bfloat16

HEAD_DIM = 128
N_HEADS = 4
GROUP = N_HEADS * HEAD_DIM
N_GROUPS = 9
ROPE_BASE = 10000.0
NORM_EPS = 1e-6
CAPACITY_FACTOR = 2

RET_CHUNK = 256
RET_PAIR = 4
HGRN_CHUNK = 128
HGRN_PAIR = 16
HGRN_LOCK = 2
HGRN_DESC = 16
LOG2E = 1.4426950408889634
VMEM_LIMIT = 58 * 1024 * 1024


def _cparams(n_axes):
    return pltpu.CompilerParams(
        dimension_semantics=("arbitrary",) * n_axes, vmem_limit_bytes=VMEM_LIMIT)


def _dot(a, b):
    return jnp.dot(a, b, preferred_element_type=F32)


def _dot_nt(a, b):
    return lax.dot_general(a, b, (((1,), (1,)), ((), ())), preferred_element_type=F32)


def _dot_tn(a, b):
    return lax.dot_general(a, b, (((0,), (0,)), ((), ())), preferred_element_type=F32)


def _silu(x):
    return x * (1.0 / (1.0 + jnp.exp(-x)))


def _rms(x):
    return x * lax.rsqrt(jnp.mean(x * x, axis=-1, keepdims=True) + NORM_EPS)


def _inproj_kernel(h_ref, g_ref, wf_ref, cos_ref, sin_ref, out_ref, w_ref):
    @pl.when(jnp.logical_and(pl.program_id(0) == 0, pl.program_id(1) == 0))
    def _():
        w_ref[...] = wf_ref[...].astype(BF16)

    xn = (_rms(h_ref[...]) * g_ref[...]).astype(BF16)
    cos = cos_ref[...]
    sin = sin_ref[...]
    scale = HEAD_DIM ** -0.5
    for j in range(N_GROUPS):
        acc = _dot(xn, w_ref[:, j * GROUP:(j + 1) * GROUP])
        if j in (0, 1):
            for hh in range(N_HEADS):
                sl = acc[:, hh * HEAD_DIM:(hh + 1) * HEAD_DIM]
                rot = sl * cos + pltpu.roll(sl, HEAD_DIM // 2, 1) * sin
                if j == 1:
                    rot = rot * scale
                out_ref[:, j * GROUP + hh * HEAD_DIM:j * GROUP + (hh + 1) * HEAD_DIM] = rot.astype(BF16)
            continue
        if j in (3, 8):
            acc = _silu(acc)
        elif j == 4:
            acc = _silu(acc) * scale
        out_ref[:, j * GROUP:(j + 1) * GROUP] = acc.astype(BF16)


def _inproj(h, g, w_in, layer, cos, sin, tm):
    B, S, D = h.shape
    ncol = w_in.shape[-1]
    return pl.pallas_call(
        _inproj_kernel,
        grid=(B, S // tm),
        in_specs=[
            pl.BlockSpec((None, tm, D), lambda b, i: (b, i, 0)),
            pl.BlockSpec((1, D), lambda b, i: (0, 0)),
            pl.BlockSpec((None, D, ncol), lambda b, i: (layer, 0, 0), pipeline_mode=pl.Buffered(1)),
            pl.BlockSpec((tm, HEAD_DIM), lambda b, i: (i, 0)),
            pl.BlockSpec((tm, HEAD_DIM), lambda b, i: (i, 0)),
        ],
        out_specs=pl.BlockSpec((None, tm, ncol), lambda b, i: (b, i, 0)),
        out_shape=jax.ShapeDtypeStruct((B, S, ncol), BF16),
        scratch_shapes=[pltpu.VMEM((D, ncol), BF16)],
        compiler_params=_cparams(2),
        name="inproj",
    )(h, g.reshape(1, D), w_in, cos, sin)


def _ret_kernel(q_ref, k_ref, v_ref, g_ref, dmat_ref, qdec_ref, kdec_ref, cdec_ref, rg_ref,
                out_ref, acc_ref, *, n_chunks):
    L = RET_CHUNK
    cdec = cdec_ref[0:1, :]
    zero_state = jnp.zeros((HEAD_DIM, HEAD_DIM), F32)

    def rows(i):
        return pl.ds(pl.multiple_of(i * L, L), L)

    def finish(sl, o):
        y = _rms(o) * rg_ref[...] * g_ref[sl, :].astype(F32)
        out_ref[sl, :] = y.astype(BF16)

    def step(i, states, second):
        state_f, state_b = states
        sfs = [rows(i * RET_PAIR + j) for j in range(RET_PAIR)]
        sbs = [rows(n_chunks - 1 - i * RET_PAIR - j) for j in range(RET_PAIR)]
        qf, kf, vf = ([r[s, :] for s in sfs] for r in (q_ref, k_ref, v_ref))
        qb, kb, vb = ([r[s, :] for s in sbs] for r in (q_ref, k_ref, v_ref))
        raw = [_dot_nt(q, k) for q, k in zip(qf, kf)]
        qdf = [(q.astype(F32) * qdec_ref[...]).astype(BF16) for q in qf]
        kdf = [(k.astype(F32) * kdec_ref[...]).astype(BF16) for k in kf]
        qdb = [(q.astype(F32) * kdec_ref[...]).astype(BF16) for q in qb]
        kdb = [(k.astype(F32) * qdec_ref[...]).astype(BF16) for k in kb]
        upd_f = [_dot_tn(k, v) for k, v in zip(kdf, vf)]
        upd_b = [_dot_tn(k, v) for k, v in zip(kdb, vb)]
        intra = [_dot((s * dmat_ref[...]).astype(BF16), v) for s, v in zip(raw, vf)]
        for j in range(RET_PAIR):
            of = intra[j] + _dot(qdf[j], state_f.astype(BF16))
            ob = _dot(qdb[j], state_b.astype(BF16))
            state_f = cdec * state_f + upd_f[j]
            state_b = cdec * state_b + upd_b[j]
            if second:
                finish(sfs[j], acc_ref[sfs[j], :] + of)
                finish(sbs[j], acc_ref[sbs[j], :] + ob)
            else:
                acc_ref[sfs[j], :] = of
                acc_ref[sbs[j], :] = ob
        return state_f, state_b

    steps = n_chunks // RET_PAIR
    states = lax.fori_loop(0, steps // 2, functools.partial(step, second=False),
                           (zero_state, zero_state))
    lax.fori_loop(steps // 2, steps, functools.partial(step, second=True), states)


def _ret_tables():
    L = RET_CHUNK
    t = np.arange(L, dtype=np.float64)
    lg = np.log1p(-(2.0 ** (-5.0 - np.arange(N_HEADS, dtype=np.float64))))
    dmat = np.exp(lg[:, None, None] * np.abs(t[:, None] - t[None, :])[None])
    qdec = np.exp(lg[:, None] * (t + 1.0))[:, :, None] * np.ones((1, 1, HEAD_DIM))
    kdec = np.exp(lg[:, None] * (L - 1.0 - t))[:, :, None] * np.ones((1, 1, HEAD_DIM))
    cdec = np.exp(lg * L)[:, None, None] * np.ones((1, 8, HEAD_DIM))
    return tuple(jnp.asarray(a, F32) for a in (dmat, qdec, kdec, cdec))


def _retention(proj, ret_g):
    B, S, _ = proj.shape
    L = RET_CHUNK
    assert S % (2 * RET_PAIR * L) == 0, "retention walks chunks from both ends, RET_PAIR at a time"
    dmat, qdec, kdec, cdec = _ret_tables()

    def col(c0):
        return pl.BlockSpec((None, S, HEAD_DIM), lambda b, h: (b, 0, c0 + h))

    def tab(r):
        return pl.BlockSpec((None, r, HEAD_DIM), lambda b, h: (h, 0, 0))

    return pl.pallas_call(
        functools.partial(_ret_kernel, n_chunks=S // L),
        grid=(B, N_HEADS),
        in_specs=[col(0), col(N_HEADS), col(2 * N_HEADS), col(3 * N_HEADS),
                  pl.BlockSpec((None, L, L), lambda b, h: (h, 0, 0)),
                  tab(L), tab(L), tab(8),
                  pl.BlockSpec((1, HEAD_DIM), lambda b, h: (0, h))],
        out_specs=pl.BlockSpec((None, S, HEAD_DIM), lambda b, h: (b, 0, h)),
        out_shape=jax.ShapeDtypeStruct((B, S, GROUP), BF16),
        scratch_shapes=[pltpu.VMEM((S, HEAD_DIM), F32)],
        compiler_params=_cparams(2),
        name="retention",
    )(proj, proj, proj, proj, dmat, qdec, kdec, cdec, ret_g.reshape(1, GROUP))


def _hgrn_gates(z, lb, first_layer):
    zf = z.astype(F32)
    e = jnp.exp(-jnp.abs(zf))
    inv = 1.0 / (1.0 + e)
    pos = zf >= 0.0
    sigm = jnp.where(pos, e * inv, inv)
    if first_layer:
        return jnp.minimum(zf, 0.0) * LOG2E - jnp.log2(1.0 + e), sigm
    sig = jnp.where(pos, inv, e * inv)
    return jnp.log2(lb + (1.0 - lb) * sig), (1.0 - lb) * sigm


def _hgrn_intra(chunks, lb, mtab_f, mtab_b, lvl, first_layer):
    L = HGRN_CHUNK
    n = len(chunks)
    qfs = [c[0].astype(F32) for c in chunks]
    gates_f = [_hgrn_gates(c[1], lb, first_layer) for c in chunks]
    gates_b = [_hgrn_gates(c[2], lb, first_layer) for c in chunks]

    def exponents(gates, mtab):
        splits = []
        for logf2, _ in gates:
            hi = logf2.astype(BF16)
            splits.append(jnp.concatenate([hi, (logf2 - hi.astype(F32)).astype(BF16)], axis=0))
        decs = []
        for a in range(0, n, 2):
            both = _dot(mtab, jnp.concatenate(splits[a:a + 2], axis=1))
            decs += [both[:, j * HEAD_DIM:(j + 1) * HEAD_DIM] for j in range(len(splits[a:a + 2]))]
        return decs

    decs_f = exponents(gates_f, mtab_f)
    decs_b = exponents(gates_b, mtab_b)
    cfs = [d[0:L, :] for d in decs_f]
    cbs = [d[0:L, :] for d in decs_b]
    kfs = [g[1] for g in gates_f]
    kbs = [g[1] for g in gates_b]

    scores = [jnp.where(lvl == -1, _dot_nt(chunks[i][0], (kfs[i] + kbs[i]).astype(BF16)), 0.0)
              for i in range(n)]
    m, level = 1, 0
    while m < L:
        for i in range(n):
            if m < 8:
                def as3(x):
                    return x.reshape(L // 8, 8, HEAD_DIM)
                pf = as3(jnp.exp2(decs_f[i][(1 + level) * L:(2 + level) * L, :]))
                pb = as3(jnp.exp2(decs_b[i][(1 + level) * L:(2 + level) * L, :]))
                sub = lax.broadcasted_iota(jnp.int32, (1, 8, HEAD_DIM), 1)
                upper = ((sub >> level) & 1) == 1
                qside = (as3(qfs[i]) * jnp.where(upper, pf, pb)).reshape(L, HEAD_DIM)
                kside = jnp.where(upper, as3(kbs[i]) * pb, as3(kfs[i]) * pf).reshape(L, HEAD_DIM)
            else:
                qslabs, kslabs = [], []
                for j in range(L // m):
                    mid = (j // 2) * 2 * m + m
                    sl = slice(j * m, (j + 1) * m)
                    if j % 2 == 1:
                        qdec = jnp.exp2(cfs[i][sl, :] - cfs[i][mid - 1:mid, :])
                        kslabs.append(kbs[i][sl, :] * jnp.exp2(cbs[i][mid:mid + 1, :] - cbs[i][sl, :]))
                    else:
                        qdec = jnp.exp2(cbs[i][sl, :] - cbs[i][mid:mid + 1, :])
                        kslabs.append(kfs[i][sl, :] * jnp.exp2(cfs[i][mid - 1:mid, :] - cfs[i][sl, :]))
                    qslabs.append(qfs[i][sl, :] * qdec)
                qside = jnp.concatenate(qslabs, axis=0)
                kside = jnp.concatenate(kslabs, axis=0)
            s = _dot_nt(qside.astype(BF16), kside.astype(BF16))
            scores[i] = jnp.where(lvl == level, s, scores[i])
        m, level = 2 * m, level + 1

    results = []
    for i in range(n):
        intra = _dot(scores[i].astype(BF16), chunks[i][3])
        cf_end = cfs[i][L - 1:L, :]
        cb_end = cbs[i][0:1, :]
        fwd = ((qfs[i] * jnp.exp2(cfs[i])).astype(BF16),
               (kfs[i] * jnp.exp2(cf_end - cfs[i])).astype(BF16), jnp.exp2(cf_end))
        bwd = ((qfs[i] * jnp.exp2(cbs[i])).astype(BF16),
               (kbs[i] * jnp.exp2(cb_end - cbs[i])).astype(BF16), jnp.exp2(cb_end))
        results.append((intra, fwd, bwd))
    return results


def _hgrn_kernel(q_ref, zf_ref, zb_ref, v_ref, lb_ref, mtabf_ref, mtabb_ref, lvl_ref,
                 out_ref, qeb_ref, keb_ref, dend_ref, *, n_chunks, first_layer):
    L = HGRN_CHUNK
    lb = lb_ref[...]
    zero_state = jnp.zeros((HEAD_DIM, HEAD_DIM), F32)

    def rows(c):
        return pl.ds(pl.multiple_of(c * L, L), L)

    def ascend(i, state):
        cs = [i * HGRN_PAIR + j for j in range(HGRN_PAIR)]
        sls = [rows(c) for c in cs]
        res = []
        for a in range(0, HGRN_PAIR, HGRN_LOCK):
            res += _hgrn_intra([(q_ref[sl, :], zf_ref[sl, :], zb_ref[sl, :], v_ref[sl, :])
                                for sl in sls[a:a + HGRN_LOCK]],
                               lb, mtabf_ref[...], mtabb_ref[...], lvl_ref[...], first_layer)
        updates = [_dot_tn(v_ref[sl, :], r[1][1]) for sl, r in zip(sls, res)]
        for c, sl, (intra, (qe_f, _, dend_f), (qe_b, ke_b, dend_b)), upd in zip(cs, sls, res, updates):
            out_ref[sl, :] = intra + _dot_nt(qe_f, state.astype(BF16))
            state = dend_f * state + upd
            qeb_ref[sl, :] = qe_b
            keb_ref[sl, :] = ke_b
            dend_ref[c] = jnp.broadcast_to(dend_b, (8, HEAD_DIM))
        return state

    lax.fori_loop(0, n_chunks // HGRN_PAIR, ascend, zero_state)

    def descend(i, state):
        cs = [n_chunks - 1 - i * HGRN_DESC - j for j in range(HGRN_DESC)]
        sls = [rows(c) for c in cs]
        updates = [_dot_tn(v_ref[sl, :], keb_ref[sl, :]) for sl in sls]
        for c, sl, upd in zip(cs, sls, updates):
            out_ref[sl, :] += _dot_nt(qeb_ref[sl, :], state.astype(BF16))
            state = dend_ref[c][0:1, :] * state + upd
        return state

    lax.fori_loop(0, n_chunks // HGRN_DESC, descend, zero_state)


def _hgrn_tables():
    L = HGRN_CHUNK
    t = np.arange(L)[:, None]
    u = np.arange(L)[None, :]
    x = t ^ u
    lvl = np.where(x == 0, -1, np.floor(np.log2(np.maximum(x, 1)))).astype(np.int32)

    def exponent_rows(backward):
        blocks = [(u >= t) if backward else (u <= t)]
        m = 1
        while m < 8:
            mid = (t // (2 * m)) * (2 * m) + m
            if backward:
                blocks.append(np.where(t < mid, (u >= t) & (u < mid), (u >= mid) & (u < t)))
            else:
                blocks.append(np.where(t >= mid, (u >= mid) & (u <= t), (u > t) & (u < mid)))
            m *= 2
        tab = np.concatenate(blocks, axis=0).astype(np.float32)
        return np.concatenate([tab, tab], axis=1)

    return (jnp.asarray(exponent_rows(False), BF16), jnp.asarray(exponent_rows(True), BF16),
            jnp.asarray(lvl, jnp.int32))


def _hgrn(proj, lb, first_layer):
    B, S, _ = proj.shape
    L = HGRN_CHUNK
    n_chunks = S // L
    assert S % L == 0 and n_chunks % HGRN_PAIR == 0 and n_chunks % HGRN_DESC == 0
    assert HGRN_PAIR % HGRN_LOCK == 0
    mtab_f, mtab_b, lvl = _hgrn_tables()

    def col(g):
        return pl.BlockSpec((None, S, HEAD_DIM), lambda b, h: (b, 0, g * N_HEADS + h))

    def full(a):
        return pl.BlockSpec(a.shape, lambda b, h: (0, 0))

    return pl.pallas_call(
        functools.partial(_hgrn_kernel, n_chunks=n_chunks, first_layer=first_layer),
        grid=(B, N_HEADS),
        in_specs=[col(4), col(5), col(6), col(7),
                  pl.BlockSpec((1, HEAD_DIM), lambda b, h: (0, h)),
                  full(mtab_f), full(mtab_b), full(lvl)],
        out_specs=pl.BlockSpec((None, S, HEAD_DIM), lambda b, h: (b, 0, h)),
        out_shape=jax.ShapeDtypeStruct((B, S, GROUP), F32),
        scratch_shapes=[pltpu.VMEM((S, HEAD_DIM), BF16), pltpu.VMEM((S, HEAD_DIM), BF16),
                        pltpu.VMEM((n_chunks, 8, HEAD_DIM), F32)],
        compiler_params=_cparams(2),
        name="hgrn",
    )(proj, proj, proj, proj, lb.reshape(1, GROUP), mtab_f, mtab_b, lvl)


def _outproj_kernel(r_ref, hraw_ref, hg_ref, hgn_ref, wf_ref, h_ref, g2_ref, wr_ref,
                    h1_ref, xn_ref, aff_ref, w_ref):
    @pl.when(jnp.logical_and(pl.program_id(0) == 0, pl.program_id(1) == 0))
    def _():
        w_ref[...] = wf_ref[...].astype(BF16)

    hn = _rms(hraw_ref[...]) * hgn_ref[...] * hg_ref[...].astype(F32)
    mix = _dot(r_ref[...], w_ref[0:GROUP, :]) + _dot(hn.astype(BF16), w_ref[GROUP:2 * GROUP, :])
    h1 = h_ref[...] + mix
    h1_ref[...] = h1
    xn = (_rms(h1) * g2_ref[...]).astype(BF16)
    xn_ref[...] = xn
    logits = _dot_nt(wr_ref[...], xn)
    mx = jnp.max(logits, axis=0, keepdims=True)
    ex = jnp.exp(logits - mx)
    aff_ref[...] = ex / jnp.sum(ex, axis=0, keepdims=True)


def _outproj(r_out, h_raw, proj, hgrn_g, w_out, layer, h, g2, wr_t_bf, tm):
    B, S, D = h.shape
    E = wr_t_bf.shape[0]
    return pl.pallas_call(
        _outproj_kernel,
        grid=(B, S // tm),
        in_specs=[
            pl.BlockSpec((None, tm, GROUP), lambda b, i: (b, i, 0)),
            pl.BlockSpec((None, tm, GROUP), lambda b, i: (b, i, 0)),
            pl.BlockSpec((None, tm, GROUP), lambda b, i: (b, i, N_GROUPS - 1)),
            pl.BlockSpec((1, GROUP), lambda b, i: (0, 0)),
            pl.BlockSpec((None, 2 * GROUP, D), lambda b, i: (layer, 0, 0), pipeline_mode=pl.Buffered(1)),
            pl.BlockSpec((None, tm, D), lambda b, i: (b, i, 0)),
            pl.BlockSpec((1, D), lambda b, i: (0, 0)),
            pl.BlockSpec((E, D), lambda b, i: (0, 0)),
        ],
        out_specs=[
            pl.BlockSpec((None, tm, D), lambda b, i: (b, i, 0)),
            pl.BlockSpec((None, tm, D), lambda b, i: (b, i, 0)),
            pl.BlockSpec((None, E, tm), lambda b, i: (b, 0, i)),
        ],
        out_shape=[jax.ShapeDtypeStruct((B, S, D), F32),
                   jax.ShapeDtypeStruct((B, S, D), BF16),
                   jax.ShapeDtypeStruct((B, E, S), F32)],
        scratch_shapes=[pltpu.VMEM((2 * GROUP, D), BF16)],
        compiler_params=_cparams(2),
        name="outproj",
    )(r_out, h_raw, proj, hgrn_g.reshape(1, GROUP), w_out, h, g2.reshape(1, D), wr_t_bf)


def _select_kernel(aff_ref, pos_ref, gate_ref, *, cap):
    a = aff_ref[...]
    E, S = a.shape
    u = pltpu.bitcast(a, jnp.int32)
    capf = jnp.float32(cap)

    def count(mask):
        return jnp.sum(jnp.where(mask, 1.0, 0.0), axis=-1, keepdims=True)

    def value_bit(i, thr):
        cand = thr | (jnp.int32(1) << (30 - i))
        return jnp.where(count(u >= cand) >= capf, cand, thr)

    thr = lax.fori_loop(0, 31, value_bit, jnp.zeros((E, 1), jnp.int32))
    gt = u > thr
    eq = u == thr
    need = capf - count(gt)
    idx = lax.broadcasted_iota(jnp.int32, (E, S), 1)
    nbits = int(S).bit_length()

    def index_bit(i, cut):
        cand = cut | (jnp.int32(1) << (nbits - 1 - i))
        return jnp.where(count(eq & (idx < cand)) <= need, cand, cut)

    cut = lax.fori_loop(0, nbits, index_bit, jnp.zeros((E, 1), jnp.int32))
    sel = gt | (eq & (idx < cut))
    gate_ref[...] = jnp.where(sel, a, 0.0)

    li = lax.broadcasted_iota(jnp.int32, (128, 128), 0)
    lj = lax.broadcasted_iota(jnp.int32, (128, 128), 1)
    upper = jnp.where(li <= lj, 1.0, 0.0).astype(BF16)
    carry = jnp.zeros((E, 1), F32)
    for j in range(S // 128):
        sb = jnp.where(sel[:, j * 128:(j + 1) * 128], 1.0, 0.0)
        incl = _dot(sb.astype(BF16), upper)
        pos_ref[:, j * 128:(j + 1) * 128] = (incl - sb + carry).astype(jnp.int32)
        carry = carry + incl[:, 127:128]


def _select(aff, cap):
    B, E, S = aff.shape
    spec = pl.BlockSpec((None, E, S), lambda b: (b, 0, 0))
    return pl.pallas_call(
        functools.partial(_select_kernel, cap=cap),
        grid=(B,),
        in_specs=[spec],
        out_specs=[spec, spec],
        out_shape=[jax.ShapeDtypeStruct((B, E, S), jnp.int32),
                   jax.ShapeDtypeStruct((B, E, S), F32)],
        compiler_params=_cparams(1),
        name="select",
    )(aff)


def _gather_kernel(bs_ref, x_ref, pos_ref, gate_ref, xe_ref, oh_ref,
                   *, n_sb, ts, sub, wn, cap, n_exp, eg):
    b = pl.program_id(0)
    grp = pl.program_id(1)
    tb = pl.program_id(2)
    lane_slot = lax.broadcasted_iota(jnp.int32, (wn, ts), 0)

    @pl.when(tb == 0)
    def _():
        xe_ref[...] = jnp.zeros_like(xe_ref)

    def window_start(first):
        return pl.multiple_of(jnp.minimum(first, cap - wn), 16)

    for s in range(sub):
        toks = slice(s * ts, (s + 1) * ts)
        sb = tb * sub + s

        def onehot(el, start, first):
            pos = pos_ref[el:el + 1, toks]
            wanted = jnp.where(gate_ref[el:el + 1, toks] > 0.0, pos, -1)
            key = jnp.where(pos >= first, wanted, -1)
            return jnp.where(key == start + lane_slot, 1.0, 0.0).astype(BF16)

        spans = []
        for el in range(eg):
            base = (b * n_exp + grp * eg + el) * (n_sb + 1)
            first = (bs_ref[base + sb] // 16) * 16
            start = window_start(first)
            spans.append((first, start, bs_ref[base + sb + 1]))
            oh_ref[s, el * wn:(el + 1) * wn, :] = onehot(el, start, first)
        rows = _dot(oh_ref[s], x_ref[toks, :]).astype(BF16)
        for el in range(eg):
            xe_ref[el, pl.ds(spans[el][1], wn), :] += rows[el * wn:(el + 1) * wn, :]

        for el in range(eg):
            first0, _, end = spans[el]
            n_more = jnp.maximum(end - first0 - 1, 0) // wn

            def window(w, carry):
                first = first0 + (w + 1) * wn
                start = window_start(first)
                xe_ref[el, pl.ds(start, wn), :] += _dot(onehot(el, start, first),
                                                        x_ref[toks, :]).astype(BF16)
                return carry

            lax.fori_loop(0, n_more, window, 0)


def _moe_gather(bs, xn, pos_t, gate_t, cap, tk, ts, wn, eg):
    B, S, D = xn.shape
    E = pos_t.shape[2]
    grid_spec = pltpu.PrefetchScalarGridSpec(
        num_scalar_prefetch=1,
        grid=(B, E // eg, S // tk),
        in_specs=[
            pl.BlockSpec((None, tk, D), lambda b, g, t, bs: (b, t, 0)),
            pl.BlockSpec((None, None, eg, tk), lambda b, g, t, bs: (b, t, g, 0)),
            pl.BlockSpec((None, None, eg, tk), lambda b, g, t, bs: (b, t, g, 0)),
        ],
        out_specs=pl.BlockSpec((None, eg, cap, D), lambda b, g, t, bs: (b, g, 0, 0)),
        scratch_shapes=[pltpu.VMEM((tk // ts, eg * wn, ts), BF16)],
    )
    return pl.pallas_call(
        functools.partial(_gather_kernel, n_sb=S // ts, ts=ts, sub=tk // ts, wn=wn, cap=cap,
                          n_exp=E, eg=eg),
        grid_spec=grid_spec,
        out_shape=jax.ShapeDtypeStruct((B, E, cap, D), BF16),
        compiler_params=_cparams(3),
        name="moe_gather",
    )(bs, xn, pos_t, gate_t)


def _ffn_kernel(xe_ref, wg_ref, wu_ref, wd_ref, ye_ref, wg_bf, wu_bf, wd_bf, *, fm):
    @pl.when(pl.program_id(1) == 0)
    def _():
        wg_bf[...] = wg_ref[...].astype(BF16)
        wu_bf[...] = wu_ref[...].astype(BF16)
        wd_bf[...] = wd_ref[...].astype(BF16)

    for r in range(xe_ref.shape[0] // fm):
        xe = xe_ref[r * fm:(r + 1) * fm, :]
        hid = (_silu(_dot(xe, wg_bf[...])) * _dot(xe, wu_bf[...])).astype(BF16)
        ye_ref[r * fm:(r + 1) * fm, :] = _dot(hid, wd_bf[...]).astype(BF16)


def _moe_ffn(xe, w_gate, w_up, w_down, layer, fm):
    B, E, cap, D = xe.shape
    FF = w_gate.shape[-1]
    rows = pl.BlockSpec((None, None, cap, D), lambda e, b: (b, e, 0, 0))
    return pl.pallas_call(
        functools.partial(_ffn_kernel, fm=fm),
        grid=(E, B),
        in_specs=[rows,
                  pl.BlockSpec((None, None, D, FF), lambda e, b: (layer, e, 0, 0)),
                  pl.BlockSpec((None, None, D, FF), lambda e, b: (layer, e, 0, 0)),
                  pl.BlockSpec((None, None, FF, D), lambda e, b: (layer, e, 0, 0))],
        out_specs=rows,
        out_shape=jax.ShapeDtypeStruct((B, E, cap, D), BF16),
        scratch_shapes=[pltpu.VMEM((D, FF), BF16), pltpu.VMEM((D, FF), BF16), pltpu.VMEM((FF, D), BF16)],
        compiler_params=_cparams(2),
        name="moe_ffn",
    )(xe, w_gate, w_up, w_down)


def _combine_kernel(bs_ref, h1_ref, ye_ref, pos_ref, gate_ref, fg_ref, out_ref, w_ref, y_ref,
                    *, n_sb, ts, sub, wn, cap, n_exp, final_norm):
    b = pl.program_id(0)
    tb = pl.program_id(1)
    lane_slot = lax.broadcasted_iota(jnp.int32, (wn, ts), 0)

    def window_start(first):
        return pl.multiple_of(jnp.minimum(first, cap - wn), 16)

    for s in range(sub):
        toks = slice(s * ts, (s + 1) * ts)
        sb = tb * sub + s

        def weights(e, start, first):
            pos = pos_ref[e:e + 1, toks]
            key = jnp.where(pos >= first, pos, -1)
            return jnp.where(key == start + lane_slot, gate_ref[e:e + 1, toks], 0.0).astype(BF16)

        firsts = []
        for e in range(n_exp):
            base = (b * n_exp + e) * (n_sb + 1)
            first = (bs_ref[base + sb] // 16) * 16
            start = window_start(first)
            firsts.append((first, bs_ref[base + sb + 1]))
            w_ref[s % 2, e * wn:(e + 1) * wn, :] = weights(e, start, first)
            y_ref[s % 2, e * wn:(e + 1) * wn, :] = ye_ref[e, pl.ds(start, wn), :]
        out_ref[toks, :] = h1_ref[toks, :] + _dot_tn(w_ref[s % 2], y_ref[s % 2])

        for e in range(n_exp):
            first0, end = firsts[e]
            n_more = jnp.maximum(end - first0 - 1, 0) // wn

            def window(w, carry):
                first = first0 + (w + 1) * wn
                start = window_start(first)
                out_ref[toks, :] += _dot_tn(weights(e, start, first), ye_ref[e, pl.ds(start, wn), :])
                return carry

            lax.fori_loop(0, n_more, window, 0)
    if final_norm:
        out_ref[...] = _rms(out_ref[...]) * fg_ref[...]


def _combine(bs, h1, ye, pos_t, gate_t, final_g, cap, tk, ts, wn, final_norm):
    B, S, D = h1.shape
    E = ye.shape[1]
    sub = tk // ts
    grid_spec = pltpu.PrefetchScalarGridSpec(
        num_scalar_prefetch=1,
        grid=(B, S // tk),
        in_specs=[
            pl.BlockSpec((None, tk, D), lambda b, t, bs: (b, t, 0)),
            pl.BlockSpec((None, E, cap, D), lambda b, t, bs: (b, 0, 0, 0), pipeline_mode=pl.Buffered(1)),
            pl.BlockSpec((None, None, E, tk), lambda b, t, bs: (b, t, 0, 0)),
            pl.BlockSpec((None, None, E, tk), lambda b, t, bs: (b, t, 0, 0)),
            pl.BlockSpec((1, D), lambda b, t, bs: (0, 0)),
        ],
        out_specs=pl.BlockSpec((None, tk, D), lambda b, t, bs: (b, t, 0)),
        scratch_shapes=[pltpu.VMEM((2, E * wn, ts), BF16), pltpu.VMEM((2, E * wn, D), BF16)],
    )
    return pl.pallas_call(
        functools.partial(_combine_kernel, n_sb=S // ts, ts=ts, sub=sub, wn=wn, cap=cap, n_exp=E,
                          final_norm=final_norm),
        grid_spec=grid_spec,
        out_shape=jax.ShapeDtypeStruct((B, S, D), F32),
        compiler_params=_cparams(2),
        name="combine",
    )(bs, h1, ye, pos_t, gate_t, final_g.reshape(1, D))


def _rope_tables(S):
    half = HEAD_DIM // 2
    inv_freq = ROPE_BASE ** (-jnp.arange(half, dtype=F32) / half)
    ang = jnp.arange(S).astype(F32)[:, None] * inv_freq[None, :]
    cos, sin = jnp.cos(ang), jnp.sin(ang)
    return jnp.concatenate([cos, cos], axis=-1), jnp.concatenate([-sin, sin], axis=-1)


def _block_starts(pos, tk, cap):
    B, E, _ = pos.shape
    bs = jnp.concatenate([pos[:, :, ::tk], jnp.full((B, E, 1), cap, jnp.int32)], axis=-1)
    return bs.reshape(-1)


def kernel(x, norm1_g, w_in, ret_norm_g, hgrn_norm_g, w_out, lower_bounds, norm2_g, w_router,
           w_gate, w_up, w_down, final_norm_g):
    B, S, D = x.shape
    depth = w_in.shape[0]
    E = w_router.shape[-1]
    cap = CAPACITY_FACTOR * S // E
    tm = min(512, S)
    tmo = min(1024, S)
    tk = min(1024, S)
    tkg = min(2048, S)
    ts = min(256, S)
    wn = min(64, cap)
    fm = min(512, cap)
    eg = min(8, E)

    lbs = jax.nn.softmax(lower_bounds.astype(F32), axis=0)
    lbs = jnp.cumsum(lbs, axis=0) - lbs[0]
    cos, sin = _rope_tables(S)

    h = x
    for layer in range(depth):
        proj = _inproj(h, norm1_g[layer], w_in, layer, cos, sin, tm)
        r_out = _retention(proj, ret_norm_g[layer])
        h_raw = _hgrn(proj, lbs[layer], first_layer=(layer == 0))
        h1, xn, aff = _outproj(r_out, h_raw, proj, hgrn_norm_g[layer], w_out, layer,
                               h, norm2_g[layer], w_router[layer].T.astype(BF16), tmo)
        pos, gate = _select(aff, cap)
        bs = _block_starts(pos, ts, cap)

        def by_block(a, t):
            return a.reshape(B, E, S // t, t).transpose(0, 2, 1, 3)

        xe = _moe_gather(bs, xn, by_block(pos, tkg), by_block(gate, tkg), cap, tkg, ts, wn, eg)
        pos_t = by_block(pos, tk)
        gate_t = by_block(gate, tk)
        ye = _moe_ffn(xe, w_gate, w_up, w_down, layer, fm)
        h = _combine(bs, h1, ye, pos_t, gate_t, final_norm_g, cap, tk, ts, wn,
                     final_norm=(layer == depth - 1))
    return h
```

```python
import functools

import numpy as np
import jax
import jax.numpy as jnp
from jax import lax
from jax.experimental import pallas as pl
from jax.experimental.pallas import tpu as pltpu

F32 = jnp.float32
BF16 = jnp.bfloat16

HEAD_DIM = 128
N_HEADS = 4
GROUP = N_HEADS * HEAD_DIM
N_GROUPS = 9
ROPE_BASE = 10000.0
NORM_EPS = 1e-6
CAPACITY_FACTOR = 2

RET_CHUNK = 256
RET_PAIR = 8
RET_LOCK = 2
HGRN_CHUNK = 128
HGRN_PAIR = 16
HGRN_LOCK = 2
HGRN_DESC = 16
LOG2E = 1.4426950408889634
VMEM_LIMIT = 58 * 1024 * 1024


def _cparams(n_axes):
    return pltpu.CompilerParams(
        dimension_semantics=("arbitrary",) * n_axes, vmem_limit_bytes=VMEM_LIMIT)


def _dot(a, b):
    return jnp.dot(a, b, preferred_element_type=F32)


def _dot_nt(a, b):
    return lax.dot_general(a, b, (((1,), (1,)), ((), ())), preferred_element_type=F32)


def _dot_tn(a, b):
    return lax.dot_general(a, b, (((0,), (0,)), ((), ())), preferred_element_type=F32)


def _silu(x):
    return x * (1.0 / (1.0 + jnp.exp(-x)))


def _rms(x):
    return x * lax.rsqrt(jnp.mean(x * x, axis=-1, keepdims=True) + NORM_EPS)


def _inproj_kernel(h_ref, g_ref, wf_ref, cos_ref, sin_ref, out_ref, w_ref):
    @pl.when(jnp.logical_and(pl.program_id(0) == 0, pl.program_id(1) == 0))
    def _():
        w_ref[...] = wf_ref[...].astype(BF16)

    x = h_ref[...]
    inv_rms = lax.rsqrt(jnp.mean(x * x, axis=-1, keepdims=True) + NORM_EPS)
    xg = (x * g_ref[...]).astype(BF16)
    cos = cos_ref[...]
    sin = sin_ref[...]
    scale = HEAD_DIM ** -0.5
    for j in range(N_GROUPS):
        acc = _dot(xg, w_ref[:, j * GROUP:(j + 1) * GROUP]) * inv_rms
        if j in (0, 1):
            for hh in range(N_HEADS):
                sl = acc[:, hh * HEAD_DIM:(hh + 1) * HEAD_DIM]
                rot = sl * cos + pltpu.roll(sl, HEAD_DIM // 2, 1) * sin
                if j == 1:
                    rot = rot * scale
                out_ref[:, j * GROUP + hh * HEAD_DIM:j * GROUP + (hh + 1) * HEAD_DIM] = rot.astype(BF16)
            continue
        if j in (3, 8):
            acc = _silu(acc)
        elif j == 4:
            acc = _silu(acc) * scale
        out_ref[:, j * GROUP:(j + 1) * GROUP] = acc.astype(BF16)


def _inproj(h, g, w_in, layer, cos, sin, tm):
    B, S, D = h.shape
    ncol = w_in.shape[-1]
    return pl.pallas_call(
        _inproj_kernel,
        grid=(B, S // tm),
        in_specs=[
            pl.BlockSpec((None, tm, D), lambda b, i: (b, i, 0)),
            pl.BlockSpec((1, D), lambda b, i: (0, 0)),
            pl.BlockSpec((None, D, ncol), lambda b, i: (layer, 0, 0), pipeline_mode=pl.Buffered(1)),
            pl.BlockSpec((tm, HEAD_DIM), lambda b, i: (i, 0)),
            pl.BlockSpec((tm, HEAD_DIM), lambda b, i: (i, 0)),
        ],
        out_specs=pl.BlockSpec((None, tm, ncol), lambda b, i: (b, i, 0)),
        out_shape=jax.ShapeDtypeStruct((B, S, ncol), BF16),
        scratch_shapes=[pltpu.VMEM((D, ncol), BF16)],
        compiler_params=_cparams(2),
        name="inproj",
    )(h, g.reshape(1, D), w_in, cos, sin)


def _ret_kernel(q_ref, k_ref, v_ref, g_ref, dmat_ref, qdec_ref, kdec_ref, cdec_ref, rg_ref,
                out_ref, acc_ref, *, n_chunks):
    L = RET_CHUNK
    cdec = cdec_ref[0:1, :]
    zero_state = jnp.zeros((HEAD_DIM, HEAD_DIM), F32)

    def rows(i):
        return pl.ds(pl.multiple_of(i * L, L), L)

    def finish(sl, o):
        y = _rms(o) * rg_ref[...] * g_ref[sl, :].astype(F32)
        out_ref[sl, :] = y.astype(BF16)

    def step(i, states, second):
        state_f, state_b = states
        sfs = [rows(i * RET_PAIR + j) for j in range(RET_PAIR)]
        sbs = [rows(n_chunks - 1 - i * RET_PAIR - j) for j in range(RET_PAIR)]
        qdf, qdb, upd_f, upd_b, intra = [], [], [], [], []
        for a in range(0, RET_PAIR, RET_LOCK):
            grp = slice(a, a + RET_LOCK)
            qf, kf, vf = ([r[s, :] for s in sfs[grp]] for r in (q_ref, k_ref, v_ref))
            qb, kb, vb = ([r[s, :] for s in sbs[grp]] for r in (q_ref, k_ref, v_ref))
            raw = [_dot_nt(q, k) for q, k in zip(qf, kf)]
            qdf += [(q.astype(F32) * qdec_ref[...]).astype(BF16) for q in qf]
            kdf = [(k.astype(F32) * kdec_ref[...]).astype(BF16) for k in kf]
            qdb += [(q.astype(F32) * kdec_ref[...]).astype(BF16) for q in qb]
            kdb = [(k.astype(F32) * qdec_ref[...]).astype(BF16) for k in kb]
            upd_f += [_dot_tn(k, v) for k, v in zip(kdf, vf)]
            upd_b += [_dot_tn(k, v) for k, v in zip(kdb, vb)]
            intra += [_dot((s * dmat_ref[...]).astype(BF16), v) for s, v in zip(raw, vf)]
        for j in range(RET_PAIR):
            of = intra[j] + _dot(qdf[j], state_f.astype(BF16))
            ob = _dot(qdb[j], state_b.astype(BF16))
            state_f = cdec * state_f + upd_f[j]
            state_b = cdec * state_b + upd_b[j]
            if second:
                finish(sfs[j], acc_ref[sfs[j], :] + of)
                finish(sbs[j], acc_ref[sbs[j], :] + ob)
            else:
                acc_ref[sfs[j], :] = of
                acc_ref[sbs[j], :] = ob
        return state_f, state_b

    steps = n_chunks // RET_PAIR
    states = lax.fori_loop(0, steps // 2, functools.partial(step, second=False),
                           (zero_state, zero_state))
    lax.fori_loop(steps // 2, steps, functools.partial(step, second=True), states)


def _ret_tables():
    L = RET_CHUNK
    t = np.arange(L, dtype=np.float64)
    lg = np.log1p(-(2.0 ** (-5.0 - np.arange(N_HEADS, dtype=np.float64))))
    dmat = np.exp(lg[:, None, None] * np.abs(t[:, None] - t[None, :])[None])
    qdec = np.exp(lg[:, None] * (t + 1.0))[:, :, None] * np.ones((1, 1, HEAD_DIM))
    kdec = np.exp(lg[:, None] * (L - 1.0 - t))[:, :, None] * np.ones((1, 1, HEAD_DIM))
    cdec = np.exp(lg * L)[:, None, None] * np.ones((1, 8, HEAD_DIM))
    return tuple(jnp.asarray(a, F32) for a in (dmat, qdec, kdec, cdec))


def _retention(proj, ret_g):
    B, S, _ = proj.shape
    L = RET_CHUNK
    assert S % (2 * RET_PAIR * L) == 0, "retention walks chunks from both ends, RET_PAIR at a time"
    dmat, qdec, kdec, cdec = _ret_tables()

    def col(c0):
        return pl.BlockSpec((None, S, HEAD_DIM), lambda b, h: (b, 0, c0 + h))

    def tab(r):
        return pl.BlockSpec((None, r, HEAD_DIM), lambda b, h: (h, 0, 0))

    return pl.pallas_call(
        functools.partial(_ret_kernel, n_chunks=S // L),
        grid=(B, N_HEADS),
        in_specs=[col(0), col(N_HEADS), col(2 * N_HEADS), col(3 * N_HEADS),
                  pl.BlockSpec((None, L, L), lambda b, h: (h, 0, 0)),
                  tab(L), tab(L), tab(8),
                  pl.BlockSpec((1, HEAD_DIM), lambda b, h: (0, h))],
        out_specs=pl.BlockSpec((None, S, HEAD_DIM), lambda b, h: (b, 0, h)),
        out_shape=jax.ShapeDtypeStruct((B, S, GROUP), BF16),
        scratch_shapes=[pltpu.VMEM((S, HEAD_DIM), F32)],
        compiler_params=_cparams(2),
        name="retention",
    )(proj, proj, proj, proj, dmat, qdec, kdec, cdec, ret_g.reshape(1, GROUP))


def _hgrn_gates(z, lb, first_layer):
    zf = z.astype(F32)
    e = jnp.exp(-jnp.abs(zf))
    inv = 1.0 / (1.0 + e)
    pos = zf >= 0.0
    sigm = jnp.where(pos, e * inv, inv)
    if first_layer:
        return jnp.minimum(zf, 0.0) * LOG2E - jnp.log2(1.0 + e), sigm
    sig = jnp.where(pos, inv, e * inv)
    return jnp.log2(lb + (1.0 - lb) * sig), (1.0 - lb) * sigm


def _hgrn_intra(chunks, lb, mtab_f, mtab_b, lvl, first_layer):
    L = HGRN_CHUNK
    n = len(chunks)
    qfs = [c[0].astype(F32) for c in chunks]
    gates_f = [_hgrn_gates(c[1], lb, first_layer) for c in chunks]
    gates_b = [_hgrn_gates(c[2], lb, first_layer) for c in chunks]

    def exponents(gates, mtab):
        splits = []
        for logf2, _ in gates:
            hi = logf2.astype(BF16)
            splits.append(jnp.concatenate([hi, (logf2 - hi.astype(F32)).astype(BF16)], axis=0))
        decs = []
        for a in range(0, n, 2):
            both = _dot(mtab, jnp.concatenate(splits[a:a + 2], axis=1))
            decs += [both[:, j * HEAD_DIM:(j + 1) * HEAD_DIM] for j in range(len(splits[a:a + 2]))]
        return decs

    decs_f = exponents(gates_f, mtab_f)
    decs_b = exponents(gates_b, mtab_b)
    cfs = [d[0:L, :] for d in decs_f]
    cbs = [d[0:L, :] for d in decs_b]
    kfs = [g[1] for g in gates_f]
    kbs = [g[1] for g in gates_b]

    scores = [jnp.where(lvl == -1, _dot_nt(chunks[i][0], (kfs[i] + kbs[i]).astype(BF16)), 0.0)
              for i in range(n)]
    m, level = 1, 0
    while m < L:
        for i in range(n):
            if m < 8:
                def as3(x):
                    return x.reshape(L // 8, 8, HEAD_DIM)
                pf = as3(jnp.exp2(decs_f[i][(1 + level) * L:(2 + level) * L, :]))
                pb = as3(jnp.exp2(decs_b[i][(1 + level) * L:(2 + level) * L, :]))
                sub = lax.broadcasted_iota(jnp.int32, (1, 8, HEAD_DIM), 1)
                upper = ((sub >> level) & 1) == 1
                qside = (as3(qfs[i]) * jnp.where(upper, pf, pb)).reshape(L, HEAD_DIM)
                kside = jnp.where(upper, as3(kbs[i]) * pb, as3(kfs[i]) * pf).reshape(L, HEAD_DIM)
            else:
                qslabs, kslabs = [], []
                for j in range(L // m):
                    mid = (j // 2) * 2 * m + m
                    sl = slice(j * m, (j + 1) * m)
                    if j % 2 == 1:
                        qdec = jnp.exp2(cfs[i][sl, :] - cfs[i][mid - 1:mid, :])
                        kslabs.append(kbs[i][sl, :] * jnp.exp2(cbs[i][mid:mid + 1, :] - cbs[i][sl, :]))
                    else:
                        qdec = jnp.exp2(cbs[i][sl, :] - cbs[i][mid:mid + 1, :])
                        kslabs.append(kfs[i][sl, :] * jnp.exp2(cfs[i][mid - 1:mid, :] - cfs[i][sl, :]))
                    qslabs.append(qfs[i][sl, :] * qdec)
                qside = jnp.concatenate(qslabs, axis=0)
                kside = jnp.concatenate(kslabs, axis=0)
            s = _dot_nt(qside.astype(BF16), kside.astype(BF16))
            scores[i] = jnp.where(lvl == level, s, scores[i])
        m, level = 2 * m, level + 1

    results = []
    for i in range(n):
        intra = _dot(scores[i].astype(BF16), chunks[i][3])
        cf_end = cfs[i][L - 1:L, :]
        cb_end = cbs[i][0:1, :]
        fwd = ((qfs[i] * jnp.exp2(cfs[i])).astype(BF16),
               (kfs[i] * jnp.exp2(cf_end - cfs[i])).astype(BF16), jnp.exp2(cf_end))
        bwd = ((qfs[i] * jnp.exp2(cbs[i])).astype(BF16),
               (kbs[i] * jnp.exp2(cb_end - cbs[i])).astype(BF16), jnp.exp2(cb_end))
        results.append((intra, fwd, bwd))
    return results


def _hgrn_kernel(q_ref, zf_ref, zb_ref, v_ref, lb_ref, mtabf_ref, mtabb_ref, lvl_ref,
                 out_ref, qeb_ref, keb_ref, dend_ref, *, n_chunks, first_layer):
    L = HGRN_CHUNK
    lb = lb_ref[...]
    zero_state = jnp.zeros((HEAD_DIM, HEAD_DIM), F32)

    def rows(c):
        return pl.ds(pl.multiple_of(c * L, L), L)

    def ascend(i, state):
        cs = [i * HGRN_PAIR + j for j in range(HGRN_PAIR)]
        sls = [rows(c) for c in cs]
        res = []
        for a in range(0, HGRN_PAIR, HGRN_LOCK):
            res += _hgrn_intra([(q_ref[sl, :], zf_ref[sl, :], zb_ref[sl, :], v_ref[sl, :])
                                for sl in sls[a:a + HGRN_LOCK]],
                               lb, mtabf_ref[...], mtabb_ref[...], lvl_ref[...], first_layer)
        updates = [_dot_tn(v_ref[sl, :], r[1][1]) for sl, r in zip(sls, res)]
        for c, sl, (intra, (qe_f, _, dend_f), (qe_b, ke_b, dend_b)), upd in zip(cs, sls, res, updates):
            out_ref[sl, :] = intra + _dot_nt(qe_f, state.astype(BF16))
            state = dend_f * state + upd
            qeb_ref[sl, :] = qe_b
            keb_ref[sl, :] = ke_b
            dend_ref[c] = jnp.broadcast_to(dend_b, (8, HEAD_DIM))
        return state

    lax.fori_loop(0, n_chunks // HGRN_PAIR, ascend, zero_state)

    def descend(i, state):
        cs = [n_chunks - 1 - i * HGRN_DESC - j for j in range(HGRN_DESC)]
        sls = [rows(c) for c in cs]
        updates = [_dot_tn(v_ref[sl, :], keb_ref[sl, :]) for sl in sls]
        for c, sl, upd in zip(cs, sls, updates):
            out_ref[sl, :] += _dot_nt(qeb_ref[sl, :], state.astype(BF16))
            state = dend_ref[c][0:1, :] * state + upd
        return state

    lax.fori_loop(0, n_chunks // HGRN_DESC, descend, zero_state)


def _hgrn_tables():
    L = HGRN_CHUNK
    t = np.arange(L)[:, None]
    u = np.arange(L)[None, :]
    x = t ^ u
    lvl = np.where(x == 0, -1, np.floor(np.log2(np.maximum(x, 1)))).astype(np.int32)

    def exponent_rows(backward):
        blocks = [(u >= t) if backward else (u <= t)]
        m = 1
        while m < 8:
            mid = (t // (2 * m)) * (2 * m) + m
            if backward:
                blocks.append(np.where(t < mid, (u >= t) & (u < mid), (u >= mid) & (u < t)))
            else:
                blocks.append(np.where(t >= mid, (u >= mid) & (u <= t), (u > t) & (u < mid)))
            m *= 2
        tab = np.concatenate(blocks, axis=0).astype(np.float32)
        return np.concatenate([tab, tab], axis=1)

    return (jnp.asarray(exponent_rows(False), BF16), jnp.asarray(exponent_rows(True), BF16),
            jnp.asarray(lvl, jnp.int32))


def _hgrn(proj, lb, first_layer):
    B, S, _ = proj.shape
    L = HGRN_CHUNK
    n_chunks = S // L
    assert S % L == 0 and n_chunks % HGRN_PAIR == 0 and n_chunks % HGRN_DESC == 0
    assert HGRN_PAIR % HGRN_LOCK == 0
    mtab_f, mtab_b, lvl = _hgrn_tables()

    def col(g):
        return pl.BlockSpec((None, S, HEAD_DIM), lambda b, h: (b, 0, g * N_HEADS + h))

    def full(a):
        return pl.BlockSpec(a.shape, lambda b, h: (0, 0))

    return pl.pallas_call(
        functools.partial(_hgrn_kernel, n_chunks=n_chunks, first_layer=first_layer),
        grid=(B, N_HEADS),
        in_specs=[col(4), col(5), col(6), col(7),
                  pl.BlockSpec((1, HEAD_DIM), lambda b, h: (0, h)),
                  full(mtab_f), full(mtab_b), full(lvl)],
        out_specs=pl.BlockSpec((None, S, HEAD_DIM), lambda b, h: (b, 0, h)),
        out_shape=jax.ShapeDtypeStruct((B, S, GROUP), F32),
        scratch_shapes=[pltpu.VMEM((S, HEAD_DIM), BF16), pltpu.VMEM((S, HEAD_DIM), BF16),
                        pltpu.VMEM((n_chunks, 8, HEAD_DIM), F32)],
        compiler_params=_cparams(2),
        name="hgrn",
    )(proj, proj, proj, proj, lb.reshape(1, GROUP), mtab_f, mtab_b, lvl)


def _outproj_kernel(r_ref, hraw_ref, hg_ref, hgn_ref, wf_ref, h_ref, g2_ref, wr_ref,
                    h1_ref, xn_ref, aff_ref, w_ref):
    @pl.when(jnp.logical_and(pl.program_id(0) == 0, pl.program_id(1) == 0))
    def _():
        w_ref[...] = wf_ref[...].astype(BF16)

    hn = _rms(hraw_ref[...]) * hgn_ref[...] * hg_ref[...].astype(F32)
    mix = _dot(r_ref[...], w_ref[0:GROUP, :]) + _dot(hn.astype(BF16), w_ref[GROUP:2 * GROUP, :])
    h1 = h_ref[...] + mix
    h1_ref[...] = h1
    xn = (_rms(h1) * g2_ref[...]).astype(BF16)
    xn_ref[...] = xn
    logits = _dot_nt(wr_ref[...], xn)
    mx = jnp.max(logits, axis=0, keepdims=True)
    ex = jnp.exp(logits - mx)
    aff_ref[...] = ex / jnp.sum(ex, axis=0, keepdims=True)


def _outproj(r_out, h_raw, proj, hgrn_g, w_out, layer, h, g2, wr_t_bf, tm):
    B, S, D = h.shape
    E = wr_t_bf.shape[0]
    return pl.pallas_call(
        _outproj_kernel,
        grid=(B, S // tm),
        in_specs=[
            pl.BlockSpec((None, tm, GROUP), lambda b, i: (b, i, 0)),
            pl.BlockSpec((None, tm, GROUP), lambda b, i: (b, i, 0)),
            pl.BlockSpec((None, tm, GROUP), lambda b, i: (b, i, N_GROUPS - 1)),
            pl.BlockSpec((1, GROUP), lambda b, i: (0, 0)),
            pl.BlockSpec((None, 2 * GROUP, D), lambda b, i: (layer, 0, 0), pipeline_mode=pl.Buffered(1)),
            pl.BlockSpec((None, tm, D), lambda b, i: (b, i, 0)),
            pl.BlockSpec((1, D), lambda b, i: (0, 0)),
            pl.BlockSpec((E, D), lambda b, i: (0, 0)),
        ],
        out_specs=[
            pl.BlockSpec((None, tm, D), lambda b, i: (b, i, 0)),
            pl.BlockSpec((None, tm, D), lambda b, i: (b, i, 0)),
            pl.BlockSpec((None, E, tm), lambda b, i: (b, 0, i)),
        ],
        out_shape=[jax.ShapeDtypeStruct((B, S, D), F32),
                   jax.ShapeDtypeStruct((B, S, D), BF16),
                   jax.ShapeDtypeStruct((B, E, S), F32)],
        scratch_shapes=[pltpu.VMEM((2 * GROUP, D), BF16)],
        compiler_params=_cparams(2),
        name="outproj",
    )(r_out, h_raw, proj, hgrn_g.reshape(1, GROUP), w_out, h, g2.reshape(1, D), wr_t_bf)


def _select_kernel(aff_ref, pos_ref, gate_ref, *, cap):
    a = aff_ref[...]
    E, S = a.shape
    u = pltpu.bitcast(a, jnp.int32)
    capf = jnp.float32(cap)

    def count(mask):
        return jnp.sum(jnp.where(mask, 1.0, 0.0), axis=-1, keepdims=True)

    def value_bit(i, thr):
        cand = thr | (jnp.int32(1) << (30 - i))
        return jnp.where(count(u >= cand) >= capf, cand, thr)

    thr = lax.fori_loop(0, 31, value_bit, jnp.zeros((E, 1), jnp.int32))
    gt = u > thr
    eq = u == thr
    need = capf - count(gt)
    idx = lax.broadcasted_iota(jnp.int32, (E, S), 1)
    nbits = int(S).bit_length()

    def index_bit(i, cut):
        cand = cut | (jnp.int32(1) << (nbits - 1 - i))
        return jnp.where(count(eq & (idx < cand)) <= need, cand, cut)

    cut = lax.fori_loop(0, nbits, index_bit, jnp.zeros((E, 1), jnp.int32))
    sel = gt | (eq & (idx < cut))
    gate_ref[...] = jnp.where(sel, a, 0.0)

    li = lax.broadcasted_iota(jnp.int32, (128, 128), 0)
    lj = lax.broadcasted_iota(jnp.int32, (128, 128), 1)
    upper = jnp.where(li <= lj, 1.0, 0.0).astype(BF16)
    carry = jnp.zeros((E, 1), F32)
    for j in range(S // 128):
        sb = jnp.where(sel[:, j * 128:(j + 1) * 128], 1.0, 0.0)
        incl = _dot(sb.astype(BF16), upper)
        pos_ref[:, j * 128:(j + 1) * 128] = (incl - sb + carry).astype(jnp.int32)
        carry = carry + incl[:, 127:128]


def _select(aff, cap):
    B, E, S = aff.shape
    spec = pl.BlockSpec((None, E, S), lambda b: (b, 0, 0))
    return pl.pallas_call(
        functools.partial(_select_kernel, cap=cap),
        grid=(B,),
        in_specs=[spec],
        out_specs=[spec, spec],
        out_shape=[jax.ShapeDtypeStruct((B, E, S), jnp.int32),
                   jax.ShapeDtypeStruct((B, E, S), F32)],
        compiler_params=_cparams(1),
        name="select",
    )(aff)


def _gather_kernel(bs_ref, x_ref, pos_ref, gate_ref, xe_ref, oh_ref,
                   *, n_sb, ts, sub, wn, cap, n_exp, eg):
    b = pl.program_id(0)
    grp = pl.program_id(1)
    tb = pl.program_id(2)
    lane_slot = lax.broadcasted_iota(jnp.int32, (wn, ts), 0)

    @pl.when(tb == 0)
    def _():
        xe_ref[...] = jnp.zeros_like(xe_ref)

    def window_start(first):
        return pl.multiple_of(jnp.minimum(first, cap - wn), 16)

    for s in range(sub):
        toks = slice(s * ts, (s + 1) * ts)
        sb = tb * sub + s

        def onehot(el, start, first):
            pos = pos_ref[el:el + 1, toks]
            wanted = jnp.where(gate_ref[el:el + 1, toks] > 0.0, pos, -1)
            key = jnp.where(pos >= first, wanted, -1)
            return jnp.where(key == start + lane_slot, 1.0, 0.0).astype(BF16)

        spans = []
        for el in range(eg):
            base = (b * n_exp + grp * eg + el) * (n_sb + 1)
            first = (bs_ref[base + sb] // 16) * 16
            start = window_start(first)
            spans.append((first, start, bs_ref[base + sb + 1]))
            oh_ref[s, el * wn:(el + 1) * wn, :] = onehot(el, start, first)
        rows = _dot(oh_ref[s], x_ref[toks, :]).astype(BF16)
        for el in range(eg):
            xe_ref[el, pl.ds(spans[el][1], wn), :] += rows[el * wn:(el + 1) * wn, :]

        for el in range(eg):
            first0, _, end = spans[el]
            n_more = jnp.maximum(end - first0 - 1, 0) // wn

            def window(w, carry):
                first = first0 + (w + 1) * wn
                start = window_start(first)
                xe_ref[el, pl.ds(start, wn), :] += _dot(onehot(el, start, first),
                                                        x_ref[toks, :]).astype(BF16)
                return carry

            lax.fori_loop(0, n_more, window, 0)


def _moe_gather(bs, xn, pos_t, gate_t, cap, tk, ts, wn, eg):
    B, S, D = xn.shape
    E = pos_t.shape[2]
    grid_spec = pltpu.PrefetchScalarGridSpec(
        num_scalar_prefetch=1,
        grid=(B, E // eg, S // tk),
        in_specs=[
            pl.BlockSpec((None, tk, D), lambda b, g, t, bs: (b, t, 0)),
            pl.BlockSpec((None, None, eg, tk), lambda b, g, t, bs: (b, t, g, 0)),
            pl.BlockSpec((None, None, eg, tk), lambda b, g, t, bs: (b, t, g, 0)),
        ],
        out_specs=pl.BlockSpec((None, eg, cap, D), lambda b, g, t, bs: (b, g, 0, 0)),
        scratch_shapes=[pltpu.VMEM((tk // ts, eg * wn, ts), BF16)],
    )
    return pl.pallas_call(
        functools.partial(_gather_kernel, n_sb=S // ts, ts=ts, sub=tk // ts, wn=wn, cap=cap,
                          n_exp=E, eg=eg),
        grid_spec=grid_spec,
        out_shape=jax.ShapeDtypeStruct((B, E, cap, D), BF16),
        compiler_params=_cparams(3),
        name="moe_gather",
    )(bs, xn, pos_t, gate_t)


def _ffn_kernel(xe_ref, wg_ref, wu_ref, wd_ref, ye_ref, wg_bf, wu_bf, wd_bf, *, fm):
    @pl.when(pl.program_id(1) == 0)
    def _():
        wg_bf[...] = wg_ref[...].astype(BF16)
        wu_bf[...] = wu_ref[...].astype(BF16)
        wd_bf[...] = wd_ref[...].astype(BF16)

    for r in range(xe_ref.shape[0] // fm):
        xe = xe_ref[r * fm:(r + 1) * fm, :]
        hid = (_silu(_dot(xe, wg_bf[...])) * _dot(xe, wu_bf[...])).astype(BF16)
        ye_ref[r * fm:(r + 1) * fm, :] = _dot(hid, wd_bf[...]).astype(BF16)


def _moe_ffn(xe, w_gate, w_up, w_down, layer, fm):
    B, E, cap, D = xe.shape
    FF = w_gate.shape[-1]
    rows = pl.BlockSpec((None, None, cap, D), lambda e, b: (b, e, 0, 0))
    return pl.pallas_call(
        functools.partial(_ffn_kernel, fm=fm),
        grid=(E, B),
        in_specs=[rows,
                  pl.BlockSpec((None, None, D, FF), lambda e, b: (layer, e, 0, 0)),
                  pl.BlockSpec((None, None, D, FF), lambda e, b: (layer, e, 0, 0)),
                  pl.BlockSpec((None, None, FF, D), lambda e, b: (layer, e, 0, 0))],
        out_specs=rows,
        out_shape=jax.ShapeDtypeStruct((B, E, cap, D), BF16),
        scratch_shapes=[pltpu.VMEM((D, FF), BF16), pltpu.VMEM((D, FF), BF16), pltpu.VMEM((FF, D), BF16)],
        compiler_params=_cparams(2),
        name="moe_ffn",
    )(xe, w_gate, w_up, w_down)


def _combine_kernel(bs_ref, h1_ref, ye_ref, pos_ref, gate_ref, fg_ref, out_ref, w_ref, y_ref,
                    *, n_sb, ts, sub, wn, cap, n_exp, final_norm):
    b = pl.program_id(0)
    tb = pl.program_id(1)
    lane_slot = lax.broadcasted_iota(jnp.int32, (wn, ts), 0)

    def window_start(first):
        return pl.multiple_of(jnp.minimum(first, cap - wn), 16)

    for s in range(sub):
        toks = slice(s * ts, (s + 1) * ts)
        sb = tb * sub + s

        def weights(e, start, first):
            pos = pos_ref[e:e + 1, toks]
            key = jnp.where(pos >= first, pos, -1)
            return jnp.where(key == start + lane_slot, gate_ref[e:e + 1, toks], 0.0).astype(BF16)

        firsts = []
        for e in range(n_exp):
            base = (b * n_exp + e) * (n_sb + 1)
            first = (bs_ref[base + sb] // 16) * 16
            start = window_start(first)
            firsts.append((first, bs_ref[base + sb + 1]))
            w_ref[s % 2, e * wn:(e + 1) * wn, :] = weights(e, start, first)
            y_ref[s % 2, e * wn:(e + 1) * wn, :] = ye_ref[e, pl.ds(start, wn), :]
        out_ref[toks, :] = h1_ref[toks, :] + _dot_tn(w_ref[s % 2], y_ref[s % 2])

        for e in range(n_exp):
            first0, end = firsts[e]
            n_more = jnp.maximum(end - first0 - 1, 0) // wn

            def window(w, carry):
                first = first0 + (w + 1) * wn
                start = window_start(first)
                out_ref[toks, :] += _dot_tn(weights(e, start, first), ye_ref[e, pl.ds(start, wn), :])
                return carry

            lax.fori_loop(0, n_more, window, 0)
    if final_norm:
        out_ref[...] = _rms(out_ref[...]) * fg_ref[...]


def _combine(bs, h1, ye, pos_t, gate_t, final_g, cap, tk, ts, wn, final_norm):
    B, S, D = h1.shape
    E = ye.shape[1]
    sub = tk // ts
    grid_spec = pltpu.PrefetchScalarGridSpec(
        num_scalar_prefetch=1,
        grid=(B, S // tk),
        in_specs=[
            pl.BlockSpec((None, tk, D), lambda b, t, bs: (b, t, 0)),
            pl.BlockSpec((None, E, cap, D), lambda b, t, bs: (b, 0, 0, 0), pipeline_mode=pl.Buffered(1)),
            pl.BlockSpec((None, None, E, tk), lambda b, t, bs: (b, t, 0, 0)),
            pl.BlockSpec((None, None, E, tk), lambda b, t, bs: (b, t, 0, 0)),
            pl.BlockSpec((1, D), lambda b, t, bs: (0, 0)),
        ],
        out_specs=pl.BlockSpec((None, tk, D), lambda b, t, bs: (b, t, 0)),
        scratch_shapes=[pltpu.VMEM((2, E * wn, ts), BF16), pltpu.VMEM((2, E * wn, D), BF16)],
    )
    return pl.pallas_call(
        functools.partial(_combine_kernel, n_sb=S // ts, ts=ts, sub=sub, wn=wn, cap=cap, n_exp=E,
                          final_norm=final_norm),
        grid_spec=grid_spec,
        out_shape=jax.ShapeDtypeStruct((B, S, D), F32),
        compiler_params=_cparams(2),
        name="combine",
    )(bs, h1, ye, pos_t, gate_t, final_g.reshape(1, D))


def _rope_tables(S):
    half = HEAD_DIM // 2
    inv_freq = ROPE_BASE ** (-jnp.arange(half, dtype=F32) / half)
    ang = jnp.arange(S).astype(F32)[:, None] * inv_freq[None, :]
    cos, sin = jnp.cos(ang), jnp.sin(ang)
    return jnp.concatenate([cos, cos], axis=-1), jnp.concatenate([-sin, sin], axis=-1)


def _block_starts(pos, tk, cap):
    B, E, _ = pos.shape
    bs = jnp.concatenate([pos[:, :, ::tk], jnp.full((B, E, 1), cap, jnp.int32)], axis=-1)
    return bs.reshape(-1)


def kernel(x, norm1_g, w_in, ret_norm_g, hgrn_norm_g, w_out, lower_bounds, norm2_g, w_router,
           w_gate, w_up, w_down, final_norm_g):
    B, S, D = x.shape
    depth = w_in.shape[0]
    E = w_router.shape[-1]
    cap = CAPACITY_FACTOR * S // E
    tm = min(512, S)
    tmo = min(1024, S)
    tk = min(1024, S)
    tkg = min(2048, S)
    ts = min(256, S)
    wn = min(64, cap)
    fm = min(512, cap)
    eg = min(8, E)

    lbs = jax.nn.softmax(lower_bounds.astype(F32), axis=0)
    lbs = jnp.cumsum(lbs, axis=0) - lbs[0]
    cos, sin = _rope_tables(S)

    h = x
    for layer in range(depth):
        proj = _inproj(h, norm1_g[layer], w_in, layer, cos, sin, tm)
        r_out = _retention(proj, ret_norm_g[layer])
        h_raw = _hgrn(proj, lbs[layer], first_layer=(layer == 0))
        h1, xn, aff = _outproj(r_out, h_raw, proj, hgrn_norm_g[layer], w_out, layer,
                               h, norm2_g[layer], w_router[layer].T.astype(BF16), tmo)
        pos, gate = _select(aff, cap)
        bs = _block_starts(pos, ts, cap)

        def by_block(a, t):
            return a.reshape(B, E, S // t, t).transpose(0, 2, 1, 3)

        xe = _moe_gather(bs, xn, by_block(pos, tkg), by_block(gate, tkg), cap, tkg, ts, wn, eg)
        pos_t = by_block(pos, tk)
        gate_t = by_block(gate, tk)
        ye = _moe_ffn(xe, w_gate, w_up, w_down, layer, fm)
        h = _combine(bs, h1, ye, pos_t, gate_t, final_norm_g, cap, tk, ts, wn,
                     final_norm=(layer == depth - 1))
    return h
```

```python
import functools

import numpy as np
import jax
import jax.numpy as jnp
from jax import lax
from jax.experimental import pallas as pl
from jax.experimental.pallas import tpu as pltpu

F32 = jnp.float32
BF16 = jnp.bfloat16

HEAD_DIM = 128
N_HEADS = 4
GROUP = N_HEADS * HEAD_DIM
N_GROUPS = 9
ROPE_BASE = 10000.0
NORM_EPS = 1e-6
CAPACITY_FACTOR = 2

RET_CHUNK = 256
RET_PAIR = 8
RET_LOCK = 2
HGRN_CHUNK = 128
HGRN_PAIR = 16
HGRN_LOCK = 2
HGRN_DESC = 16
LOG2E = 1.4426950408889634
VMEM_LIMIT = 58 * 1024 * 1024


def _cparams(n_axes):
    return pltpu.CompilerParams(
        dimension_semantics=("arbitrary",) * n_axes, vmem_limit_bytes=VMEM_LIMIT)


def _dot(a, b):
    return jnp.dot(a, b, preferred_element_type=F32)


def _dot_nt(a, b):
    return lax.dot_general(a, b, (((1,), (1,)), ((), ())), preferred_element_type=F32)


def _dot_tn(a, b):
    return lax.dot_general(a, b, (((0,), (0,)), ((), ())), preferred_element_type=F32)


def _silu(x):
    return x * (1.0 / (1.0 + jnp.exp(-x)))


def _rms(x):
    return x * lax.rsqrt(jnp.mean(x * x, axis=-1, keepdims=True) + NORM_EPS)


def _inproj_kernel(h_ref, g_ref, wf_ref, cos_ref, sin_ref, out_ref, w_ref):
    @pl.when(jnp.logical_and(pl.program_id(0) == 0, pl.program_id(1) == 0))
    def _():
        w_ref[...] = wf_ref[...].astype(BF16)

    x = h_ref[...]
    inv_rms = lax.rsqrt(jnp.mean(x * x, axis=-1, keepdims=True) + NORM_EPS)
    xg = (x * g_ref[...]).astype(BF16)
    cos = cos_ref[...]
    sin = sin_ref[...]
    scale = HEAD_DIM ** -0.5
    for j in range(N_GROUPS):
        acc = _dot(xg, w_ref[:, j * GROUP:(j + 1) * GROUP]) * inv_rms
        if j in (0, 1):
            for hh in range(N_HEADS):
                sl = acc[:, hh * HEAD_DIM:(hh + 1) * HEAD_DIM]
                rot = sl * cos + pltpu.roll(sl, HEAD_DIM // 2, 1) * sin
                if j == 1:
                    rot = rot * scale
                out_ref[:, j * GROUP + hh * HEAD_DIM:j * GROUP + (hh + 1) * HEAD_DIM] = rot.astype(BF16)
            continue
        if j in (3, 8):
            acc = _silu(acc)
        elif j == 4:
            acc = _silu(acc) * scale
        out_ref[:, j * GROUP:(j + 1) * GROUP] = acc.astype(BF16)


def _inproj(h, g, w_in, layer, cos, sin, tm):
    B, S, D = h.shape
    ncol = w_in.shape[-1]
    return pl.pallas_call(
        _inproj_kernel,
        grid=(B, S // tm),
        in_specs=[
            pl.BlockSpec((None, tm, D), lambda b, i: (b, i, 0)),
            pl.BlockSpec((1, D), lambda b, i: (0, 0)),
            pl.BlockSpec((None, D, ncol), lambda b, i: (layer, 0, 0), pipeline_mode=pl.Buffered(1)),
            pl.BlockSpec((tm, HEAD_DIM), lambda b, i: (i, 0)),
            pl.BlockSpec((tm, HEAD_DIM), lambda b, i: (i, 0)),
        ],
        out_specs=pl.BlockSpec((None, tm, ncol), lambda b, i: (b, i, 0)),
        out_shape=jax.ShapeDtypeStruct((B, S, ncol), BF16),
        scratch_shapes=[pltpu.VMEM((D, ncol), BF16)],
        compiler_params=_cparams(2),
        name="inproj",
    )(h, g.reshape(1, D), w_in, cos, sin)


def _ret_kernel(q_ref, k_ref, v_ref, g_ref, dmat_ref, qdec_ref, kdec_ref, cdec_ref, rg_ref,
                out_ref, acc_ref, *, n_chunks):
    L = RET_CHUNK
    cdec = cdec_ref[0:1, :]
    zero_state = jnp.zeros((HEAD_DIM, HEAD_DIM), F32)

    def rows(i):
        return pl.ds(pl.multiple_of(i * L, L), L)

    def finish(sl, o):
        y = _rms(o) * rg_ref[...] * g_ref[sl, :].astype(F32)
        out_ref[sl, :] = y.astype(BF16)

    def step(i, states, second):
        state_f, state_b = states
        sfs = [rows(i * RET_PAIR + j) for j in range(RET_PAIR)]
        sbs = [rows(n_chunks - 1 - i * RET_PAIR - j) for j in range(RET_PAIR)]
        qdf, qdb, upd_f, upd_b, intra = [], [], [], [], []
        for a in range(0, RET_PAIR, RET_LOCK):
            grp = slice(a, a + RET_LOCK)
            qf, kf, vf = ([r[s, :] for s in sfs[grp]] for r in (q_ref, k_ref, v_ref))
            qb, kb, vb = ([r[s, :] for s in sbs[grp]] for r in (q_ref, k_ref, v_ref))
            raw = [_dot_nt(q, k) for q, k in zip(qf, kf)]
            qdf += [(q.astype(F32) * qdec_ref[...]).astype(BF16) for q in qf]
            kdf = [(k.astype(F32) * kdec_ref[...]).astype(BF16) for k in kf]
            qdb += [(q.astype(F32) * kdec_ref[...]).astype(BF16) for q in qb]
            kdb = [(k.astype(F32) * qdec_ref[...]).astype(BF16) for k in kb]
            upd_f += [_dot_tn(k, v) for k, v in zip(kdf, vf)]
            upd_b += [_dot_tn(k, v) for k, v in zip(kdb, vb)]
            intra += [_dot((s * dmat_ref[...]).astype(BF16), v) for s, v in zip(raw, vf)]
        for j in range(RET_PAIR):
            of = intra[j] + _dot(qdf[j], state_f.astype(BF16))
            ob = _dot(qdb[j], state_b.astype(BF16))
            state_f = cdec * state_f + upd_f[j]
            state_b = cdec * state_b + upd_b[j]
            if second:
                finish(sfs[j], acc_ref[sfs[j], :] + of)
                finish(sbs[j], acc_ref[sbs[j], :] + ob)
            else:
                acc_ref[sfs[j], :] = of
                acc_ref[sbs[j], :] = ob
        return state_f, state_b

    steps = n_chunks // RET_PAIR
    states = lax.fori_loop(0, steps // 2, functools.partial(step, second=False),
                           (zero_state, zero_state))
    lax.fori_loop(steps // 2, steps, functools.partial(step, second=True), states)


def _ret_tables():
    L = RET_CHUNK
    t = np.arange(L, dtype=np.float64)
    lg = np.log1p(-(2.0 ** (-5.0 - np.arange(N_HEADS, dtype=np.float64))))
    dmat = np.exp(lg[:, None, None] * np.abs(t[:, None] - t[None, :])[None])
    qdec = np.exp(lg[:, None] * (t + 1.0))[:, :, None] * np.ones((1, 1, HEAD_DIM))
    kdec = np.exp(lg[:, None] * (L - 1.0 - t))[:, :, None] * np.ones((1, 1, HEAD_DIM))
    cdec = np.exp(lg * L)[:, None, None] * np.ones((1, 8, HEAD_DIM))
    return tuple(jnp.asarray(a, F32) for a in (dmat, qdec, kdec, cdec))


def _retention(proj, ret_g):
    B, S, _ = proj.shape
    L = RET_CHUNK
    assert S % (2 * RET_PAIR * L) == 0, "retention walks chunks from both ends, RET_PAIR at a time"
    dmat, qdec, kdec, cdec = _ret_tables()

    def col(c0):
        return pl.BlockSpec((None, S, HEAD_DIM), lambda b, h: (b, 0, c0 + h))

    def tab(r):
        return pl.BlockSpec((None, r, HEAD_DIM), lambda b, h: (h, 0, 0))

    return pl.pallas_call(
        functools.partial(_ret_kernel, n_chunks=S // L),
        grid=(B, N_HEADS),
        in_specs=[col(0), col(N_HEADS), col(2 * N_HEADS), col(3 * N_HEADS),
                  pl.BlockSpec((None, L, L), lambda b, h: (h, 0, 0)),
                  tab(L), tab(L), tab(8),
                  pl.BlockSpec((1, HEAD_DIM), lambda b, h: (0, h))],
        out_specs=pl.BlockSpec((None, S, HEAD_DIM), lambda b, h: (b, 0, h)),
        out_shape=jax.ShapeDtypeStruct((B, S, GROUP), BF16),
        scratch_shapes=[pltpu.VMEM((S, HEAD_DIM), F32)],
        compiler_params=_cparams(2),
        name="retention",
    )(proj, proj, proj, proj, dmat, qdec, kdec, cdec, ret_g.reshape(1, GROUP))


def _hgrn_gates(z, lb, first_layer):
    zf = z.astype(F32)
    e = jnp.exp(-jnp.abs(zf))
    inv = 1.0 / (1.0 + e)
    pos = zf >= 0.0
    sigm = jnp.where(pos, e * inv, inv)
    if first_layer:
        return jnp.minimum(zf, 0.0) * LOG2E - jnp.log2(1.0 + e), sigm
    sig = jnp.where(pos, inv, e * inv)
    return jnp.log2(lb + (1.0 - lb) * sig), (1.0 - lb) * sigm


def _hgrn_intra(chunks, lb, mtab_f, mtab_b, lvl, first_layer):
    L = HGRN_CHUNK
    n = len(chunks)
    qfs = [c[0].astype(F32) for c in chunks]
    gates_f = [_hgrn_gates(c[1], lb, first_layer) for c in chunks]
    gates_b = [_hgrn_gates(c[2], lb, first_layer) for c in chunks]

    def exponents(gates, mtab):
        splits = []
        for logf2, _ in gates:
            hi = logf2.astype(BF16)
            splits.append(jnp.concatenate([hi, (logf2 - hi.astype(F32)).astype(BF16)], axis=0))
        decs = []
        for a in range(0, n, 2):
            both = _dot(mtab, jnp.concatenate(splits[a:a + 2], axis=1))
            decs += [both[:, j * HEAD_DIM:(j + 1) * HEAD_DIM] for j in range(len(splits[a:a + 2]))]
        return decs

    decs_f = exponents(gates_f, mtab_f)
    decs_b = exponents(gates_b, mtab_b)
    cfs = [d[0:L, :] for d in decs_f]
    cbs = [d[0:L, :] for d in decs_b]
    kfs = [g[1] for g in gates_f]
    kbs = [g[1] for g in gates_b]

    scores = [jnp.where(lvl == -1, _dot_nt(chunks[i][0], (kfs[i] + kbs[i]).astype(BF16)), 0.0)
              for i in range(n)]
    m, level = 1, 0
    while m < L:
        for i in range(n):
            if m < 8:
                def as3(x):
                    return x.reshape(L // 8, 8, HEAD_DIM)
                pf = as3(jnp.exp2(decs_f[i][(1 + level) * L:(2 + level) * L, :]))
                pb = as3(jnp.exp2(decs_b[i][(1 + level) * L:(2 + level) * L, :]))
                sub = lax.broadcasted_iota(jnp.int32, (1, 8, HEAD_DIM), 1)
                upper = ((sub >> level) & 1) == 1
                qside = (as3(qfs[i]) * jnp.where(upper, pf, pb)).reshape(L, HEAD_DIM)
                kside = jnp.where(upper, as3(kbs[i]) * pb, as3(kfs[i]) * pf).reshape(L, HEAD_DIM)
            else:
                qslabs, kslabs = [], []
                for j in range(L // m):
                    mid = (j // 2) * 2 * m + m
                    sl = slice(j * m, (j + 1) * m)
                    if j % 2 == 1:
                        qdec = jnp.exp2(cfs[i][sl, :] - cfs[i][mid - 1:mid, :])
                        kslabs.append(kbs[i][sl, :] * jnp.exp2(cbs[i][mid:mid + 1, :] - cbs[i][sl, :]))
                    else:
                        qdec = jnp.exp2(cbs[i][sl, :] - cbs[i][mid:mid + 1, :])
                        kslabs.append(kfs[i][sl, :] * jnp.exp2(cfs[i][mid - 1:mid, :] - cfs[i][sl, :]))
                    qslabs.append(qfs[i][sl, :] * qdec)
                qside = jnp.concatenate(qslabs, axis=0)
                kside = jnp.concatenate(kslabs, axis=0)
            s = _dot_nt(qside.astype(BF16), kside.astype(BF16))
            scores[i] = jnp.where(lvl == level, s, scores[i])
        m, level = 2 * m, level + 1

    results = []
    for i in range(n):
        intra = _dot(scores[i].astype(BF16), chunks[i][3])
        cf_end = cfs[i][L - 1:L, :]
        cb_end = cbs[i][0:1, :]
        fwd = ((qfs[i] * jnp.exp2(cfs[i])).astype(BF16),
               (kfs[i] * jnp.exp2(cf_end - cfs[i])).astype(BF16), jnp.exp2(cf_end))
        bwd = ((qfs[i] * jnp.exp2(cbs[i])).astype(BF16),
               (kbs[i] * jnp.exp2(cb_end - cbs[i])).astype(BF16), jnp.exp2(cb_end))
        results.append((intra, fwd, bwd))
    return results


def _hgrn_kernel(q_ref, zf_ref, zb_ref, v_ref, lb_ref, mtabf_ref, mtabb_ref, lvl_ref,
                 out_ref, qeb_ref, keb_ref, dend_ref, *, n_chunks, first_layer):
    L = HGRN_CHUNK
    lb = lb_ref[...]
    zero_state = jnp.zeros((HEAD_DIM, HEAD_DIM), F32)

    def rows(c):
        return pl.ds(pl.multiple_of(c * L, L), L)

    def ascend(i, state):
        cs = [i * HGRN_PAIR + j for j in range(HGRN_PAIR)]
        sls = [rows(c) for c in cs]
        res = []
        for a in range(0, HGRN_PAIR, HGRN_LOCK):
            res += _hgrn_intra([(q_ref[sl, :], zf_ref[sl, :], zb_ref[sl, :], v_ref[sl, :])
                                for sl in sls[a:a + HGRN_LOCK]],
                               lb, mtabf_ref[...], mtabb_ref[...], lvl_ref[...], first_layer)
        updates = [_dot_tn(v_ref[sl, :], r[1][1]) for sl, r in zip(sls, res)]
        for c, sl, (intra, (qe_f, _, dend_f), (qe_b, ke_b, dend_b)), upd in zip(cs, sls, res, updates):
            out_ref[sl, :] = intra + _dot_nt(qe_f, state.astype(BF16))
            state = dend_f * state + upd
            qeb_ref[sl, :] = qe_b
            keb_ref[sl, :] = ke_b
            dend_ref[c] = jnp.broadcast_to(dend_b, (8, HEAD_DIM))
        return state

    lax.fori_loop(0, n_chunks // HGRN_PAIR, ascend, zero_state)

    def descend(i, state):
        cs = [n_chunks - 1 - i * HGRN_DESC - j for j in range(HGRN_DESC)]
        sls = [rows(c) for c in cs]
        updates = [_dot_tn(v_ref[sl, :], keb_ref[sl, :]) for sl in sls]
        for c, sl, upd in zip(cs, sls, updates):
            out_ref[sl, :] += _dot_nt(qeb_ref[sl, :], state.astype(BF16))
            state = dend_ref[c][0:1, :] * state + upd
        return state

    lax.fori_loop(0, n_chunks // HGRN_DESC, descend, zero_state)


def _hgrn_tables():
    L = HGRN_CHUNK
    t = np.arange(L)[:, None]
    u = np.arange(L)[None, :]
    x = t ^ u
    lvl = np.where(x == 0, -1, np.floor(np.log2(np.maximum(x, 1)))).astype(np.int32)

    def exponent_rows(backward):
        blocks = [(u >= t) if backward else (u <= t)]
        m = 1
        while m < 8:
            mid = (t // (2 * m)) * (2 * m) + m
            if backward:
                blocks.append(np.where(t < mid, (u >= t) & (u < mid), (u >= mid) & (u < t)))
            else:
                blocks.append(np.where(t >= mid, (u >= mid) & (u <= t), (u > t) & (u < mid)))
            m *= 2
        tab = np.concatenate(blocks, axis=0).astype(np.float32)
        return np.concatenate([tab, tab], axis=1)

    return (jnp.asarray(exponent_rows(False), BF16), jnp.asarray(exponent_rows(True), BF16),
            jnp.asarray(lvl, jnp.int32))


def _hgrn(proj, lb, first_layer):
    B, S, _ = proj.shape
    L = HGRN_CHUNK
    n_chunks = S // L
    assert S % L == 0 and n_chunks % HGRN_PAIR == 0 and n_chunks % HGRN_DESC == 0
    assert HGRN_PAIR % HGRN_LOCK == 0
    mtab_f, mtab_b, lvl = _hgrn_tables()

    def col(g):
        return pl.BlockSpec((None, S, HEAD_DIM), lambda b, h: (b, 0, g * N_HEADS + h))

    def full(a):
        return pl.BlockSpec(a.shape, lambda b, h: (0, 0))

    return pl.pallas_call(
        functools.partial(_hgrn_kernel, n_chunks=n_chunks, first_layer=first_layer),
        grid=(B, N_HEADS),
        in_specs=[col(4), col(5), col(6), col(7),
                  pl.BlockSpec((1, HEAD_DIM), lambda b, h: (0, h)),
                  full(mtab_f), full(mtab_b), full(lvl)],
        out_specs=pl.BlockSpec((None, S, HEAD_DIM), lambda b, h: (b, 0, h)),
        out_shape=jax.ShapeDtypeStruct((B, S, GROUP), F32),
        scratch_shapes=[pltpu.VMEM((S, HEAD_DIM), BF16), pltpu.VMEM((S, HEAD_DIM), BF16),
                        pltpu.VMEM((n_chunks, 8, HEAD_DIM), F32)],
        compiler_params=_cparams(2),
        name="hgrn",
    )(proj, proj, proj, proj, lb.reshape(1, GROUP), mtab_f, mtab_b, lvl)


def _outproj_kernel(r_ref, hraw_ref, hg_ref, hgn_ref, wf_ref, h_ref, g2_ref, wr_ref,
                    h1_ref, xn_ref, aff_ref, w_ref):
    @pl.when(jnp.logical_and(pl.program_id(0) == 0, pl.program_id(1) == 0))
    def _():
        w_ref[...] = wf_ref[...].astype(BF16)

    hn = _rms(hraw_ref[...]) * hgn_ref[...] * hg_ref[...].astype(F32)
    mix = _dot(r_ref[...], w_ref[0:GROUP, :]) + _dot(hn.astype(BF16), w_ref[GROUP:2 * GROUP, :])
    h1 = h_ref[...] + mix
    h1_ref[...] = h1
    xn = (_rms(h1) * g2_ref[...]).astype(BF16)
    xn_ref[...] = xn
    logits = _dot_nt(wr_ref[...], xn)
    mx = jnp.max(logits, axis=0, keepdims=True)
    ex = jnp.exp(logits - mx)
    aff_ref[...] = ex / jnp.sum(ex, axis=0, keepdims=True)


def _outproj(r_out, h_raw, proj, hgrn_g, w_out, layer, h, g2, wr_t_bf, tm):
    B, S, D = h.shape
    E = wr_t_bf.shape[0]
    return pl.pallas_call(
        _outproj_kernel,
        grid=(B, S // tm),
        in_specs=[
            pl.BlockSpec((None, tm, GROUP), lambda b, i: (b, i, 0)),
            pl.BlockSpec((None, tm, GROUP), lambda b, i: (b, i, 0)),
            pl.BlockSpec((None, tm, GROUP), lambda b, i: (b, i, N_GROUPS - 1)),
            pl.BlockSpec((1, GROUP), lambda b, i: (0, 0)),
            pl.BlockSpec((None, 2 * GROUP, D), lambda b, i: (layer, 0, 0), pipeline_mode=pl.Buffered(1)),
            pl.BlockSpec((None, tm, D), lambda b, i: (b, i, 0)),
            pl.BlockSpec((1, D), lambda b, i: (0, 0)),
            pl.BlockSpec((E, D), lambda b, i: (0, 0)),
        ],
        out_specs=[
            pl.BlockSpec((None, tm, D), lambda b, i: (b, i, 0)),
            pl.BlockSpec((None, tm, D), lambda b, i: (b, i, 0)),
            pl.BlockSpec((None, E, tm), lambda b, i: (b, 0, i)),
        ],
        out_shape=[jax.ShapeDtypeStruct((B, S, D), F32),
                   jax.ShapeDtypeStruct((B, S, D), BF16),
                   jax.ShapeDtypeStruct((B, E, S), F32)],
        scratch_shapes=[pltpu.VMEM((2 * GROUP, D), BF16)],
        compiler_params=_cparams(2),
        name="outproj",
    )(r_out, h_raw, proj, hgrn_g.reshape(1, GROUP), w_out, h, g2.reshape(1, D), wr_t_bf)


def _select_kernel(aff_ref, pos_ref, gate_ref, *, cap):
    a = aff_ref[...]
    E, S = a.shape
    u = pltpu.bitcast(a, jnp.int32)
    capf = jnp.float32(cap)

    def count(mask):
        return jnp.sum(jnp.where(mask, 1.0, 0.0), axis=-1, keepdims=True)

    def value_bit(i, thr):
        cand = thr | (jnp.int32(1) << (30 - i))
        return jnp.where(count(u >= cand) >= capf, cand, thr)

    thr = lax.fori_loop(0, 31, value_bit, jnp.zeros((E, 1), jnp.int32))
    gt = u > thr
    eq = u == thr
    need = capf - count(gt)
    idx = lax.broadcasted_iota(jnp.int32, (E, S), 1)
    nbits = int(S).bit_length()

    def index_bit(i, cut):
        cand = cut | (jnp.int32(1) << (nbits - 1 - i))
        return jnp.where(count(eq & (idx < cand)) <= need, cand, cut)

    cut = lax.fori_loop(0, nbits, index_bit, jnp.zeros((E, 1), jnp.int32))
    sel = gt | (eq & (idx < cut))
    gate_ref[...] = jnp.where(sel, a, 0.0)

    li = lax.broadcasted_iota(jnp.int32, (128, 128), 0)
    lj = lax.broadcasted_iota(jnp.int32, (128, 128), 1)
    upper = jnp.where(li <= lj, 1.0, 0.0).astype(BF16)
    carry = jnp.zeros((E, 1), F32)
    for j in range(S // 128):
        sb = jnp.where(sel[:, j * 128:(j + 1) * 128], 1.0, 0.0)
        incl = _dot(sb.astype(BF16), upper)
        pos_ref[:, j * 128:(j + 1) * 128] = (incl - sb + carry).astype(jnp.int32)
        carry = carry + incl[:, 127:128]


def _select(aff, cap):
    B, E, S = aff.shape
    spec = pl.BlockSpec((None, E, S), lambda b: (b, 0, 0))
    return pl.pallas_call(
        functools.partial(_select_kernel, cap=cap),
        grid=(B,),
        in_specs=[spec],
        out_specs=[spec, spec],
        out_shape=[jax.ShapeDtypeStruct((B, E, S), jnp.int32),
                   jax.ShapeDtypeStruct((B, E, S), F32)],
        compiler_params=_cparams(1),
        name="select",
    )(aff)


def _gather_kernel(bs_ref, x_ref, pos_ref, gate_ref, xe_ref, oh_ref,
                   *, n_sb, ts, sub, wn, cap, n_exp, eg):
    b = pl.program_id(0)
    grp = pl.program_id(1)
    tb = pl.program_id(2)
    lane_slot = lax.broadcasted_iota(jnp.int32, (wn, ts), 0)

    @pl.when(tb == 0)
    def _():
        xe_ref[...] = jnp.zeros_like(xe_ref)

    def window_start(first):
        return pl.multiple_of(jnp.minimum(first, cap - wn), 16)

    for s in range(sub):
        toks = slice(s * ts, (s + 1) * ts)
        sb = tb * sub + s

        def onehot(el, start, first):
            pos = pos_ref[el:el + 1, toks]
            wanted = jnp.where(gate_ref[el:el + 1, toks] > 0.0, pos, -1)
            key = jnp.where(pos >= first, wanted, -1)
            return jnp.where(key == start + lane_slot, 1.0, 0.0).astype(BF16)

        spans = []
        for el in range(eg):
            base = (b * n_exp + grp * eg + el) * (n_sb + 1)
            first = (bs_ref[base + sb] // 16) * 16
            start = window_start(first)
            spans.append((first, start, bs_ref[base + sb + 1]))
            oh_ref[s, el * wn:(el + 1) * wn, :] = onehot(el, start, first)
        rows = _dot(oh_ref[s], x_ref[toks, :]).astype(BF16)
        for el in range(eg):
            xe_ref[el, pl.ds(spans[el][1], wn), :] += rows[el * wn:(el + 1) * wn, :]

        for el in range(eg):
            first0, _, end = spans[el]
            n_more = jnp.maximum(end - first0 - 1, 0) // wn

            def window(w, carry):
                first = first0 + (w + 1) * wn
                start = window_start(first)
                xe_ref[el, pl.ds(start, wn), :] += _dot(onehot(el, start, first),
                                                        x_ref[toks, :]).astype(BF16)
                return carry

            lax.fori_loop(0, n_more, window, 0)


def _moe_gather(bs, xn, pos_t, gate_t, cap, tk, ts, wn, eg):
    B, S, D = xn.shape
    E = pos_t.shape[2]
    grid_spec = pltpu.PrefetchScalarGridSpec(
        num_scalar_prefetch=1,
        grid=(B, E // eg, S // tk),
        in_specs=[
            pl.BlockSpec((None, tk, D), lambda b, g, t, bs: (b, t, 0)),
            pl.BlockSpec((None, None, eg, tk), lambda b, g, t, bs: (b, t, g, 0)),
            pl.BlockSpec((None, None, eg, tk), lambda b, g, t, bs: (b, t, g, 0)),
        ],
        out_specs=pl.BlockSpec((None, eg, cap, D), lambda b, g, t, bs: (b, g, 0, 0)),
        scratch_shapes=[pltpu.VMEM((tk // ts, eg * wn, ts), BF16)],
    )
    return pl.pallas_call(
        functools.partial(_gather_kernel, n_sb=S // ts, ts=ts, sub=tk // ts, wn=wn, cap=cap,
                          n_exp=E, eg=eg),
        grid_spec=grid_spec,
        out_shape=jax.ShapeDtypeStruct((B, E, cap, D), BF16),
        compiler_params=_cparams(3),
        name="moe_gather",
    )(bs, xn, pos_t, gate_t)


def _ffn_kernel(xe_ref, wg_ref, wu_ref, wd_ref, ye_ref, wg_bf, wu_bf, wd_bf, *, fm):
    @pl.when(pl.program_id(1) == 0)
    def _():
        wg_bf[...] = wg_ref[...].astype(BF16)
        wu_bf[...] = wu_ref[...].astype(BF16)
        wd_bf[...] = wd_ref[...].astype(BF16)

    for r in range(xe_ref.shape[0] // fm):
        xe = xe_ref[r * fm:(r + 1) * fm, :]
        hid = (_silu(_dot(xe, wg_bf[...])) * _dot(xe, wu_bf[...])).astype(BF16)
        ye_ref[r * fm:(r + 1) * fm, :] = _dot(hid, wd_bf[...]).astype(BF16)


def _moe_ffn(xe, w_gate, w_up, w_down, layer, fm):
    B, E, cap, D = xe.shape
    FF = w_gate.shape[-1]
    rows = pl.BlockSpec((None, None, cap, D), lambda e, b: (b, e, 0, 0))
    return pl.pallas_call(
        functools.partial(_ffn_kernel, fm=fm),
        grid=(E, B),
        in_specs=[rows,
                  pl.BlockSpec((None, None, D, FF), lambda e, b: (layer, e, 0, 0)),
                  pl.BlockSpec((None, None, D, FF), lambda e, b: (layer, e, 0, 0)),
                  pl.BlockSpec((None, None, FF, D), lambda e, b: (layer, e, 0, 0))],
        out_specs=rows,
        out_shape=jax.ShapeDtypeStruct((B, E, cap, D), BF16),
        scratch_shapes=[pltpu.VMEM((D, FF), BF16), pltpu.VMEM((D, FF), BF16), pltpu.VMEM((FF, D), BF16)],
        compiler_params=_cparams(2),
        name="moe_ffn",
    )(xe, w_gate, w_up, w_down)


def _combine_kernel(bs_ref, h1_ref, ye_ref, pos_ref, gate_ref, fg_ref, out_ref, w_ref, y_ref,
                    *, n_sb, ts, sub, wn, cap, n_exp, final_norm):
    b = pl.program_id(0)
    tb = pl.program_id(1)
    lane_slot = lax.broadcasted_iota(jnp.int32, (wn, ts), 0)

    def window_start(first):
        return pl.multiple_of(jnp.minimum(first, cap - wn), 16)

    for s in range(sub):
        toks = slice(s * ts, (s + 1) * ts)
        sb = tb * sub + s

        def weights(e, start, first):
            pos = pos_ref[e:e + 1, toks]
            key = jnp.where(pos >= first, pos, -1)
            return jnp.where(key == start + lane_slot, gate_ref[e:e + 1, toks], 0.0).astype(BF16)

        firsts = []
        for e in range(n_exp):
            base = (b * n_exp + e) * (n_sb + 1)
            first = (bs_ref[base + sb] // 16) * 16
            start = window_start(first)
            firsts.append((first, bs_ref[base + sb + 1]))
            w_ref[s % 2, e * wn:(e + 1) * wn, :] = weights(e, start, first)
            y_ref[s % 2, e * wn:(e + 1) * wn, :] = ye_ref[e, pl.ds(start, wn), :]
        out_ref[toks, :] = h1_ref[toks, :] + _dot_tn(w_ref[s % 2], y_ref[s % 2])

        for e in range(n_exp):
            first0, end = firsts[e]
            n_more = jnp.maximum(end - first0 - 1, 0) // wn

            def window(w, carry):
                first = first0 + (w + 1) * wn
                start = window_start(first)
                out_ref[toks, :] += _dot_tn(weights(e, start, first), ye_ref[e, pl.ds(start, wn), :])
                return carry

            lax.fori_loop(0, n_more, window, 0)
        if final_norm:
            out_ref[toks, :] = _rms(out_ref[toks, :]) * fg_ref[...]


def _combine(bs, h1, ye, pos_t, gate_t, final_g, cap, tk, ts, wn, final_norm):
    B, S, D = h1.shape
    E = ye.shape[1]
    sub = tk // ts
    grid_spec = pltpu.PrefetchScalarGridSpec(
        num_scalar_prefetch=1,
        grid=(B, S // tk),
        in_specs=[
            pl.BlockSpec((None, tk, D), lambda b, t, bs: (b, t, 0)),
            pl.BlockSpec((None, E, cap, D), lambda b, t, bs: (b, 0, 0, 0), pipeline_mode=pl.Buffered(1)),
            pl.BlockSpec((None, None, E, tk), lambda b, t, bs: (b, t, 0, 0)),
            pl.BlockSpec((None, None, E, tk), lambda b, t, bs: (b, t, 0, 0)),
            pl.BlockSpec((1, D), lambda b, t, bs: (0, 0)),
        ],
        out_specs=pl.BlockSpec((None, tk, D), lambda b, t, bs: (b, t, 0)),
        scratch_shapes=[pltpu.VMEM((2, E * wn, ts), BF16), pltpu.VMEM((2, E * wn, D), BF16)],
    )
    return pl.pallas_call(
        functools.partial(_combine_kernel, n_sb=S // ts, ts=ts, sub=sub, wn=wn, cap=cap, n_exp=E,
                          final_norm=final_norm),
        grid_spec=grid_spec,
        out_shape=jax.ShapeDtypeStruct((B, S, D), F32),
        compiler_params=_cparams(2),
        name="combine",
    )(bs, h1, ye, pos_t, gate_t, final_g.reshape(1, D))


def _rope_tables(S):
    half = HEAD_DIM // 2
    inv_freq = ROPE_BASE ** (-jnp.arange(half, dtype=F32) / half)
    ang = jnp.arange(S).astype(F32)[:, None] * inv_freq[None, :]
    cos, sin = jnp.cos(ang), jnp.sin(ang)
    return jnp.concatenate([cos, cos], axis=-1), jnp.concatenate([-sin, sin], axis=-1)


def _block_starts(pos, tk, cap):
    B, E, _ = pos.shape
    bs = jnp.concatenate([pos[:, :, ::tk], jnp.full((B, E, 1), cap, jnp.int32)], axis=-1)
    return bs.reshape(-1)


def kernel(x, norm1_g, w_in, ret_norm_g, hgrn_norm_g, w_out, lower_bounds, norm2_g, w_router,
           w_gate, w_up, w_down, final_norm_g):
    B, S, D = x.shape
    depth = w_in.shape[0]
    E = w_router.shape[-1]
    cap = CAPACITY_FACTOR * S // E
    tm = min(512, S)
    tmo = min(1024, S)
    tk = min(1024, S)
    tkg = min(2048, S)
    ts = min(256, S)
    wn = min(64, cap)
    fm = min(512, cap)
    eg = min(8, E)

    lbs = jax.nn.softmax(lower_bounds.astype(F32), axis=0)
    lbs = jnp.cumsum(lbs, axis=0) - lbs[0]
    cos, sin = _rope_tables(S)

    h = x
    for layer in range(depth):
        proj = _inproj(h, norm1_g[layer], w_in, layer, cos, sin, tm)
        r_out = _retention(proj, ret_norm_g[layer])
        h_raw = _hgrn(proj, lbs[layer], first_layer=(layer == 0))
        h1, xn, aff = _outproj(r_out, h_raw, proj, hgrn_norm_g[layer], w_out, layer,
                               h, norm2_g[layer], w_router[layer].T.astype(BF16), tmo)
        pos, gate = _select(aff, cap)
        bs = _block_starts(pos, ts, cap)

        def by_block(a, t):
            return a.reshape(B, E, S // t, t).transpose(0, 2, 1, 3)

        xe = _moe_gather(bs, xn, by_block(pos, tkg), by_block(gate, tkg), cap, tkg, ts, wn, eg)
        pos_t = by_block(pos, tk)
        gate_t = by_block(gate, tk)
        ye = _moe_ffn(xe, w_gate, w_up, w_down, layer, fm)
        h = _combine(bs, h1, ye, pos_t, gate_t, final_norm_g, cap, tk, ts, wn,
                     final_norm=(layer == depth - 1))
    return h
```

```python
import functools

import numpy as np
import jax
import jax.numpy as jnp
from jax import lax
from jax.experimental import pallas as pl
from jax.experimental.pallas import tpu as pltpu

F32 = jnp.float32
BF16 = jnp.bfloat16

HEAD_DIM = 128
N_HEADS = 4
GROUP = N_HEADS * HEAD_DIM
N_GROUPS = 9
ROPE_BASE = 10000.0
NORM_EPS = 1e-6
CAPACITY_FACTOR = 2

RET_CHUNK = 256
RET_PAIR = 8
RET_LOCK = 2
HGRN_CHUNK = 128
HGRN_PAIR = 16
HGRN_LOCK = 2
HGRN_DESC = 16
LOG2E = 1.4426950408889634
VMEM_LIMIT = 58 * 1024 * 1024


def _cparams(n_axes):
    return pltpu.CompilerParams(
        dimension_semantics=("arbitrary",) * n_axes, vmem_limit_bytes=VMEM_LIMIT)


def _dot(a, b):
    return jnp.dot(a, b, preferred_element_type=F32)


def _dot_nt(a, b):
    return lax.dot_general(a, b, (((1,), (1,)), ((), ())), preferred_element_type=F32)


def _dot_tn(a, b):
    return lax.dot_general(a, b, (((0,), (0,)), ((), ())), preferred_element_type=F32)


def _silu(x):
    return x * (1.0 / (1.0 + jnp.exp(-x)))


def _rms(x):
    return x * lax.rsqrt(jnp.mean(x * x, axis=-1, keepdims=True) + NORM_EPS)


def _inproj_kernel(h_ref, g_ref, wf_ref, cos_ref, sin_ref, out_ref, w_ref):
    @pl.when(jnp.logical_and(pl.program_id(0) == 0, pl.program_id(1) == 0))
    def _():
        w_ref[...] = wf_ref[...].astype(BF16)

    x = h_ref[...]
    inv_rms = lax.rsqrt(jnp.mean(x * x, axis=-1, keepdims=True) + NORM_EPS)
    xg = (x * g_ref[...]).astype(BF16)
    cos = cos_ref[...]
    sin = sin_ref[...]
    scale = HEAD_DIM ** -0.5
    for j in range(N_GROUPS):
        acc = _dot(xg, w_ref[:, j * GROUP:(j + 1) * GROUP]) * inv_rms
        if j in (0, 1):
            for hh in range(N_HEADS):
                sl = acc[:, hh * HEAD_DIM:(hh + 1) * HEAD_DIM]
                rot = sl * cos + pltpu.roll(sl, HEAD_DIM // 2, 1) * sin
                if j == 1:
                    rot = rot * scale
                out_ref[:, j * GROUP + hh * HEAD_DIM:j * GROUP + (hh + 1) * HEAD_DIM] = rot.astype(BF16)
            continue
        if j in (3, 8):
            acc = _silu(acc)
        elif j == 4:
            acc = _silu(acc) * scale
        out_ref[:, j * GROUP:(j + 1) * GROUP] = acc.astype(BF16)


def _inproj(h, g, w_in, layer, cos, sin, tm):
    B, S, D = h.shape
    ncol = w_in.shape[-1]
    return pl.pallas_call(
        _inproj_kernel,
        grid=(B, S // tm),
        in_specs=[
            pl.BlockSpec((None, tm, D), lambda b, i: (b, i, 0)),
            pl.BlockSpec((1, D), lambda b, i: (0, 0)),
            pl.BlockSpec((None, D, ncol), lambda b, i: (layer, 0, 0), pipeline_mode=pl.Buffered(1)),
            pl.BlockSpec((tm, HEAD_DIM), lambda b, i: (i, 0)),
            pl.BlockSpec((tm, HEAD_DIM), lambda b, i: (i, 0)),
        ],
        out_specs=pl.BlockSpec((None, tm, ncol), lambda b, i: (b, i, 0)),
        out_shape=jax.ShapeDtypeStruct((B, S, ncol), BF16),
        scratch_shapes=[pltpu.VMEM((D, ncol), BF16)],
        compiler_params=_cparams(2),
        name="inproj",
    )(h, g.reshape(1, D), w_in, cos, sin)


def _ret_kernel(q_ref, k_ref, v_ref, g_ref, dmat_ref, qdec_ref, kdec_ref, cdec_ref, rg_ref,
                out_ref, acc_ref, *, n_chunks):
    L = RET_CHUNK
    cdec = cdec_ref[0:1, :]
    zero_state = jnp.zeros((HEAD_DIM, HEAD_DIM), F32)

    def rows(i):
        return pl.ds(pl.multiple_of(i * L, L), L)

    def finish(sl, o):
        y = _rms(o) * rg_ref[...] * g_ref[sl, :].astype(F32)
        out_ref[sl, :] = y.astype(BF16)

    def step(i, states, second):
        state_f, state_b = states
        sfs = [rows(i * RET_PAIR + j) for j in range(RET_PAIR)]
        sbs = [rows(n_chunks - 1 - i * RET_PAIR - j) for j in range(RET_PAIR)]
        qdf, qdb, upd_f, upd_b, intra = [], [], [], [], []
        for a in range(0, RET_PAIR, RET_LOCK):
            grp = slice(a, a + RET_LOCK)
            qf, kf, vf = ([r[s, :] for s in sfs[grp]] for r in (q_ref, k_ref, v_ref))
            qb, kb, vb = ([r[s, :] for s in sbs[grp]] for r in (q_ref, k_ref, v_ref))
            raw = [_dot_nt(q, k) for q, k in zip(qf, kf)]
            qdf += [(q.astype(F32) * qdec_ref[...]).astype(BF16) for q in qf]
            kdf = [(k.astype(F32) * kdec_ref[...]).astype(BF16) for k in kf]
            qdb += [(q.astype(F32) * kdec_ref[...]).astype(BF16) for q in qb]
            kdb = [(k.astype(F32) * qdec_ref[...]).astype(BF16) for k in kb]
            upd_f += [_dot_tn(k, v) for k, v in zip(kdf, vf)]
            upd_b += [_dot_tn(k, v) for k, v in zip(kdb, vb)]
            intra += [_dot((s * dmat_ref[...]).astype(BF16), v) for s, v in zip(raw, vf)]
        for j in range(RET_PAIR):
            of = intra[j] + _dot(qdf[j], state_f.astype(BF16))
            ob = _dot(qdb[j], state_b.astype(BF16))
            state_f = cdec * state_f + upd_f[j]
            state_b = cdec * state_b + upd_b[j]
            if second:
                finish(sfs[j], acc_ref[sfs[j], :] + of)
                finish(sbs[j], acc_ref[sbs[j], :] + ob)
            else:
                acc_ref[sfs[j], :] = of
                acc_ref[sbs[j], :] = ob
        return state_f, state_b

    steps = n_chunks // RET_PAIR
    states = lax.fori_loop(0, steps // 2, functools.partial(step, second=False),
                           (zero_state, zero_state))
    lax.fori_loop(steps // 2, steps, functools.partial(step, second=True), states)


def _ret_tables():
    L = RET_CHUNK
    t = np.arange(L, dtype=np.float64)
    lg = np.log1p(-(2.0 ** (-5.0 - np.arange(N_HEADS, dtype=np.float64))))
    dmat = np.exp(lg[:, None, None] * np.abs(t[:, None] - t[None, :])[None])
    qdec = np.exp(lg[:, None] * (t + 1.0))[:, :, None] * np.ones((1, 1, HEAD_DIM))
    kdec = np.exp(lg[:, None] * (L - 1.0 - t))[:, :, None] * np.ones((1, 1, HEAD_DIM))
    cdec = np.exp(lg * L)[:, None, None] * np.ones((1, 8, HEAD_DIM))
    return tuple(jnp.asarray(a, F32) for a in (dmat, qdec, kdec, cdec))


def _retention(proj, ret_g):
    B, S, _ = proj.shape
    L = RET_CHUNK
    assert S % (2 * RET_PAIR * L) == 0, "retention walks chunks from both ends, RET_PAIR at a time"
    dmat, qdec, kdec, cdec = _ret_tables()

    def col(c0):
        return pl.BlockSpec((None, S, HEAD_DIM), lambda b, h: (b, 0, c0 + h))

    def tab(r):
        return pl.BlockSpec((None, r, HEAD_DIM), lambda b, h: (h, 0, 0))

    return pl.pallas_call(
        functools.partial(_ret_kernel, n_chunks=S // L),
        grid=(B, N_HEADS),
        in_specs=[col(0), col(N_HEADS), col(2 * N_HEADS), col(3 * N_HEADS),
                  pl.BlockSpec((None, L, L), lambda b, h: (h, 0, 0)),
                  tab(L), tab(L), tab(8),
                  pl.BlockSpec((1, HEAD_DIM), lambda b, h: (0, h))],
        out_specs=pl.BlockSpec((None, S, HEAD_DIM), lambda b, h: (b, 0, h)),
        out_shape=jax.ShapeDtypeStruct((B, S, GROUP), BF16),
        scratch_shapes=[pltpu.VMEM((S, HEAD_DIM), F32)],
        compiler_params=_cparams(2),
        name="retention",
    )(proj, proj, proj, proj, dmat, qdec, kdec, cdec, ret_g.reshape(1, GROUP))


def _hgrn_gates(z, lb, first_layer):
    zf = z.astype(F32)
    e = jnp.exp(-jnp.abs(zf))
    inv = 1.0 / (1.0 + e)
    pos = zf >= 0.0
    sigm = jnp.where(pos, e * inv, inv)
    if first_layer:
        return jnp.minimum(zf, 0.0) * LOG2E - jnp.log2(1.0 + e), sigm
    sig = jnp.where(pos, inv, e * inv)
    return jnp.log2(lb + (1.0 - lb) * sig), (1.0 - lb) * sigm


def _hgrn_intra(chunks, lb, mtab_f, mtab_b, lvl, first_layer):
    L = HGRN_CHUNK
    n = len(chunks)
    qfs = [c[0].astype(F32) for c in chunks]
    gates_f = [_hgrn_gates(c[1], lb, first_layer) for c in chunks]
    gates_b = [_hgrn_gates(c[2], lb, first_layer) for c in chunks]

    def exponents(gates, mtab):
        splits = []
        for logf2, _ in gates:
            hi = logf2.astype(BF16)
            splits.append(jnp.concatenate([hi, (logf2 - hi.astype(F32)).astype(BF16)], axis=0))
        decs = []
        for a in range(0, n, 2):
            both = _dot(mtab, jnp.concatenate(splits[a:a + 2], axis=1))
            decs += [both[:, j * HEAD_DIM:(j + 1) * HEAD_DIM] for j in range(len(splits[a:a + 2]))]
        return decs

    decs_f = exponents(gates_f, mtab_f)
    decs_b = exponents(gates_b, mtab_b)
    cfs = [d[0:L, :] for d in decs_f]
    cbs = [d[0:L, :] for d in decs_b]
    kfs = [g[1] for g in gates_f]
    kbs = [g[1] for g in gates_b]

    scores = [jnp.where(lvl == -1, _dot_nt(chunks[i][0], (kfs[i] + kbs[i]).astype(BF16)), 0.0)
              for i in range(n)]
    m, level = 1, 0
    while m < L:
        for i in range(n):
            if m < 8:
                def as3(x):
                    return x.reshape(L // 8, 8, HEAD_DIM)
                pf = as3(jnp.exp2(decs_f[i][(1 + level) * L:(2 + level) * L, :]))
                pb = as3(jnp.exp2(decs_b[i][(1 + level) * L:(2 + level) * L, :]))
                sub = lax.broadcasted_iota(jnp.int32, (1, 8, HEAD_DIM), 1)
                upper = ((sub >> level) & 1) == 1
                qside = (as3(qfs[i]) * jnp.where(upper, pf, pb)).reshape(L, HEAD_DIM)
                kside = jnp.where(upper, as3(kbs[i]) * pb, as3(kfs[i]) * pf).reshape(L, HEAD_DIM)
            else:
                qslabs, kslabs = [], []
                for j in range(L // m):
                    mid = (j // 2) * 2 * m + m
                    sl = slice(j * m, (j + 1) * m)
                    if j % 2 == 1:
                        qdec = jnp.exp2(cfs[i][sl, :] - cfs[i][mid - 1:mid, :])
                        kslabs.append(kbs[i][sl, :] * jnp.exp2(cbs[i][mid:mid + 1, :] - cbs[i][sl, :]))
                    else:
                        qdec = jnp.exp2(cbs[i][sl, :] - cbs[i][mid:mid + 1, :])
                        kslabs.append(kfs[i][sl, :] * jnp.exp2(cfs[i][mid - 1:mid, :] - cfs[i][sl, :]))
                    qslabs.append(qfs[i][sl, :] * qdec)
                qside = jnp.concatenate(qslabs, axis=0)
                kside = jnp.concatenate(kslabs, axis=0)
            s = _dot_nt(qside.astype(BF16), kside.astype(BF16))
            scores[i] = jnp.where(lvl == level, s, scores[i])
        m, level = 2 * m, level + 1

    results = []
    for i in range(n):
        intra = _dot(scores[i].astype(BF16), chunks[i][3])
        cf_end = cfs[i][L - 1:L, :]
        cb_end = cbs[i][0:1, :]
        fwd = ((qfs[i] * jnp.exp2(cfs[i])).astype(BF16),
               (kfs[i] * jnp.exp2(cf_end - cfs[i])).astype(BF16), jnp.exp2(cf_end))
        bwd = ((qfs[i] * jnp.exp2(cbs[i])).astype(BF16),
               (kbs[i] * jnp.exp2(cb_end - cbs[i])).astype(BF16), jnp.exp2(cb_end))
        results.append((intra, fwd, bwd))
    return results


def _hgrn_kernel(q_ref, zf_ref, zb_ref, v_ref, lb_ref, mtabf_ref, mtabb_ref, lvl_ref,
                 out_ref, qeb_ref, keb_ref, dend_ref, *, n_chunks, first_layer):
    L = HGRN_CHUNK
    lb = lb_ref[...]
    zero_state = jnp.zeros((HEAD_DIM, HEAD_DIM), F32)

    def rows(c):
        return pl.ds(pl.multiple_of(c * L, L), L)

    def ascend(i, state):
        cs = [i * HGRN_PAIR + j for j in range(HGRN_PAIR)]
        sls = [rows(c) for c in cs]
        res = []
        for a in range(0, HGRN_PAIR, HGRN_LOCK):
            res += _hgrn_intra([(q_ref[sl, :], zf_ref[sl, :], zb_ref[sl, :], v_ref[sl, :])
                                for sl in sls[a:a + HGRN_LOCK]],
                               lb, mtabf_ref[...], mtabb_ref[...], lvl_ref[...], first_layer)
        updates = [_dot_tn(v_ref[sl, :], r[1][1]) for sl, r in zip(sls, res)]
        for c, sl, (intra, (qe_f, _, dend_f), (qe_b, ke_b, dend_b)), upd in zip(cs, sls, res, updates):
            out_ref[sl, :] = intra + _dot_nt(qe_f, state.astype(BF16))
            state = dend_f * state + upd
            qeb_ref[sl, :] = qe_b
            keb_ref[sl, :] = ke_b
            dend_ref[c] = jnp.broadcast_to(dend_b, (8, HEAD_DIM))
        return state

    lax.fori_loop(0, n_chunks // HGRN_PAIR, ascend, zero_state)

    def descend(i, state):
        cs = [n_chunks - 1 - i * HGRN_DESC - j for j in range(HGRN_DESC)]
        sls = [rows(c) for c in cs]
        updates = [_dot_tn(v_ref[sl, :], keb_ref[sl, :]) for sl in sls]
        for c, sl, upd in zip(cs, sls, updates):
            out_ref[sl, :] += _dot_nt(qeb_ref[sl, :], state.astype(BF16))
            state = dend_ref[c][0:1, :] * state + upd
        return state

    lax.fori_loop(0, n_chunks // HGRN_DESC, descend, zero_state)


def _hgrn_tables():
    L = HGRN_CHUNK
    t = np.arange(L)[:, None]
    u = np.arange(L)[None, :]
    x = t ^ u
    lvl = np.where(x == 0, -1, np.floor(np.log2(np.maximum(x, 1)))).astype(np.int32)

    def exponent_rows(backward):
        blocks = [(u >= t) if backward else (u <= t)]
        m = 1
        while m < 8:
            mid = (t // (2 * m)) * (2 * m) + m
            if backward:
                blocks.append(np.where(t < mid, (u >= t) & (u < mid), (u >= mid) & (u < t)))
            else:
                blocks.append(np.where(t >= mid, (u >= mid) & (u <= t), (u > t) & (u < mid)))
            m *= 2
        tab = np.concatenate(blocks, axis=0).astype(np.float32)
        return np.concatenate([tab, tab], axis=1)

    return (jnp.asarray(exponent_rows(False), BF16), jnp.asarray(exponent_rows(True), BF16),
            jnp.asarray(lvl, jnp.int32))


def _hgrn(proj, lb, first_layer):
    B, S, _ = proj.shape
    L = HGRN_CHUNK
    n_chunks = S // L
    assert S % L == 0 and n_chunks % HGRN_PAIR == 0 and n_chunks % HGRN_DESC == 0
    assert HGRN_PAIR % HGRN_LOCK == 0
    mtab_f, mtab_b, lvl = _hgrn_tables()

    def col(g):
        return pl.BlockSpec((None, S, HEAD_DIM), lambda b, h: (b, 0, g * N_HEADS + h))

    def full(a):
        return pl.BlockSpec(a.shape, lambda b, h: (0, 0))

    return pl.pallas_call(
        functools.partial(_hgrn_kernel, n_chunks=n_chunks, first_layer=first_layer),
        grid=(B, N_HEADS),
        in_specs=[col(4), col(5), col(6), col(7),
                  pl.BlockSpec((1, HEAD_DIM), lambda b, h: (0, h)),
                  full(mtab_f), full(mtab_b), full(lvl)],
        out_specs=pl.BlockSpec((None, S, HEAD_DIM), lambda b, h: (b, 0, h)),
        out_shape=jax.ShapeDtypeStruct((B, S, GROUP), F32),
        scratch_shapes=[pltpu.VMEM((S, HEAD_DIM), BF16), pltpu.VMEM((S, HEAD_DIM), BF16),
                        pltpu.VMEM((n_chunks, 8, HEAD_DIM), F32)],
        compiler_params=_cparams(2),
        name="hgrn",
    )(proj, proj, proj, proj, lb.reshape(1, GROUP), mtab_f, mtab_b, lvl)


def _outproj_kernel(r_ref, hraw_ref, hg_ref, hgn_ref, wf_ref, h_ref, g2_ref, wr_ref,
                    h1_ref, xn_ref, aff_ref, w_ref):
    @pl.when(jnp.logical_and(pl.program_id(0) == 0, pl.program_id(1) == 0))
    def _():
        w_ref[...] = wf_ref[...].astype(BF16)

    hn = _rms(hraw_ref[...]) * hgn_ref[...] * hg_ref[...].astype(F32)
    mix = _dot(r_ref[...], w_ref[0:GROUP, :]) + _dot(hn.astype(BF16), w_ref[GROUP:2 * GROUP, :])
    h1 = h_ref[...] + mix
    h1_ref[...] = h1
    xn = (_rms(h1) * g2_ref[...]).astype(BF16)
    xn_ref[...] = xn
    logits = _dot_nt(wr_ref[...], xn)
    mx = jnp.max(logits, axis=0, keepdims=True)
    ex = jnp.exp(logits - mx)
    aff_ref[...] = ex / jnp.sum(ex, axis=0, keepdims=True)


def _outproj(r_out, h_raw, proj, hgrn_g, w_out, layer, h, g2, wr_t_bf, tm):
    B, S, D = h.shape
    E = wr_t_bf.shape[0]
    return pl.pallas_call(
        _outproj_kernel,
        grid=(B, S // tm),
        in_specs=[
            pl.BlockSpec((None, tm, GROUP), lambda b, i: (b, i, 0)),
            pl.BlockSpec((None, tm, GROUP), lambda b, i: (b, i, 0)),
            pl.BlockSpec((None, tm, GROUP), lambda b, i: (b, i, N_GROUPS - 1)),
            pl.BlockSpec((1, GROUP), lambda b, i: (0, 0)),
            pl.BlockSpec((None, 2 * GROUP, D), lambda b, i: (layer, 0, 0), pipeline_mode=pl.Buffered(1)),
            pl.BlockSpec((None, tm, D), lambda b, i: (b, i, 0)),
            pl.BlockSpec((1, D), lambda b, i: (0, 0)),
            pl.BlockSpec((E, D), lambda b, i: (0, 0)),
        ],
        out_specs=[
            pl.BlockSpec((None, tm, D), lambda b, i: (b, i, 0)),
            pl.BlockSpec((None, tm, D), lambda b, i: (b, i, 0)),
            pl.BlockSpec((None, E, tm), lambda b, i: (b, 0, i)),
        ],
        out_shape=[jax.ShapeDtypeStruct((B, S, D), F32),
                   jax.ShapeDtypeStruct((B, S, D), BF16),
                   jax.ShapeDtypeStruct((B, E, S), F32)],
        scratch_shapes=[pltpu.VMEM((2 * GROUP, D), BF16)],
        compiler_params=_cparams(2),
        name="outproj",
    )(r_out, h_raw, proj, hgrn_g.reshape(1, GROUP), w_out, h, g2.reshape(1, D), wr_t_bf)


def _select_kernel(aff_ref, pos_ref, gate_ref, *, cap):
    a = aff_ref[...]
    E, S = a.shape
    u = pltpu.bitcast(a, jnp.int32)
    capf = jnp.float32(cap)

    def count(mask):
        return jnp.sum(jnp.where(mask, 1.0, 0.0), axis=-1, keepdims=True)

    def pick(base, low, ok):
        c1, c2 = base | low, base | (low << 1)
        c3 = c2 | low
        return jnp.where(ok(c3), c3, jnp.where(ok(c2), c2, jnp.where(ok(c1), c1, base)))

    def value_bits(i, thr):
        return pick(thr, jnp.int32(1) << (29 - 2 * i), lambda c: count(u >= c) >= capf)

    thr = lax.fori_loop(0, 15, value_bits, jnp.zeros((E, 1), jnp.int32))
    thr = jnp.where(count(u >= (thr | 1)) >= capf, thr | 1, thr)
    gt = u > thr
    eq = u == thr
    need = capf - count(gt)
    idx = lax.broadcasted_iota(jnp.int32, (E, S), 1)
    nbits = int(S).bit_length()

    def few_enough(c):
        return count(eq & (idx < c)) <= need

    cut = jnp.zeros((E, 1), jnp.int32)
    top = nbits
    if nbits % 2:
        top -= 1
        cut = jnp.where(few_enough(cut | (1 << top)), cut | (1 << top), cut)

    def index_bits(i, cut):
        return pick(cut, jnp.int32(1) << (top - 2 - 2 * i), few_enough)

    cut = lax.fori_loop(0, top // 2, index_bits, cut)
    sel = gt | (eq & (idx < cut))
    gate_ref[...] = jnp.where(sel, a, 0.0)

    li = lax.broadcasted_iota(jnp.int32, (128, 128), 0)
    lj = lax.broadcasted_iota(jnp.int32, (128, 128), 1)
    upper = jnp.where(li <= lj, 1.0, 0.0).astype(BF16)
    carry = jnp.zeros((E, 1), F32)
    for j in range(S // 128):
        sb = jnp.where(sel[:, j * 128:(j + 1) * 128], 1.0, 0.0)
        incl = _dot(sb.astype(BF16), upper)
        pos_ref[:, j * 128:(j + 1) * 128] = (incl - sb + carry).astype(jnp.int32)
        carry = carry + incl[:, 127:128]


def _select(aff, cap):
    B, E, S = aff.shape
    spec = pl.BlockSpec((None, E, S), lambda b: (b, 0, 0))
    return pl.pallas_call(
        functools.partial(_select_kernel, cap=cap),
        grid=(B,),
        in_specs=[spec],
        out_specs=[spec, spec],
        out_shape=[jax.ShapeDtypeStruct((B, E, S), jnp.int32),
                   jax.ShapeDtypeStruct((B, E, S), F32)],
        compiler_params=_cparams(1),
        name="select",
    )(aff)


def _gather_kernel(bs_ref, x_ref, pos_ref, gate_ref, xe_ref, oh_ref,
                   *, n_sb, ts, sub, wn, cap, n_exp, eg):
    b = pl.program_id(0)
    grp = pl.program_id(1)
    tb = pl.program_id(2)
    lane_slot = lax.broadcasted_iota(jnp.int32, (wn, ts), 0)

    @pl.when(tb == 0)
    def _():
        xe_ref[...] = jnp.zeros_like(xe_ref)

    def window_start(first):
        return pl.multiple_of(jnp.minimum(first, cap - wn), 16)

    for s in range(sub):
        toks = slice(s * ts, (s + 1) * ts)
        sb = tb * sub + s

        def onehot(el, start, first):
            pos = pos_ref[el:el + 1, toks]
            wanted = jnp.where(gate_ref[el:el + 1, toks] > 0.0, pos, -1)
            key = jnp.where(pos >= first, wanted, -1)
            return jnp.where(key == start + lane_slot, 1.0, 0.0).astype(BF16)

        spans = []
        for el in range(eg):
            base = (b * n_exp + grp * eg + el) * (n_sb + 1)
            first = (bs_ref[base + sb] // 16) * 16
            start = window_start(first)
            spans.append((first, start, bs_ref[base + sb + 1]))
            oh_ref[s, el * wn:(el + 1) * wn, :] = onehot(el, start, first)
        rows = _dot(oh_ref[s], x_ref[toks, :]).astype(BF16)
        for el in range(eg):
            xe_ref[el, pl.ds(spans[el][1], wn), :] += rows[el * wn:(el + 1) * wn, :]

        for el in range(eg):
            first0, _, end = spans[el]
            n_more = jnp.maximum(end - first0 - 1, 0) // wn

            def window(w, carry):
                first = first0 + (w + 1) * wn
                start = window_start(first)
                xe_ref[el, pl.ds(start, wn), :] += _dot(onehot(el, start, first),
                                                        x_ref[toks, :]).astype(BF16)
                return carry

            lax.fori_loop(0, n_more, window, 0)


def _moe_gather(bs, xn, pos_t, gate_t, cap, tk, ts, wn, eg):
    B, S, D = xn.shape
    E = pos_t.shape[2]
    grid_spec = pltpu.PrefetchScalarGridSpec(
        num_scalar_prefetch=1,
        grid=(B, E // eg, S // tk),
        in_specs=[
            pl.BlockSpec((None, tk, D), lambda b, g, t, bs: (b, t, 0)),
            pl.BlockSpec((None, None, eg, tk), lambda b, g, t, bs: (b, t, g, 0)),
            pl.BlockSpec((None, None, eg, tk), lambda b, g, t, bs: (b, t, g, 0)),
        ],
        out_specs=pl.BlockSpec((None, eg, cap, D), lambda b, g, t, bs: (b, g, 0, 0)),
        scratch_shapes=[pltpu.VMEM((tk // ts, eg * wn, ts), BF16)],
    )
    return pl.pallas_call(
        functools.partial(_gather_kernel, n_sb=S // ts, ts=ts, sub=tk // ts, wn=wn, cap=cap,
                          n_exp=E, eg=eg),
        grid_spec=grid_spec,
        out_shape=jax.ShapeDtypeStruct((B, E, cap, D), BF16),
        compiler_params=_cparams(3),
        name="moe_gather",
    )(bs, xn, pos_t, gate_t)


def _ffn_kernel(xe_ref, wg_ref, wu_ref, wd_ref, ye_ref, wg_bf, wu_bf, wd_bf, *, fm):
    @pl.when(pl.program_id(1) == 0)
    def _():
        wg_bf[...] = wg_ref[...].astype(BF16)
        wu_bf[...] = wu_ref[...].astype(BF16)
        wd_bf[...] = wd_ref[...].astype(BF16)

    for r in range(xe_ref.shape[0] // fm):
        xe = xe_ref[r * fm:(r + 1) * fm, :]
        hid = (_silu(_dot(xe, wg_bf[...])) * _dot(xe, wu_bf[...])).astype(BF16)
        ye_ref[r * fm:(r + 1) * fm, :] = _dot(hid, wd_bf[...]).astype(BF16)


def _moe_ffn(xe, w_gate, w_up, w_down, layer, fm):
    B, E, cap, D = xe.shape
    FF = w_gate.shape[-1]
    rows = pl.BlockSpec((None, None, cap, D), lambda e, b: (b, e, 0, 0))
    return pl.pallas_call(
        functools.partial(_ffn_kernel, fm=fm),
        grid=(E, B),
        in_specs=[rows,
                  pl.BlockSpec((None, None, D, FF), lambda e, b: (layer, e, 0, 0)),
                  pl.BlockSpec((None, None, D, FF), lambda e, b: (layer, e, 0, 0)),
                  pl.BlockSpec((None, None, FF, D), lambda e, b: (layer, e, 0, 0))],
        out_specs=rows,
        out_shape=jax.ShapeDtypeStruct((B, E, cap, D), BF16),
        scratch_shapes=[pltpu.VMEM((D, FF), BF16), pltpu.VMEM((D, FF), BF16), pltpu.VMEM((FF, D), BF16)],
        compiler_params=_cparams(2),
        name="moe_ffn",
    )(xe, w_gate, w_up, w_down)


def _combine_kernel(bs_ref, h1_ref, ye_ref, pos_ref, gate_ref, fg_ref, out_ref, w_ref, y_ref,
                    *, n_sb, ts, sub, wn, cap, n_exp, final_norm):
    b = pl.program_id(0)
    tb = pl.program_id(1)
    lane_slot = lax.broadcasted_iota(jnp.int32, (wn, ts), 0)

    def window_start(first):
        return pl.multiple_of(jnp.minimum(first, cap - wn), 16)

    for s in range(sub):
        toks = slice(s * ts, (s + 1) * ts)
        sb = tb * sub + s

        def weights(e, start, first):
            pos = pos_ref[e:e + 1, toks]
            key = jnp.where(pos >= first, pos, -1)
            return jnp.where(key == start + lane_slot, gate_ref[e:e + 1, toks], 0.0).astype(BF16)

        firsts = []
        for e in range(n_exp):
            base = (b * n_exp + e) * (n_sb + 1)
            first = (bs_ref[base + sb] // 16) * 16
            start = window_start(first)
            firsts.append((first, bs_ref[base + sb + 1]))
            w_ref[s % 2, e * wn:(e + 1) * wn, :] = weights(e, start, first)
            y_ref[s % 2, e * wn:(e + 1) * wn, :] = ye_ref[e, pl.ds(start, wn), :]
        out_ref[toks, :] = h1_ref[toks, :] + _dot_tn(w_ref[s % 2], y_ref[s % 2])

        for e in range(n_exp):
            first0, end = firsts[e]
            n_more = jnp.maximum(end - first0 - 1, 0) // wn

            def window(w, carry):
                first = first0 + (w + 1) * wn
                start = window_start(first)
                out_ref[toks, :] += _dot_tn(weights(e, start, first), ye_ref[e, pl.ds(start, wn), :])
                return carry

            lax.fori_loop(0, n_more, window, 0)
        if final_norm:
            out_ref[toks, :] = _rms(out_ref[toks, :]) * fg_ref[...]


def _combine(bs, h1, ye, pos_t, gate_t, final_g, cap, tk, ts, wn, final_norm):
    B, S, D = h1.shape
    E = ye.shape[1]
    sub = tk // ts
    grid_spec = pltpu.PrefetchScalarGridSpec(
        num_scalar_prefetch=1,
        grid=(B, S // tk),
        in_specs=[
            pl.BlockSpec((None, tk, D), lambda b, t, bs: (b, t, 0)),
            pl.BlockSpec((None, E, cap, D), lambda b, t, bs: (b, 0, 0, 0), pipeline_mode=pl.Buffered(1)),
            pl.BlockSpec((None, None, E, tk), lambda b, t, bs: (b, t, 0, 0)),
            pl.BlockSpec((None, None, E, tk), lambda b, t, bs: (b, t, 0, 0)),
            pl.BlockSpec((1, D), lambda b, t, bs: (0, 0)),
        ],
        out_specs=pl.BlockSpec((None, tk, D), lambda b, t, bs: (b, t, 0)),
        scratch_shapes=[pltpu.VMEM((2, E * wn, ts), BF16), pltpu.VMEM((2, E * wn, D), BF16)],
    )
    return pl.pallas_call(
        functools.partial(_combine_kernel, n_sb=S // ts, ts=ts, sub=sub, wn=wn, cap=cap, n_exp=E,
                          final_norm=final_norm),
        grid_spec=grid_spec,
        out_shape=jax.ShapeDtypeStruct((B, S, D), F32),
        compiler_params=_cparams(2),
        name="combine",
    )(bs, h1, ye, pos_t, gate_t, final_g.reshape(1, D))


def _rope_tables(S):
    half = HEAD_DIM // 2
    inv_freq = ROPE_BASE ** (-jnp.arange(half, dtype=F32) / half)
    ang = jnp.arange(S).astype(F32)[:, None] * inv_freq[None, :]
    cos, sin = jnp.cos(ang), jnp.sin(ang)
    return jnp.concatenate([cos, cos], axis=-1), jnp.concatenate([-sin, sin], axis=-1)


def _block_starts(pos, tk, cap):
    B, E, _ = pos.shape
    bs = jnp.concatenate([pos[:, :, ::tk], jnp.full((B, E, 1), cap, jnp.int32)], axis=-1)
    return bs.reshape(-1)


def kernel(x, norm1_g, w_in, ret_norm_g, hgrn_norm_g, w_out, lower_bounds, norm2_g, w_router,
           w_gate, w_up, w_down, final_norm_g):
    B, S, D = x.shape
    depth = w_in.shape[0]
    E = w_router.shape[-1]
    cap = CAPACITY_FACTOR * S // E
    tm = min(512, S)
    tmo = min(1024, S)
    tk = min(1024, S)
    tkg = min(2048, S)
    ts = min(256, S)
    wn = min(64, cap)
    fm = min(512, cap)
    eg = min(8, E)

    lbs = jax.nn.softmax(lower_bounds.astype(F32), axis=0)
    lbs = jnp.cumsum(lbs, axis=0) - lbs[0]
    cos, sin = _rope_tables(S)

    h = x
    for layer in range(depth):
        proj = _inproj(h, norm1_g[layer], w_in, layer, cos, sin, tm)
        r_out = _retention(proj, ret_norm_g[layer])
        h_raw = _hgrn(proj, lbs[layer], first_layer=(layer == 0))
        h1, xn, aff = _outproj(r_out, h_raw, proj, hgrn_norm_g[layer], w_out, layer,
                               h, norm2_g[layer], w_router[layer].T.astype(BF16), tmo)
        pos, gate = _select(aff, cap)
        bs = _block_starts(pos, ts, cap)

        def by_block(a, t):
            return a.reshape(B, E, S // t, t).transpose(0, 2, 1, 3)

        xe = _moe_gather(bs, xn, by_block(pos, tkg), by_block(gate, tkg), cap, tkg, ts, wn, eg)
        pos_t = by_block(pos, tk)
        gate_t = by_block(gate, tk)
        ye = _moe_ffn(xe, w_gate, w_up, w_down, layer, fm)
        h = _combine(bs, h1, ye, pos_t, gate_t, final_norm_g, cap, tk, ts, wn,
                     final_norm=(layer == depth - 1))
    return h
```
